```python
import math
import jax, jax.numpy as jnp
from jax import lax
import numpy as np

D_MODEL = 1024
BATCH = 2
SEQ = 16384
DEPTH = 4

CTX_LEN = 256
GRID_W = 64

F32 = jnp.float32
EPS = 1e-6

N_EVEN = (DEPTH + 1) // 2
N_ODD = DEPTH // 2

FFN_DIM = 2816

GDN_HEAD_DIM = 128
GDN_WIDTH = D_MODEL // 2
GDN_HEADS = GDN_WIDTH // GDN_HEAD_DIM
GDN_CONV = 5
GDN_CHUNK = 64

S5_WIDTH = D_MODEL - GDN_WIDTH
S5_GROUP = 16
S5_GROUPS = S5_WIDTH // S5_GROUP
S5_STATE = 64

CONV_WIDTH = D_MODEL // 2
CONV_K = 31

HGRN_KEY_DIM = 128
HGRN_VAL_DIM = 128
HGRN_HEADS = (D_MODEL - CONV_WIDTH) // HGRN_VAL_DIM
HGRN_KW = HGRN_HEADS * HGRN_KEY_DIM
HGRN_VW = HGRN_HEADS * HGRN_VAL_DIM
HGRN_CHUNK = 16

GDN_IN = 4 * GDN_WIDTH + 4 * GDN_HEADS
AB_IN = GDN_IN + S5_WIDTH
AB_OUT = GDN_WIDTH + S5_WIDTH
CONV_IN = 2 * CONV_WIDTH
CD_IN = CONV_IN + 3 * HGRN_KW + 2 * HGRN_VW
CD_OUT = CONV_WIDTH + HGRN_VW

kernel_name = "hybrid_gdn_s5_conformer_hgrn2_prefix_dit"


def rmsnorm(x, g):
    xf = x.astype(F32)
    y = xf * lax.rsqrt(jnp.mean(xf * xf, axis=-1, keepdims=True) + EPS)
    return (y * g.astype(F32)).astype(x.dtype)


def layernorm(x, g, b):
    xf = x.astype(F32)
    xc = xf - jnp.mean(xf, axis=-1, keepdims=True)
    y = xc * lax.rsqrt(jnp.mean(xc * xc, axis=-1, keepdims=True) + EPS)
    return (y * g.astype(F32) + b.astype(F32)).astype(x.dtype)


def l2norm(x):
    return x * lax.rsqrt(jnp.sum(x * x, axis=-1, keepdims=True) + EPS)


def ada_norm(x, g, mod, j):
    return rmsnorm(x, g) * (1.0 + mod[:, j, 1, None]) + mod[:, j, 0, None]


def swiglu(h, w_up, w_down):
    gt, up = jnp.split(h @ w_up, 2, axis=-1)
    return (jax.nn.silu(gt) * up) @ w_down


def to_heads(t, n_heads):
    b, L, w = t.shape
    return jnp.transpose(t.reshape(b, L, n_heads, w // n_heads), (0, 2, 1, 3))


def rev(t):
    return jnp.flip(t, axis=2)


def head_out_norm(o, gate, g):
    b, H, L, d = o.shape
    o = jnp.transpose(o, (0, 2, 1, 3))
    o = o * lax.rsqrt(jnp.mean(o * o, axis=-1, keepdims=True) + EPS) * g.astype(F32)
    o = o * jax.nn.silu(gate.astype(F32).reshape(b, L, H, d))
    return o.reshape(b, L, H * d).astype(gate.dtype)


def dwconv1d(x, w):
    k = w.shape[0]
    return lax.conv_general_dilated(x, w[:, None, :], (1,), [(k // 2, k // 2)],
                                    dimension_numbers=('NWC', 'WIO', 'NWC'),
                                    feature_group_count=x.shape[-1])


def axial_dwconv(x, w, rows):
    b, L, ch = x.shape
    half = ch // 2
    pad = w.shape[0] // 2
    img = x.reshape(b, rows, GRID_W, ch)
    dn = ('NHWC', 'HWIO', 'NHWC')
    xh = lax.conv_general_dilated(img[..., :half], w[None, :, None, :half], (1, 1),
                                  [(0, 0), (pad, pad)], dimension_numbers=dn,
                                  feature_group_count=half)
    xv = lax.conv_general_dilated(img[..., half:], w[:, None, None, half:], (1, 1),
                                  [(pad, pad), (0, 0)], dimension_numbers=dn,
                                  feature_group_count=ch - half)
    return jnp.concatenate([xh, xv], axis=-1).reshape(b, L, ch)


def gated_delta_chunks(q, k, v, log_a, beta, s0):
    b, h, L, dk = q.shape
    dv = v.shape[-1]
    c = GDN_CHUNK
    n = L // c
    q = q.reshape(b, h, n, c, dk)
    k = k.reshape(b, h, n, c, dk)
    v = v.reshape(b, h, n, c, dv)
    beta = beta.reshape(b, h, n, c)
    g = jnp.cumsum(log_a.reshape(b, h, n, c), axis=-1)
    pos = jnp.arange(c)
    incl = pos[:, None] >= pos[None, :]
    strict = pos[:, None] > pos[None, :]
    decay = jnp.where(incl, jnp.exp(jnp.where(incl, g[..., :, None] - g[..., None, :], 0.0)), 0.0)
    kk = jnp.einsum('bhntd,bhnjd->bhntj', k, k) * decay
    lhs = jnp.eye(c, dtype=F32) + jnp.where(strict, kk, 0.0) * beta[..., :, None]
    rhs = jnp.concatenate([v * beta[..., None], k * (beta * jnp.exp(g))[..., None]], axis=-1)
    sol = lax.linalg.triangular_solve(lhs, rhs, left_side=True, lower=True, unit_diagonal=True)
    u_t, w = sol[..., :dv], sol[..., dv:]
    qk = jnp.einsum('bhntd,bhnjd->bhntj', q, k) * decay
    q_dec = q * jnp.exp(g)[..., None]
    k_dec = k * jnp.exp(g[..., -1:] - g)[..., None]
    chunk_decay = jnp.exp(g[..., -1])

    def step(s, xs):
        u_c, w_c, qk_c, qd_c, kd_c, cd_c = xs
        u = u_c - jnp.einsum('bhcd,bhde->bhce', w_c, s)
        o = jnp.einsum('bhcd,bhde->bhce', qd_c, s) + jnp.einsum('bhtj,bhje->bhte', qk_c, u)
        s = s * cd_c[..., None, None] + jnp.einsum('bhcd,bhce->bhde', kd_c, u)
        return s, o

    xs = tuple(jnp.moveaxis(t, 2, 0) for t in (u_t, w, qk, q_dec, k_dec, chunk_decay))
    s_fin, o = lax.scan(step, s0, xs)
    return jnp.moveaxis(o, 0, 2).reshape(b, h, L, dv), s_fin


def gdn_prepare(p, conv_w, a_log, dt_bias):
    b, L, _ = p.shape
    W, H = GDN_WIDTH, GDN_HEADS
    qkv = jax.nn.silu(dwconv1d(p[..., :3 * W], conv_w)).astype(F32)
    q = l2norm(to_heads(qkv[..., :W], H)) * (GDN_HEAD_DIM ** -0.5)
    k = l2norm(to_heads(qkv[..., W:2 * W], H))
    v = to_heads(qkv[..., 2 * W:], H)
    z = p[..., 3 * W:4 * W]
    gates = p[..., 4 * W:].astype(F32).reshape(b, L, 4, H)
    log_a = -jnp.exp(a_log.astype(F32)) * jax.nn.softplus(gates[:, :, :2] + dt_bias.astype(F32))
    beta = jax.nn.sigmoid(gates[:, :, 2:])
    return q, k, v, jnp.transpose(log_a, (2, 0, 3, 1)), jnp.transpose(beta, (2, 0, 3, 1)), z


def gdn_mixer(p, pc, conv_w, a_log, dt_bias, norm_g, need_ctx):
    q, k, v, la, bt, z = gdn_prepare(p, conv_w, a_log, dt_bias)
    qc, kc, vc, lac, btc, zc = gdn_prepare(pc, conv_w, a_log, dt_bias)
    s0 = jnp.zeros(kc.shape[:2] + (GDN_HEAD_DIM, GDN_HEAD_DIM), F32)
    oc_f, sc_f = gated_delta_chunks(qc, kc, vc, lac[0], btc[0], s0)
    oc_b, sc_b = gated_delta_chunks(rev(qc), rev(kc), rev(vc), rev(lac[1]), rev(btc[1]), s0)
    o_f, _ = gated_delta_chunks(q, k, v, la[0], bt[0], sc_f)
    o_b, _ = gated_delta_chunks(rev(q), rev(k), rev(v), rev(la[1]), rev(bt[1]), sc_b)
    y = head_out_norm(o_f + rev(o_b), z, norm_g)
    yc = head_out_norm(oc_f + rev(oc_b), zc, norm_g) if need_ctx else None
    return y, yc


def s5_discretize(lam_re, lam_im, log_step, b_re, b_im):
    lr, li = lam_re.astype(F32), lam_im.astype(F32)
    step = jnp.exp(log_step.astype(F32))[:, None]
    mag = jnp.exp(lr * step)
    abr, abi = mag * jnp.cos(li * step), mag * jnp.sin(li * step)
    den = lr * lr + li * li
    nr, ni = abr - 1.0, abi
    cr = (nr * lr + ni * li) / den
    ci = (ni * lr - nr * li) / den
    br, bi = b_re.astype(F32), b_im.astype(F32)
    bbr = cr[..., None] * br - ci[..., None] * bi
    bbi = cr[..., None] * bi + ci[..., None] * br
    return abr, abi, bbr, bbi


def s5_combine(e1, e2):
    a1r, a1i, b1r, b1i = e1
    a2r, a2i, b2r, b2i = e2
    ar = a1r * a2r - a1i * a2i
    ai = a1r * a2i + a1i * a2r
    xr, xi = a2r[:, None], a2i[:, None]
    br = xr * b1r - xi * b1i + b2r
    bi = xr * b1i + xi * b1r + b2i
    return ar, ai, br, bi


def s5_direction(ug, disc, h0, c_re, c_im):
    abr, abi, bbr, bbi = disc
    h0r, h0i = h0
    bur = jnp.einsum('blgc,gpc->lbgp', ug, bbr)
    bui = jnp.einsum('blgc,gpc->lbgp', ug, bbi)
    bur = bur.at[0].add(abr * h0r - abi * h0i)
    bui = bui.at[0].add(abr * h0i + abi * h0r)
    L = ug.shape[1]
    ar = jnp.broadcast_to(abr, (L,) + abr.shape)
    ai = jnp.broadcast_to(abi, (L,) + abi.shape)
    _, _, hr, hi = lax.associative_scan(s5_combine, (ar, ai, bur, bui))
    y = jnp.einsum('lbgp,gcp->blgc', hr, c_re) - jnp.einsum('lbgp,gcp->blgc', hi, c_im)
    return y, (hr[-1], hi[-1])


def s5_readout(yg, u, d_skip, glu_w, glu_b):
    b, L, _ = u.shape
    y = yg.reshape(b, L, S5_WIDTH) + d_skip.astype(F32) * u.astype(F32)
    y = jax.nn.gelu(y)
    y = y * jax.nn.sigmoid(y @ glu_w.astype(F32) + glu_b.astype(F32))
    return y.astype(u.dtype)


def s5_mixer(u, uc, lam_re, lam_im, log_step, b_re, b_im, c_re, c_im, d_skip, glu_w, glu_b, need_ctx):
    disc_f = s5_discretize(lam_re[0], lam_im[0], log_step[0], b_re, b_im)
    disc_b = s5_discretize(lam_re[1], lam_im[1], log_step[1], b_re, b_im)
    cr, ci = c_re.astype(F32), c_im.astype(F32)
    ug = u.astype(F32).reshape(u.shape[0], u.shape[1], S5_GROUPS, S5_GROUP)
    ucg = uc.astype(F32).reshape(uc.shape[0], uc.shape[1], S5_GROUPS, S5_GROUP)
    h0 = jnp.zeros((u.shape[0], S5_GROUPS, S5_STATE), F32)
    yc_f, hc_f = s5_direction(ucg, disc_f, (h0, h0), cr, ci)
    yc_b, hc_b = s5_direction(jnp.flip(ucg, 1), disc_b, (h0, h0), cr, ci)
    y_f, _ = s5_direction(ug, disc_f, hc_f, cr, ci)
    y_b, _ = s5_direction(jnp.flip(ug, 1), disc_b, hc_b, cr, ci)
    y = s5_readout(y_f + jnp.flip(y_b, 1), u, d_skip, glu_w, glu_b)
    yc = s5_readout(yc_f + jnp.flip(yc_b, 1), uc, d_skip, glu_w, glu_b) if need_ctx else None
    return y, yc


def mixer_ab(hn, hcn, w_in, w_out, conv_w, a_log, dt_bias, gdn_g, lam_re, lam_im, log_step,
             b_re, b_im, c_re, c_im, d_skip, glu_w, glu_b, need_ctx):
    p = hn @ w_in
    pc = hcn @ w_in
    a_l, a_c = gdn_mixer(p[..., :GDN_IN], pc[..., :GDN_IN], conv_w, a_log, dt_bias, gdn_g, need_ctx)
    b_l, b_c = s5_mixer(p[..., GDN_IN:], pc[..., GDN_IN:], lam_re, lam_im, log_step, b_re, b_im,
                        c_re, c_im, d_skip, glu_w, glu_b, need_ctx)
    y = jnp.concatenate([a_l, b_l], axis=-1) @ w_out
    yc = jnp.concatenate([a_c, b_c], axis=-1) @ w_out if need_ctx else None
    return y, yc


def conformer_conv(p, dw_w, dw_b, ln_g, ln_b, rows):
    a, gt = jnp.split(p, 2, axis=-1)
    x = a * jax.nn.sigmoid(gt)
    x = axial_dwconv(x, dw_w, rows) if rows > 0 else dwconv1d(x, dw_w)
    x = layernorm(x + dw_b, ln_g, ln_b)
    return jax.nn.silu(x)


def gla_chunks(q, k, v, log_f, s0):
    b, h, L, dk = q.shape
    dv = v.shape[-1]
    c = HGRN_CHUNK
    n = L // c
    q, k, log_f = (t.reshape(b, h, n, c, dk) for t in (q, k, log_f))
    v = v.reshape(b, h, n, c, dv)
    bc = jnp.cumsum(log_f, axis=3)
    q_in = q * jnp.exp(bc)
    k_in = k * jnp.exp(-bc)
    pos = jnp.arange(c)
    incl = pos[:, None] >= pos[None, :]
    attn = jnp.where(incl, jnp.einsum('bhntd,bhnjd->bhntj', q_in, k_in), 0.0)
    o_intra = jnp.einsum('bhntj,bhnje->bhnte', attn, v)
    k_out = k * jnp.exp(bc[..., -1:, :] - bc)
    chunk_decay = jnp.exp(bc[..., -1, :])

    def step(s, xs):
        qi, oi, ko, vc, cd = xs
        o = jnp.einsum('bhcd,bhde->bhce', qi, s) + oi
        s = s * cd[..., None] + jnp.einsum('bhcd,bhce->bhde', ko, vc)
        return s, o

    xs = tuple(jnp.moveaxis(t, 2, 0) for t in (q_in, o_intra, k_out, v, chunk_decay))
    s_fin, o = lax.scan(step, s0, xs)
    return jnp.moveaxis(o, 0, 2).reshape(b, h, L, dv), s_fin


def hgrn_prepare(p, lb):
    b, L, _ = p.shape
    H, KW = HGRN_HEADS, HGRN_KW
    pf = p.astype(F32)
    q = to_heads(pf[..., :KW], H)
    f = lb + (1.0 - lb) * jax.nn.sigmoid(pf[..., KW:3 * KW].reshape(b, L, 2, KW))
    f = jnp.transpose(f.reshape(b, L, 2, H, HGRN_KEY_DIM), (2, 0, 3, 1, 4))
    i = to_heads(pf[..., 3 * KW:3 * KW + HGRN_VW], H)
    g = p[..., 3 * KW + HGRN_VW:]
    return q, 1.0 - f, jnp.log(f), i, g


def hgrn2_mixer(p, pc, lb, norm_g, need_ctx):
    q, k, lf, i, g = hgrn_prepare(p, lb)
    qc, kc, lfc, ic, gc = hgrn_prepare(pc, lb)
    s0 = jnp.zeros(qc.shape[:2] + (HGRN_KEY_DIM, HGRN_VAL_DIM), F32)
    oc_f, sc_f = gla_chunks(qc, kc[0], ic, lfc[0], s0)
    oc_b, sc_b = gla_chunks(rev(qc), rev(kc[1]), rev(ic), rev(lfc[1]), s0)
    o_f, _ = gla_chunks(q, k[0], i, lf[0], sc_f)
    o_b, _ = gla_chunks(rev(q), rev(k[1]), rev(i), rev(lf[1]), sc_b)
    y = head_out_norm(o_f + rev(o_b), g, norm_g)
    yc = head_out_norm(oc_f + rev(oc_b), gc, norm_g) if need_ctx else None
    return y, yc


def mixer_cd(hn, hcn, rows, w_in, w_out, dw_w, dw_b, ln_g, ln_b, lb, hgrn_g, need_ctx):
    p = hn @ w_in
    pc = hcn @ w_in
    c_l = conformer_conv(p[..., :CONV_IN], dw_w, dw_b, ln_g, ln_b, rows)
    d_l, d_c = hgrn2_mixer(p[..., CONV_IN:], pc[..., CONV_IN:], lb, hgrn_g, need_ctx)
    y = jnp.concatenate([c_l, d_l], axis=-1) @ w_out
    if need_ctx:
        c_c = conformer_conv(pc[..., :CONV_IN], dw_w, dw_b, ln_g, ln_b, 0)
        yc = jnp.concatenate([c_c, d_c], axis=-1) @ w_out
    else:
        yc = None
    return y, yc


def setup_inputs(seed: int = 0) -> dict:
    key = jax.random.key(seed)
    ks = iter(jax.random.split(key, 64))

    def nrm(shape, scale):
        return jax.random.normal(next(ks), shape, F32) * scale

    def unif(shape, lo, hi):
        return jax.random.uniform(next(ks), shape, F32, lo, hi)

    D, NE, NO = D_MODEL, N_EVEN, N_ODD
    dt = jnp.exp(unif((NE, 2, GDN_HEADS), math.log(1e-3), math.log(1e-1)))
    return {
        "x": nrm((BATCH, SEQ, D), 1.0),
        "c": nrm((BATCH, D), 1.0),
        "ctx": nrm((BATCH, CTX_LEN, D), 1.0),
        "c_ctx": nrm((D,), 1.0),
        "ada_w": nrm((DEPTH, D, 9 * D), 0.5 * D ** -0.5),
        "ada_b": nrm((DEPTH, 9 * D), 0.02),
        "norm_g": 1.0 + nrm((DEPTH, 3, D), 0.02),
        "ffn_w_up": nrm((DEPTH, 2, D, 2 * FFN_DIM), D ** -0.5),
        "ffn_w_down": nrm((DEPTH, 2, FFN_DIM, D), FFN_DIM ** -0.5),
        "ab_w_in": nrm((NE, D, AB_IN), D ** -0.5),
        "ab_w_out": nrm((NE, AB_OUT, D), AB_OUT ** -0.5),
        "gdn_conv_w": nrm((NE, GDN_CONV, 3 * GDN_WIDTH), GDN_CONV ** -0.5),
        "gdn_a_log": jnp.log(unif((NE, 2, GDN_HEADS), 1.0, 16.0)),
        "gdn_dt_bias": dt + jnp.log(-jnp.expm1(-dt)),
        "gdn_norm_g": 1.0 + nrm((NE, GDN_HEAD_DIM), 0.02),
        "s5_lambda_re": -0.5 + nrm((NE, 2, S5_GROUPS, S5_STATE), 0.01),
        "s5_lambda_im": math.pi * jnp.arange(S5_STATE, dtype=F32) + nrm((NE, 2, S5_GROUPS, S5_STATE), 0.01),
        "s5_log_step": unif((NE, 2, S5_GROUPS), math.log(1e-3), math.log(1e-1)),
        "s5_b_re": nrm((NE, S5_GROUPS, S5_STATE, S5_GROUP), (2 * S5_GROUP) ** -0.5),
        "s5_b_im": nrm((NE, S5_GROUPS, S5_STATE, S5_GROUP), (2 * S5_GROUP) ** -0.5),
        "s5_c_re": nrm((NE, S5_GROUPS, S5_GROUP, S5_STATE), S5_STATE ** -0.5),
        "s5_c_im": nrm((NE, S5_GROUPS, S5_GROUP, S5_STATE), S5_STATE ** -0.5),
        "s5_d": nrm((NE, S5_WIDTH), 1.0),
        "s5_glu_w": nrm((NE, S5_WIDTH, S5_WIDTH), S5_WIDTH ** -0.5),
        "s5_glu_b": nrm((NE, S5_WIDTH), 0.01),
        "cd_w_in": nrm((NO, D, CD_IN), D ** -0.5),
        "cd_w_out": nrm((NO, CD_OUT, D), CD_OUT ** -0.5),
        "conv_dw_w": nrm((NO, CONV_K, CONV_WIDTH), CONV_K ** -0.5),
        "conv_dw_b": nrm((NO, CONV_WIDTH), 0.01),
        "conv_ln_g": 1.0 + nrm((NO, CONV_WIDTH), 0.02),
        "conv_ln_b": nrm((NO, CONV_WIDTH), 0.01),
        "hgrn_lb_logits": nrm((NO + 1, HGRN_KW), 0.1),
        "hgrn_norm_g": 1.0 + nrm((NO, HGRN_VAL_DIM), 0.02),
        "final_norm_g": 1.0 + nrm((D,), 0.02),
    }


def reference(x, c, ctx, c_ctx, ada_w, ada_b, norm_g, ffn_w_up, ffn_w_down,
              ab_w_in, ab_w_out, gdn_conv_w, gdn_a_log, gdn_dt_bias, gdn_norm_g,
              s5_lambda_re, s5_lambda_im, s5_log_step, s5_b_re, s5_b_im, s5_c_re, s5_c_im,
              s5_d, s5_glu_w, s5_glu_b,
              cd_w_in, cd_w_out, conv_dw_w, conv_dw_b, conv_ln_g, conv_ln_b,
              hgrn_lb_logits, hgrn_norm_g, final_norm_g):
    rows = x.shape[1] // GRID_W
    sc = jax.nn.silu(c)
    scc = jax.nn.silu(c_ctx)[None]
    lb_all = jnp.cumsum(jax.nn.softmax(hgrn_lb_logits.astype(F32), axis=0), axis=0)
    h, hc = x, ctx
    for l in range(DEPTH):
        last = l == DEPTH - 1
        m = (sc @ ada_w[l] + ada_b[l]).reshape(-1, 3, 3, D_MODEL)
        mc = (scc @ ada_w[l] + ada_b[l]).reshape(-1, 3, 3, D_MODEL)
        h = h + 0.5 * m[:, 0, 2, None] * swiglu(ada_norm(h, norm_g[l, 0], m, 0), ffn_w_up[l, 0], ffn_w_down[l, 0])
        hc = hc + 0.5 * mc[:, 0, 2, None] * swiglu(ada_norm(hc, norm_g[l, 0], mc, 0), ffn_w_up[l, 0], ffn_w_down[l, 0])
        hn = ada_norm(h, norm_g[l, 1], m, 1)
        hcn = ada_norm(hc, norm_g[l, 1], mc, 1)
        if l % 2 == 0:
            e = l // 2
            y, yc = mixer_ab(hn, hcn, ab_w_in[e], ab_w_out[e], gdn_conv_w[e], gdn_a_log[e], gdn_dt_bias[e],
                             gdn_norm_g[e], s5_lambda_re[e], s5_lambda_im[e], s5_log_step[e], s5_b_re[e],
                             s5_b_im[e], s5_c_re[e], s5_c_im[e], s5_d[e], s5_glu_w[e], s5_glu_b[e], not last)
        else:
            o = l // 2
            y, yc = mixer_cd(hn, hcn, rows, cd_w_in[o], cd_w_out[o], conv_dw_w[o], conv_dw_b[o],
                             conv_ln_g[o], conv_ln_b[o], lb_all[o], hgrn_norm_g[o], not last)
        h = h + m[:, 1, 2, None] * y
        h = h + 0.5 * m[:, 2, 2, None] * swiglu(ada_norm(h, norm_g[l, 2], m, 2), ffn_w_up[l, 1], ffn_w_down[l, 1])
        if not last:
            hc = hc + mc[:, 1, 2, None] * yc
            hc = hc + 0.5 * mc[:, 2, 2, None] * swiglu(ada_norm(hc, norm_g[l, 2], mc, 2), ffn_w_up[l, 1], ffn_w_down[l, 1])
    return rmsnorm(h, final_norm_g)
```

```python
import functools
import math

import jax
import jax.numpy as jnp
from jax import lax
from jax.experimental import pallas as pl
from jax.experimental.pallas import tpu as pltpu

F32 = jnp.float32
BF16 = jnp.bfloat16
HIGHEST = lax.Precision.HIGHEST
EPS = 1e-6

D_MODEL = 1024
GRID_W = 64
FFN_DIM = 2816

GDN_HEADS = 4
GDN_DIM = 128
GDN_WIDTH = GDN_HEADS * GDN_DIM
GDN_CONV = 5
GDN_CHUNK = 64

S5_WIDTH = 512
S5_GROUP = 16
S5_GROUPS = 32
S5_STATE = 64
S5_CHUNK = 16
S5_ROW = S5_CHUNK * S5_GROUP
S5_LANES = S5_GROUPS * S5_STATE

CONV_WIDTH = 512
CONV_K = 31
CONV_PAD = 16

HGRN_HEADS = 4
HGRN_DIM = 128
HGRN_WIDTH = HGRN_HEADS * HGRN_DIM
HGRN_CHUNK = 16

V7X_VMEM_LIMIT = 48 * 1024 * 1024


def _params(*sem):
    return pltpu.CompilerParams(dimension_semantics=sem, vmem_limit_bytes=V7X_VMEM_LIMIT)


def _silu(x):
    return x * jax.nn.sigmoid(x)


def _mm(a, b):
    return jnp.dot(a.astype(BF16), b.astype(BF16), preferred_element_type=F32)


def _mm_nt(a, b):
    return lax.dot_general(a.astype(BF16), b.astype(BF16), (((1,), (1,)), ((), ())),
                           preferred_element_type=F32)


def _mm_tn(a, b):
    return lax.dot_general(a.astype(BF16), b.astype(BF16), (((0,), (0,)), ((), ())),
                           preferred_element_type=F32)


def _split3(x):
    hi = x.astype(BF16)
    r1 = x - hi.astype(F32)
    mid = r1.astype(BF16)
    lo = (r1 - mid.astype(F32)).astype(BF16)
    return hi, mid, lo


def _mm_exact_lhs(a_bf16, x):
    hi, mid, lo = _split3(x)
    dot = functools.partial(jnp.dot, preferred_element_type=F32)
    return dot(a_bf16, hi) + dot(a_bf16, mid) + dot(a_bf16, lo)


def _mm_exact_rhs(x, b_bf16):
    hi, mid, lo = _split3(x)
    dot = functools.partial(jnp.dot, preferred_element_type=F32)
    return dot(hi, b_bf16) + dot(mid, b_bf16) + dot(lo, b_bf16)


def _mm_nt_exact_lhs(a_bf16, x):
    hi, mid, lo = _split3(x)
    dot = lambda p, q: lax.dot_general(p, q, (((1,), (1,)), ((), ())), preferred_element_type=F32)
    return dot(a_bf16, hi) + dot(a_bf16, mid) + dot(a_bf16, lo)


def _mm_x3(a, b):
    ah = a.astype(BF16)
    al = (a - ah.astype(F32)).astype(BF16)
    bh = b.astype(BF16)
    bl = (b - bh.astype(F32)).astype(BF16)
    dot = functools.partial(jnp.dot, preferred_element_type=F32)
    return dot(ah, bh) + dot(ah, bl) + dot(al, bh)


def _ada_norm(x, g, scale, shift):
    y = x * lax.rsqrt(jnp.mean(x * x, axis=-1, keepdims=True) + EPS) * g
    return y * (1.0 + scale) + shift


def _ada_kernel(c_ref, w_ref, b_ref, o_ref):
    o_ref[...] = _mm(_silu(c_ref[...]), w_ref[...]) + b_ref[...]


def _ada_call(cvec, ada_w, ada_b):
    depth = ada_w.shape[0]
    ncol = ada_w.shape[2] // D_MODEL
    out = pl.pallas_call(
        _ada_kernel,
        grid=(depth, ncol),
        in_specs=[pl.BlockSpec((8, D_MODEL), lambda l, j: (0, 0)),
                  pl.BlockSpec((None, D_MODEL, D_MODEL), lambda l, j: (l, 0, j)),
                  pl.BlockSpec((None, 1, D_MODEL), lambda l, j: (l, 0, j))],
        out_specs=pl.BlockSpec((None, None, 8, D_MODEL), lambda l, j: (l, j, 0, 0)),
        out_shape=jax.ShapeDtypeStruct((depth, ncol, 8, D_MODEL), F32),
        compiler_params=_params("arbitrary", "arbitrary"),
        name="ada_mod",
    )(cvec, ada_w, ada_b.reshape(depth, 1, -1))
    return out.reshape(depth * ncol * 8, 1, D_MODEL)


def _mod_spec(layer, sub, kind, ctx):
    base = (layer * 9 + sub * 3 + kind) * 8
    if ctx:
        return pl.BlockSpec((None, 1, D_MODEL), lambda b, *_: (base + 2, 0, 0))
    return pl.BlockSpec((None, 1, D_MODEL), lambda b, *_: (base + b, 0, 0))


def _ffn_kernel(nj, final, h_ref, sh_ref, sc_ref, gt_ref, g_ref, wg_ref, wu_ref, wd_ref, *rest):
    if final:
        fg_ref, o_ref, xn_ref, acc_ref = rest
    else:
        o_ref, xn_ref, acc_ref = rest
    j = pl.program_id(2)

    @pl.when(j == 0)
    def _():
        xn_ref[...] = _ada_norm(h_ref[...], g_ref[...], sc_ref[...], sh_ref[...]).astype(BF16)
        acc_ref[...] = jnp.zeros_like(acc_ref)

    xn = xn_ref[...]
    gate = jnp.dot(xn, wg_ref[...], preferred_element_type=F32)
    up = jnp.dot(xn, wu_ref[...], preferred_element_type=F32)
    act = (_silu(gate) * up).astype(BF16)
    acc_ref[...] += jnp.dot(act, wd_ref[...], preferred_element_type=F32)

    @pl.when(j == nj - 1)
    def _():
        y = h_ref[...] + 0.5 * gt_ref[...] * acc_ref[...]
        if final:
            y = y * lax.rsqrt(jnp.mean(y * y, axis=-1, keepdims=True) + EPS) * fg_ref[...]
        o_ref[...] = y


def _ffn_call(h, mod, layer, sub, ctx, norm_g_row, w_up, w_down, final_g=None):
    b, seq, _ = h.shape
    tm = min(512, seq)
    tf = 256
    nj = FFN_DIM // tf
    final = final_g is not None
    in_specs = [
        pl.BlockSpec((None, tm, D_MODEL), lambda b, i, j: (b, i, 0)),
        _mod_spec(layer, sub, 0, ctx), _mod_spec(layer, sub, 1, ctx), _mod_spec(layer, sub, 2, ctx),
        pl.BlockSpec((1, D_MODEL), lambda b, i, j: (0, 0)),
        pl.BlockSpec((D_MODEL, tf), lambda b, i, j: (0, j)),
        pl.BlockSpec((D_MODEL, tf), lambda b, i, j: (0, nj + j)),
        pl.BlockSpec((tf, D_MODEL), lambda b, i, j: (j, 0)),
    ]
    args = [h, mod, mod, mod, norm_g_row, w_up, w_up, w_down]
    if final:
        in_specs.append(pl.BlockSpec((1, D_MODEL), lambda b, i, j: (0, 0)))
        args.append(final_g)
    return pl.pallas_call(
        functools.partial(_ffn_kernel, nj, final),
        grid=(b, seq // tm, nj),
        in_specs=in_specs,
        out_specs=pl.BlockSpec((None, tm, D_MODEL), lambda b, i, j: (b, i, 0)),
        out_shape=jax.ShapeDtypeStruct(h.shape, F32),
        scratch_shapes=[pltpu.VMEM((tm, D_MODEL), BF16), pltpu.VMEM((tm, D_MODEL), F32)],
        compiler_params=_params("parallel", "parallel", "arbitrary"),
        name="ffn",
    )(*args)


def _inproj_kernel(nw, glu, h_ref, sh_ref, sc_ref, g_ref, *refs):
    w_refs, o_refs = refs[:nw], refs[nw:]
    xn = _ada_norm(h_ref[...], g_ref[...], sc_ref[...], sh_ref[...]).astype(BF16)
    outs = [jnp.dot(xn, w[...], preferred_element_type=F32) for w in w_refs]
    if glu:
        outs = [outs[0] * jax.nn.sigmoid(outs[1])] + outs[2:]
    for o_ref, val in zip(o_refs, outs):
        o_ref[...] = val


def _inproj_call(h, mod, layer, ctx, norm_g_row, weights, glu):
    b, seq, _ = h.shape
    tm = min(256, seq)
    widths = [w.shape[1] for w in weights]
    out_widths = widths[1:] if glu else widths
    in_specs = [pl.BlockSpec((None, tm, D_MODEL), lambda b, i: (b, i, 0)),
                _mod_spec(layer, 1, 0, ctx), _mod_spec(layer, 1, 1, ctx),
                pl.BlockSpec((1, D_MODEL), lambda b, i: (0, 0))]
    in_specs += [pl.BlockSpec((D_MODEL, w), lambda b, i: (0, 0)) for w in widths]
    return pl.pallas_call(
        functools.partial(_inproj_kernel, len(weights), glu),
        grid=(b, seq // tm),
        in_specs=in_specs,
        out_specs=[pl.BlockSpec((None, tm, w), lambda b, i: (b, i, 0)) for w in out_widths],
        out_shape=[jax.ShapeDtypeStruct((b, seq, w), F32) for w in out_widths],
        compiler_params=_params("parallel", "parallel"),
        name="inproj",
    )(h, mod, mod, norm_g_row, *weights)


def _gdn_kernel(rev, t_blk, nb,
                qm, qp, qn, km, kp, kn, vm, vp, vn, cwq, cwk, cwv, gates_ref, alog_ref, dtb_ref, s0_ref,
                o_ref, sfin_ref, s_scr, pad_scr, q_scr, k_scr, v_scr, sel_scr):
    h = pl.program_id(1)
    i = pl.program_id(2)
    blk = (nb - 1 - i) if rev else i
    c = GDN_CHUNK
    nch = t_blk // c

    @pl.when(i == 0)
    def _():
        s_scr[...] = s0_ref[...]

    def conv(main, prev, nxt, cw):
        pad_scr[pl.ds(0, 8), :] = jnp.where(blk == 0, 0.0, prev[...])
        pad_scr[pl.ds(8, t_blk), :] = main[...]
        pad_scr[pl.ds(8 + t_blk, 8), :] = jnp.where(blk == nb - 1, 0.0, nxt[...])
        w = cw[...]
        acc = jnp.zeros((t_blk, GDN_DIM), F32)
        for tap in range(GDN_CONV):
            acc = acc + pad_scr[pl.ds(8 - GDN_CONV // 2 + tap, t_blk), :] * w[tap:tap + 1, :]
        return _silu(acc)

    def l2n(x):
        return x * lax.rsqrt(jnp.sum(x * x, axis=-1, keepdims=True) + EPS)

    q_scr[...] = l2n(conv(qm, qp, qn, cwq)) * (GDN_DIM ** -0.5)
    k_scr[...] = l2n(conv(km, kp, kn, cwk))
    v_scr[...] = conv(vm, vp, vn, cwv)

    gates = gates_ref[...]
    lane = lax.broadcasted_iota(jnp.int32, gates.shape, 1)
    log_a = -jnp.exp(alog_ref[...]) * jax.nn.softplus(gates + dtb_ref[...])
    mixed = jnp.where(lane < 2 * GDN_HEADS, log_a, jax.nn.sigmoid(gates))
    d_off = GDN_HEADS if rev else 0
    src = lax.broadcasted_iota(jnp.int32, (128, 256), 0)
    dst = lax.broadcasted_iota(jnp.int32, (128, 256), 1)
    want = jnp.where(dst < 128, d_off + h, 2 * GDN_HEADS + d_off + h)
    pick = (src == want).astype(BF16)
    sel_scr[...] = _mm_exact_rhs(mixed, pick)

    row = lax.broadcasted_iota(jnp.int32, (c, c), 0)
    col = lax.broadcasted_iota(jnp.int32, (c, c), 1)
    incl = (row <= col) if rev else (row >= col)
    strict = (row < col) if rev else (row > col)
    tri = incl.astype(BF16)
    lane0 = (lax.broadcasted_iota(jnp.int32, (c, 128), 1) == 0).astype(BF16)
    last = 0 if rev else c - 1

    def chunk(ci, carry):
        cidx = (nch - 1 - ci) if rev else ci
        r0 = pl.multiple_of(cidx * c, c)
        q = q_scr[pl.ds(r0, c), :]
        k = k_scr[pl.ds(r0, c), :]
        v = v_scr[pl.ds(r0, c), :]
        sel = sel_scr[pl.ds(r0, c), :]
        la = sel[:, :128]
        beta = sel[:, 128:]
        g = _mm_exact_lhs(tri, la)
        g_row = _mm_nt_exact_lhs(lane0, g)
        diff = g[:, :c] - g_row
        decay = jnp.where(incl, jnp.exp(jnp.where(incl, diff, 0.0)), 0.0)
        kb = k.astype(BF16)
        kk = _mm_nt(kb, kb)
        qk = _mm_nt(q, kb)
        a_mat = jnp.where(strict, kk * decay, 0.0) * beta[:, :c]
        p = -a_mat
        n_mat = p
        for _ in range(5):
            p = _mm(p, p)
            n_mat = n_mat + p + _mm(n_mat, p)
        eg = jnp.exp(g)
        rhs = jnp.concatenate([v * beta, k * (beta * eg)], axis=1)
        sol = rhs + _mm_x3(n_mat, rhs)
        u_t = sol[:, :GDN_DIM]
        w = sol[:, GDN_DIM:]
        qk_d = jnp.where(incl, qk * decay, 0.0)
        g_last = g[last:last + 1, :]
        k_dec = k * jnp.exp(g_last - g)
        s = s_scr[...]
        u = u_t - _mm(w, s)
        o_ref[pl.ds(r0, c), :] = _mm(q * eg, s) + _mm(qk_d, u)
        s_scr[...] = s * jnp.exp(g_last) + _mm_tn(k_dec, u)
        return carry

    lax.fori_loop(0, nch, chunk, 0)

    @pl.when(i == nb - 1)
    def _():
        sfin_ref[...] = s_scr[...]


def _gdn_call(qkv, gates, conv_w, alog_row, dtb_row, s0, rev):
    b, seq, _ = qkv.shape
    t_blk = min(512, seq)
    nb = seq // t_blk
    r8 = t_blk // 8
    hh = GDN_HEADS

    def blk(i):
        return (nb - 1 - i) if rev else i

    def main(part):
        return pl.BlockSpec((None, t_blk, GDN_DIM), lambda b, h, i: (b, blk(i), part * hh + h))

    def prev(part):
        return pl.BlockSpec((None, 8, GDN_DIM),
                            lambda b, h, i: (b, jnp.maximum(blk(i) * r8 - 1, 0), part * hh + h))

    def nxt(part):
        return pl.BlockSpec((None, 8, GDN_DIM),
                            lambda b, h, i: (b, jnp.minimum((blk(i) + 1) * r8, seq // 8 - 1), part * hh + h))

    def cw(part):
        return pl.BlockSpec((GDN_CONV, GDN_DIM), lambda b, h, i: (0, part * hh + h))

    state_spec = pl.BlockSpec((None, None, GDN_DIM, GDN_DIM), lambda b, h, i: (b, h, 0, 0))
    in_specs = []
    args = []
    for part in range(3):
        in_specs += [main(part), prev(part), nxt(part)]
        args += [qkv, qkv, qkv]
    in_specs += [cw(0), cw(1), cw(2),
                 pl.BlockSpec((None, t_blk, 128), lambda b, h, i: (b, blk(i), 0)),
                 pl.BlockSpec((1, 128), lambda b, h, i: (0, 0)),
                 pl.BlockSpec((1, 128), lambda b, h, i: (0, 0)),
                 state_spec]
    args += [conv_w, conv_w, conv_w, gates, alog_row, dtb_row, s0]
    return pl.pallas_call(
        functools.partial(_gdn_kernel, rev, t_blk, nb),
        grid=(b, hh, nb),
        in_specs=in_specs,
        out_specs=[pl.BlockSpec((None, t_blk, GDN_DIM), lambda b, h, i: (b, blk(i), h)), state_spec],
        out_shape=[jax.ShapeDtypeStruct((b, seq, GDN_WIDTH), F32),
                   jax.ShapeDtypeStruct((b, hh, GDN_DIM, GDN_DIM), F32)],
        scratch_shapes=[pltpu.VMEM((GDN_DIM, GDN_DIM), F32),
                        pltpu.VMEM((t_blk + 16, GDN_DIM), F32),
                        pltpu.VMEM((t_blk, GDN_DIM), F32),
                        pltpu.VMEM((t_blk, GDN_DIM), F32),
                        pltpu.VMEM((t_blk, GDN_DIM), F32),
                        pltpu.VMEM((t_blk, 256), F32)],
        compiler_params=_params("parallel", "parallel", "arbitrary"),
        name="gdn_bwd" if rev else "gdn_fwd",
    )(*args)


def _s5_ops_kernel(lr_ref, li_ref, ls_ref, btr_ref, bti_ref, cr_ref, ci_ref,
                   m_ref, winr_ref, wini_ref, woutr_ref, wouti_ref, a16r_ref, a16i_ref):
    d = pl.program_id(0)
    lr = lr_ref[...]
    li = li_ref[...]
    dt = jnp.exp(ls_ref[...])

    def apow(kk):
        mag = jnp.exp(lr * dt * kk)
        ang = li * dt * kk
        return mag * jnp.cos(ang), mag * jnp.sin(ang)

    ar, ai = apow(1.0)
    den = lr * lr + li * li
    nr, ni = ar - 1.0, ai
    zr = (nr * lr + ni * li) / den
    zi = (ni * lr - nr * li) / den
    btr, bti = btr_ref[...], bti_ref[...]
    bbr = zr * btr - zi * bti
    bbi = zr * bti + zi * btr
    cr, ci = cr_ref[...], ci_ref[...]

    tile = lambda x: jnp.concatenate([x] * S5_CHUNK, axis=0)
    t_row = lax.broadcasted_iota(jnp.int32, (S5_ROW, 1), 0) // S5_GROUP
    t_col = lax.broadcasted_iota(jnp.int32, (1, S5_ROW), 1) // S5_GROUP
    tv_row = jnp.where(d == 0, t_row, S5_CHUNK - 1 - t_row)
    tv_col = jnp.where(d == 0, t_col, S5_CHUNK - 1 - t_col)
    tvf = tv_row.astype(F32)

    bbr_t, bbi_t, cr_t, ci_t = tile(bbr), tile(bbi), tile(cr), tile(ci)
    pr, pi = apow(-tvf)
    xr = bbr_t * pr - bbi_t * pi
    xi = bbr_t * pi + bbi_t * pr
    pr, pi = apow(tvf)
    yr = cr_t * pr - ci_t * pi
    yi = cr_t * pi + ci_t * pr
    nt = lambda p, q: lax.dot_general(p, q, (((1,), (1,)), ((), ())), precision=HIGHEST,
                                      preferred_element_type=F32)
    m = nt(xr, yr) - nt(xi, yi)
    m_ref[...] = jnp.where(tv_col >= tv_row, m, 0.0)

    pr, pi = apow(S5_CHUNK - 1.0 - tvf)
    winr_ref[...] = bbr_t * pr - bbi_t * pi
    wini_ref[...] = bbr_t * pi + bbi_t * pr
    pr, pi = apow(tvf + 1.0)
    woutr_ref[...] = cr_t * pr - ci_t * pi
    wouti_ref[...] = -(cr_t * pi + ci_t * pr)
    a16r, a16i = apow(float(S5_CHUNK))
    a16r_ref[...] = a16r
    a16i_ref[...] = a16i


def _s5_ops_call(lam_re, lam_im, log_step, bt_re, bt_im, c_re, c_im):
    g, p, cg = S5_GROUPS, S5_STATE, S5_GROUP
    ls = jnp.broadcast_to(log_step[:, :, None, None], (2, g, 1, p))
    lam_spec = pl.BlockSpec((None, None, 1, p), lambda d, gi: (d, gi, 0, 0))
    par_spec = pl.BlockSpec((None, cg, p), lambda d, gi: (gi, 0, 0))
    out = lambda *shape: pl.BlockSpec((None, None) + shape, lambda d, gi: (d, gi, 0, 0))
    return pl.pallas_call(
        _s5_ops_kernel,
        grid=(2, g),
        in_specs=[lam_spec, lam_spec, lam_spec, par_spec, par_spec, par_spec, par_spec],
        out_specs=[out(S5_ROW, S5_ROW), out(S5_ROW, p), out(S5_ROW, p), out(S5_ROW, p), out(S5_ROW, p),
                   out(1, p), out(1, p)],
        out_shape=[jax.ShapeDtypeStruct((2, g, S5_ROW, S5_ROW), F32)]
        + [jax.ShapeDtypeStruct((2, g, S5_ROW, p), F32)] * 4
        + [jax.ShapeDtypeStruct((2, g, 1, p), F32)] * 2,
        compiler_params=_params("parallel", "parallel"),
        name="s5_ops",
    )(lam_re.reshape(2, g, 1, p), lam_im.reshape(2, g, 1, p), ls, bt_re, bt_im, c_re, c_im)


def _s5_in_kernel(u_ref, wr_ref, wi_ref, sr_ref, si_ref):
    dot = functools.partial(jnp.dot, preferred_element_type=F32)
    u0, u1 = u_ref[0], u_ref[1]
    sr_ref[...] = dot(u0, wr_ref[0]) + dot(u1, wr_ref[1])
    si_ref[...] = dot(u0, wi_ref[0]) + dot(u1, wi_ref[1])


def _s5_in_call(ug, win_r, win_i):
    b, g, nc, _ = ug.shape
    w_spec = pl.BlockSpec((None, 2, S5_ROW, 128), lambda d, b, gp: (d, gp, 0, 0))
    o_spec = pl.BlockSpec((None, None, nc, 128), lambda d, b, gp: (d, b, 0, gp))
    return pl.pallas_call(
        _s5_in_kernel,
        grid=(2, b, g // 2),
        in_specs=[pl.BlockSpec((None, 2, nc, S5_ROW), lambda d, b, gp: (b, gp, 0, 0)), w_spec, w_spec],
        out_specs=[o_spec, o_spec],
        out_shape=[jax.ShapeDtypeStruct((2, b, nc, S5_LANES), F32)] * 2,
        compiler_params=_params("parallel", "parallel", "parallel"),
        name="s5_in",
    )(ug, win_r, win_i)


def _s5_scan_kernel(nc, sr_ref, si_ref, ar_ref, ai_ref, h0r_ref, h0i_ref, hr_ref, hi_ref, fr_ref, fi_ref):
    d = pl.program_id(0)
    ar = ar_ref[...]
    ai = ai_ref[...]

    def body(i, carry):
        hr, hi = carry
        n = jnp.where(d == 0, i, nc - 1 - i)
        hr_ref[pl.ds(n, 1), :] = hr
        hi_ref[pl.ds(n, 1), :] = hi
        sr = sr_ref[pl.ds(n, 1), :]
        si = si_ref[pl.ds(n, 1), :]
        return ar * hr - ai * hi + sr, ar * hi + ai * hr + si

    hr, hi = lax.fori_loop(0, nc, body, (h0r_ref[...], h0i_ref[...]))
    fr_ref[...] = hr
    fi_ref[...] = hi


def _s5_scan_call(s_re, s_im, a16_re, a16_im, h0_re, h0_im):
    _, b, nc, _ = s_re.shape
    tl = 512
    big = pl.BlockSpec((None, None, nc, tl), lambda d, b, j: (d, b, 0, j))
    a_spec = pl.BlockSpec((None, 1, tl), lambda d, b, j: (d, 0, j))
    st_spec = pl.BlockSpec((None, None, 1, tl), lambda d, b, j: (d, b, 0, j))
    return pl.pallas_call(
        functools.partial(_s5_scan_kernel, nc),
        grid=(2, b, S5_LANES // tl),
        in_specs=[big, big, a_spec, a_spec, st_spec, st_spec],
        out_specs=[big, big, st_spec, st_spec],
        out_shape=[jax.ShapeDtypeStruct(s_re.shape, F32)] * 2
        + [jax.ShapeDtypeStruct((2, b, 1, S5_LANES), F32)] * 2,
        compiler_params=_params("parallel", "parallel", "parallel"),
        name="s5_scan",
    )(s_re, s_im, a16_re, a16_im, h0_re, h0_im)


def _s5_out_kernel(u_ref, m_ref, hr_ref, hi_ref, wr_ref, wi_ref, y_ref):
    u = u_ref[...]
    acc = None
    for d in range(2):
        y = (jnp.dot(u, m_ref[d], preferred_element_type=F32)
             + _mm(hr_ref[d], wr_ref[d]) + _mm(hi_ref[d], wi_ref[d]))
        acc = y if acc is None else acc + y
    y_ref[...] = acc


def _s5_out_call(ug, m, h_re, h_im, wout_r, wout_i):
    b, g, nc, _ = ug.shape
    h_spec = pl.BlockSpec((2, None, nc, 128), lambda b, gi: (0, b, 0, gi // 2))
    w_spec = pl.BlockSpec((2, None, 128, S5_ROW), lambda b, gi: (0, gi, 0, 0))
    return pl.pallas_call(
        _s5_out_kernel,
        grid=(b, g),
        in_specs=[pl.BlockSpec((None, None, nc, S5_ROW), lambda b, gi: (b, gi, 0, 0)),
                  pl.BlockSpec((2, None, S5_ROW, S5_ROW), lambda b, gi: (0, gi, 0, 0)),
                  h_spec, h_spec, w_spec, w_spec],
        out_specs=pl.BlockSpec((None, None, nc, S5_ROW), lambda b, gi: (b, gi, 0, 0)),
        out_shape=jax.ShapeDtypeStruct((b, g, nc, S5_ROW), F32),
        compiler_params=_params("parallel", "parallel"),
        name="s5_out",
    )(ug, m, h_re, h_im, wout_r, wout_i)


def _s5_prepare_ops(lam_re, lam_im, log_step, b_re, b_im, c_re, c_im):
    m, win_r, win_i, wout_r, wout_i, a16_r, a16_i = _s5_ops_call(
        lam_re, lam_im, log_step, jnp.swapaxes(b_re, 1, 2), jnp.swapaxes(b_im, 1, 2), c_re, c_im)
    p = S5_STATE
    odd = (jnp.arange(S5_GROUPS) % 2 == 1)[None, :, None, None]

    def pad_in(w):
        z = jnp.zeros_like(w)
        return jnp.where(odd, jnp.concatenate([z, w], -1), jnp.concatenate([w, z], -1)).astype(BF16)

    def pad_out(w):
        wt = jnp.swapaxes(w, 2, 3)
        z = jnp.zeros_like(wt)
        return jnp.where(odd, jnp.concatenate([z, wt], 2), jnp.concatenate([wt, z], 2)).astype(BF16)

    return dict(m=m.astype(BF16), win_r=pad_in(win_r), win_i=pad_in(win_i),
                wout_r=pad_out(wout_r), wout_i=pad_out(wout_i),
                a16_r=a16_r.reshape(2, 1, S5_LANES), a16_i=a16_i.reshape(2, 1, S5_LANES))


def _s5_mix(u, ops, h0_re, h0_im):
    b, seq, _ = u.shape
    nc = seq // S5_CHUNK
    ug = u.reshape(b, nc, S5_CHUNK, S5_GROUPS, S5_GROUP).transpose(0, 3, 1, 2, 4)
    ug = ug.reshape(b, S5_GROUPS, nc, S5_ROW).astype(BF16)
    s_re, s_im = _s5_in_call(ug, ops["win_r"], ops["win_i"])
    h_re, h_im, f_re, f_im = _s5_scan_call(s_re, s_im, ops["a16_r"], ops["a16_i"], h0_re, h0_im)
    y = _s5_out_call(ug, ops["m"], h_re, h_im, ops["wout_r"], ops["wout_i"])
    y = y.reshape(b, S5_GROUPS, nc, S5_CHUNK, S5_GROUP).transpose(0, 2, 3, 1, 4).reshape(b, seq, S5_WIDTH)
    return y, f_re, f_im


def _conv_kernel(rows, width, n_hor, has_ver, nt, *refs):
    if has_ver:
        (xh_ref, vp_ref, vc_ref, vn_ref, wh_ref, wv_ref, b_ref, g_ref, be_ref,
         o_ref, hs_scr, vs_scr) = refs
    else:
        xh_ref, wh_ref, b_ref, g_ref, be_ref, o_ref, hs_scr = refs
    i = pl.program_id(1)
    tokens = rows * width
    slot = width + 2 * CONV_PAD
    half = CONV_K // 2

    hs_scr[...] = jnp.zeros_like(hs_scr)
    for r in range(rows):
        hs_scr[r, pl.ds(CONV_PAD, width), :] = xh_ref[pl.ds(r * width, width), :]
    wh = wh_ref[...]
    acc_h = jnp.zeros((tokens, n_hor), F32)
    for tap in range(CONV_K):
        win = hs_scr[:, pl.ds(CONV_PAD - half + tap, width), :].reshape(tokens, n_hor)
        acc_h = acc_h + win * wh[tap:tap + 1, :]

    if has_ver:
        vs_scr[pl.ds(0, tokens), :] = jnp.where(i == 0, 0.0, vp_ref[...])
        vs_scr[pl.ds(tokens, tokens), :] = vc_ref[...]
        vs_scr[pl.ds(2 * tokens, tokens), :] = jnp.where(i == nt - 1, 0.0, vn_ref[...])
        wv = wv_ref[...]
        acc_v = jnp.zeros((tokens, CONV_WIDTH - n_hor), F32)
        for tap in range(CONV_K):
            acc_v = acc_v + vs_scr[pl.ds(tokens + (tap - half) * width, tokens), :] * wv[tap:tap + 1, :]
        x = jnp.concatenate([acc_h, acc_v], axis=1)
    else:
        x = acc_h
    x = x + b_ref[...]
    xc = x - jnp.mean(x, axis=-1, keepdims=True)
    y = xc * lax.rsqrt(jnp.mean(xc * xc, axis=-1, keepdims=True) + EPS) * g_ref[...] + be_ref[...]
    o_ref[...] = _silu(y)


def _conv_call(xc, dw_w, dw_b, ln_g, ln_b, grid_rows):
    b, seq, ch = xc.shape
    vec = pl.BlockSpec((1, ch), lambda b, i: (0, 0))
    if grid_rows > 0:
        width, rows, n_hor = GRID_W, 16, ch // 2
        tokens = rows * width
        nt = seq // tokens
        half_spec = lambda f: pl.BlockSpec((None, tokens, n_hor), f)
        in_specs = [half_spec(lambda b, i: (b, i, 0)),
                    half_spec(lambda b, i: (b, jnp.maximum(i - 1, 0), 1)),
                    half_spec(lambda b, i: (b, i, 1)),
                    half_spec(lambda b, i: (b, jnp.minimum(i + 1, nt - 1), 1)),
                    pl.BlockSpec((CONV_K, n_hor), lambda b, i: (0, 0)),
                    pl.BlockSpec((CONV_K, n_hor), lambda b, i: (0, 1)),
                    vec, vec, vec]
        args = [xc, xc, xc, xc, dw_w, dw_w, dw_b, ln_g, ln_b]
        scratch = [pltpu.VMEM((rows, width + 2 * CONV_PAD, n_hor), F32),
                   pltpu.VMEM((3 * tokens, ch - n_hor), F32)]
        has_ver = True
    else:
        width, rows, n_hor = seq, 1, ch
        tokens = seq
        nt = 1
        in_specs = [pl.BlockSpec((None, tokens, ch), lambda b, i: (b, 0, 0)),
                    pl.BlockSpec((CONV_K, ch), lambda b, i: (0, 0)),
                    vec, vec, vec]
        args = [xc, dw_w, dw_b, ln_g, ln_b]
        scratch = [pltpu.VMEM((rows, width + 2 * CONV_PAD, n_hor), F32)]
        has_ver = False
    return pl.pallas_call(
        functools.partial(_conv_kernel, rows, width, n_hor, has_ver, nt),
        grid=(b, nt),
        in_specs=in_specs,
        out_specs=pl.BlockSpec((None, tokens, ch), lambda b, i: (b, i, 0)),
        out_shape=jax.ShapeDtypeStruct((b, seq, ch), F32),
        scratch_shapes=scratch,
        compiler_params=_params("parallel", "parallel"),
        name="conformer_conv",
    )(*args)


def _lb_kernel(x_ref, o_ref):
    x = x_ref[...]
    n = x.shape[0]
    rows = [x[r:r + 1, :] for r in range(n)]
    mx = functools.reduce(jnp.maximum, rows)
    ex = [jnp.exp(r - mx) for r in rows]
    tot = functools.reduce(lambda p, q: p + q, ex)
    run = None
    for r in range(n):
        run = ex[r] / tot if run is None else run + ex[r] / tot
        o_ref[pl.ds(r, 1), :] = run


def _lb_call(logits):
    return pl.pallas_call(_lb_kernel, out_shape=jax.ShapeDtypeStruct(logits.shape, F32), name="hgrn_lb")(logits)


def _hgrn_kernel(rev, t_blk, nb, q_ref, f_ref, v_ref, lb_ref, s0_ref, o_ref, sfin_ref,
                 st_scr, qin_scr, kout_scr, oi_scr, cd_scr):
    i = pl.program_id(2)
    c = HGRN_CHUNK
    nch = t_blk // c

    @pl.when(i == 0)
    def _():
        st_scr[...] = s0_ref[...]

    lb = lb_ref[...]
    f = lb + (1.0 - lb) * jax.nn.sigmoid(f_ref[...])
    k = 1.0 - f
    log_f = jnp.log(f)
    row = lax.broadcasted_iota(jnp.int32, (t_blk, t_blk), 0)
    col = lax.broadcasted_iota(jnp.int32, (t_blk, t_blk), 1)
    same = (row // c) == (col // c)
    incl = same & ((row <= col) if rev else (row >= col))
    bc = _mm_exact_lhs(incl.astype(BF16), log_f)
    tot = _mm_exact_lhs(same.astype(BF16), log_f)
    q = q_ref[...]
    v = v_ref[...]
    q_in = q * jnp.exp(bc)
    k_in = k * jnp.exp(-bc)
    attn = jnp.where(incl, _mm_nt(q_in, k_in), 0.0)
    qin_scr[...] = q_in
    kout_scr[...] = k * jnp.exp(tot - bc)
    oi_scr[...] = _mm(attn, v)
    cd_scr[...] = jnp.exp(tot)

    def chunk(ci, carry):
        cidx = (nch - 1 - ci) if rev else ci
        r0 = pl.multiple_of(cidx * c, c)
        st = st_scr[...]
        o_ref[pl.ds(r0, c), :] = _mm_nt(qin_scr[pl.ds(r0, c), :], st) + oi_scr[pl.ds(r0, c), :]
        vc = v_ref[pl.ds(r0, c), :]
        st_scr[...] = st * cd_scr[pl.ds(r0, 1), :] + _mm_tn(vc, kout_scr[pl.ds(r0, c), :])
        return carry

    lax.fori_loop(0, nch, chunk, 0)

    @pl.when(i == nb - 1)
    def _():
        sfin_ref[...] = st_scr[...]


def _hgrn_call(q, fgate, v, lb_row, s0, rev):
    b, seq, _ = q.shape
    t_blk = min(256, seq)
    nb = seq // t_blk
    hh = HGRN_HEADS

    def blk(i):
        return (nb - 1 - i) if rev else i

    tok = pl.BlockSpec((None, t_blk, HGRN_DIM), lambda b, h, i: (b, blk(i), h))
    state_spec = pl.BlockSpec((None, None, HGRN_DIM, HGRN_DIM), lambda b, h, i: (b, h, 0, 0))
    return pl.pallas_call(
        functools.partial(_hgrn_kernel, rev, t_blk, nb),
        grid=(b, hh, nb),
        in_specs=[tok, tok, tok, pl.BlockSpec((1, HGRN_DIM), lambda b, h, i: (0, h)), state_spec],
        out_specs=[tok, state_spec],
        out_shape=[jax.ShapeDtypeStruct((b, seq, HGRN_WIDTH), F32),
                   jax.ShapeDtypeStruct((b, hh, HGRN_DIM, HGRN_DIM), F32)],
        scratch_shapes=[pltpu.VMEM((HGRN_DIM, HGRN_DIM), F32),
                        pltpu.VMEM((t_blk, HGRN_DIM), F32),
                        pltpu.VMEM((t_blk, HGRN_DIM), F32),
                        pltpu.VMEM((t_blk, HGRN_DIM), F32),
                        pltpu.VMEM((t_blk, HGRN_DIM), F32)],
        compiler_params=_params("parallel", "parallel", "arbitrary"),
        name="hgrn_bwd" if rev else "hgrn_fwd",
    )(q, fgate, v, lb_row, s0)


def _head_norm(o, gate, g):
    outs = []
    for hd in range(o.shape[1] // 128):
        x = o[:, hd * 128:(hd + 1) * 128]
        x = x * lax.rsqrt(jnp.mean(x * x, axis=-1, keepdims=True) + EPS) * g
        outs.append(x * _silu(gate[:, hd * 128:(hd + 1) * 128]))
    return jnp.concatenate(outs, axis=1)


def _out_ab_kernel(h_ref, gt_ref, of_ref, ob_ref, z_ref, y5_ref, u_ref, ng_ref, dsk_ref, glw_ref, glb_ref,
                   wout_ref, o_ref):
    a = _head_norm(of_ref[...] + ob_ref[...], z_ref[...], ng_ref[...])
    y = y5_ref[...] + dsk_ref[...] * u_ref[...]
    y = 0.5 * y * (1.0 + jnp.tanh(math.sqrt(2.0 / math.pi) * (y + 0.044715 * (y * y * y))))
    bmix = y * jax.nn.sigmoid(_mm(y, glw_ref[...]) + glb_ref[...])
    mix = _mm(jnp.concatenate([a, bmix], axis=1), wout_ref[...])
    o_ref[...] = h_ref[...] + gt_ref[...] * mix


def _out_cd_kernel(h_ref, gt_ref, c_ref, of_ref, ob_ref, g_ref, ng_ref, wout_ref, o_ref):
    dmix = _head_norm(of_ref[...] + ob_ref[...], g_ref[...], ng_ref[...])
    mix = _mm(jnp.concatenate([c_ref[...], dmix], axis=1), wout_ref[...])
    o_ref[...] = h_ref[...] + gt_ref[...] * mix


def _out_call(kern, name, h, mod, layer, ctx, toks, consts):
    b, seq, _ = h.shape
    tm = min(256, seq)
    in_specs = [pl.BlockSpec((None, tm, D_MODEL), lambda b, i: (b, i, 0)), _mod_spec(layer, 1, 2, ctx)]
    in_specs += [pl.BlockSpec((None, tm, t.shape[2]), lambda b, i: (b, i, 0)) for t in toks]
    in_specs += [pl.BlockSpec(cst.shape, lambda b, i: (0, 0)) for cst in consts]
    return pl.pallas_call(
        kern,
        grid=(b, seq // tm),
        in_specs=in_specs,
        out_specs=pl.BlockSpec((None, tm, D_MODEL), lambda b, i: (b, i, 0)),
        out_shape=jax.ShapeDtypeStruct(h.shape, F32),
        compiler_params=_params("parallel", "parallel"),
        name=name,
    )(h, mod, *toks, *consts)


def _mixer_ab(h, hc, mod, layer, need_ctx, prm):
    ng = prm["norm_g1"]
    w_in = prm["w_in"]
    outs = {}
    gdn_state = [jnp.zeros((h.shape[0], GDN_HEADS, GDN_DIM, GDN_DIM), F32)] * 2
    s5_state = [jnp.zeros((2, h.shape[0], 1, S5_LANES), F32)] * 2
    for ctx, x in ((True, hc), (False, h)):
        qkv, z, gates, u = _inproj_call(x, mod, layer, ctx, ng, w_in, glu=False)
        o_f, sf = _gdn_call(qkv, gates, prm["conv_w"], prm["alog"], prm["dtb"], gdn_state[0], rev=False)
        o_b, sb = _gdn_call(qkv, gates, prm["conv_w"], prm["alog"], prm["dtb"], gdn_state[1], rev=True)
        y5, f_re, f_im = _s5_mix(u, prm["s5"], s5_state[0], s5_state[1])
        gdn_state = [sf, sb]
        s5_state = [f_re, f_im]
        if ctx and not need_ctx:
            continue
        outs[ctx] = _out_call(_out_ab_kernel, "out_ab", x, mod, layer, ctx, [o_f, o_b, z, y5, u],
                              [prm["gdn_g"], prm["s5_d"], prm["glu_w"], prm["glu_b"], prm["w_out"]])
    return outs[False], outs.get(True)


def _mixer_cd(h, hc, mod, layer, need_ctx, prm, grid_rows):
    ng = prm["norm_g1"]
    outs = {}
    state = [jnp.zeros((h.shape[0], HGRN_HEADS, HGRN_DIM, HGRN_DIM), F32)] * 2
    for ctx, x in ((True, hc), (False, h)):
        xc, q, f_f, f_b, iv, g = _inproj_call(x, mod, layer, ctx, ng, prm["w_in"], glu=True)
        o_f, sf = _hgrn_call(q, f_f, iv, prm["lb"], state[0], rev=False)
        o_b, sb = _hgrn_call(q, f_b, iv, prm["lb"], state[1], rev=True)
        state = [sf, sb]
        if ctx and not need_ctx:
            continue
        cmix = _conv_call(xc, prm["dw_w"], prm["dw_b"], prm["ln_g"], prm["ln_b"], 0 if ctx else grid_rows)
        outs[ctx] = _out_call(_out_cd_kernel, "out_cd", x, mod, layer, ctx, [cmix, o_f, o_b, g],
                              [prm["hgrn_g"], prm["w_out"]])
    return outs[False], outs.get(True)


def kernel(x, c, ctx, c_ctx, ada_w, ada_b, norm_g, ffn_w_up, ffn_w_down, ab_w_in, ab_w_out, gdn_conv_w, gdn_a_log, gdn_dt_bias, gdn_norm_g, s5_lambda_re, s5_lambda_im, s5_log_step, s5_b_re, s5_b_im, s5_c_re, s5_c_im, s5_d, s5_glu_w, s5_glu_b, cd_w_in, cd_w_out, conv_dw_w, conv_dw_b, conv_ln_g, conv_ln_b, hgrn_lb_logits, hgrn_norm_g, final_norm_g):
    depth = ada_w.shape[0]
    batch = x.shape[0]
    grid_rows = x.shape[1] // GRID_W
    assert batch <= 2 and x.shape[1] % 1024 == 0 and ctx.shape[1] % 256 == 0

    cvec = jnp.zeros((8, D_MODEL), F32).at[:batch].set(c).at[2].set(c_ctx)
    mod = _ada_call(cvec, ada_w, ada_b)
    lb_all = _lb_call(hgrn_lb_logits)
    row = lambda v: v.reshape(1, -1)
    gw, kw = GDN_WIDTH, HGRN_WIDTH

    h, hc = x, ctx
    for l in range(depth):
        last = l == depth - 1
        w_up = ffn_w_up[l].astype(BF16)
        w_down = ffn_w_down[l].astype(BF16)
        h = _ffn_call(h, mod, l, 0, False, row(norm_g[l, 0]), w_up[0], w_down[0])
        hc = _ffn_call(hc, mod, l, 0, True, row(norm_g[l, 0]), w_up[0], w_down[0])
        if l % 2 == 0:
            e = l // 2
            w = ab_w_in[e].astype(BF16)
            w_gate = jnp.zeros((D_MODEL, 128), BF16).at[:, :4 * GDN_HEADS].set(w[:, 4 * gw:4 * gw + 4 * GDN_HEADS])
            pad8 = lambda v: jnp.zeros((1, 128), F32).at[0, :2 * GDN_HEADS].set(v.reshape(-1))
            prm = dict(
                norm_g1=row(norm_g[l, 1]),
                w_in=[w[:, :3 * gw], w[:, 3 * gw:4 * gw], w_gate, w[:, 4 * gw + 4 * GDN_HEADS:]],
                conv_w=gdn_conv_w[e], alog=pad8(gdn_a_log[e]), dtb=pad8(gdn_dt_bias[e]),
                gdn_g=row(gdn_norm_g[e]),
                s5=_s5_prepare_ops(s5_lambda_re[e], s5_lambda_im[e], s5_log_step[e], s5_b_re[e], s5_b_im[e],
                                   s5_c_re[e], s5_c_im[e]),
                s5_d=row(s5_d[e]), glu_w=s5_glu_w[e].astype(BF16), glu_b=row(s5_glu_b[e]),
                w_out=ab_w_out[e].astype(BF16))
            h, hc_new = _mixer_ab(h, hc, mod, l, not last, prm)
        else:
            o = l // 2
            w = cd_w_in[o].astype(BF16)
            cw = CONV_WIDTH
            prm = dict(
                norm_g1=row(norm_g[l, 1]),
                w_in=[w[:, :cw], w[:, cw:2 * cw]] + [w[:, 2 * cw + k * kw:2 * cw + (k + 1) * kw] for k in range(5)],
                lb=lb_all[o:o + 1], hgrn_g=row(hgrn_norm_g[o]),
                dw_w=conv_dw_w[o], dw_b=row(conv_dw_b[o]), ln_g=row(conv_ln_g[o]), ln_b=row(conv_ln_b[o]),
                w_out=cd_w_out[o].astype(BF16))
            h, hc_new = _mixer_cd(h, hc, mod, l, not last, prm, grid_rows)
        h = _ffn_call(h, mod, l, 2, False, row(norm_g[l, 2]), w_up[1], w_down[1],
                      final_g=row(final_norm_g) if last else None)
        if not last:
            hc = _ffn_call(hc_new, mod, l, 2, True, row(norm_g[l, 2]), w_up[1], w_down[1])
    return h
```

```python
import functools
import math

import jax
import jax.numpy as jnp
from jax import lax
from jax.experimental import pallas as pl
from jax.experimental.pallas import tpu as pltpu

F32 = jnp.float32
BF16 = jnp.bfloat16
HIGHEST = lax.Precision.HIGHEST
EPS = 1e-6

D_MODEL = 1024
GRID_W = 64
FFN_DIM = 2816

GDN_HEADS = 4
GDN_DIM = 128
GDN_WIDTH = GDN_HEADS * GDN_DIM
GDN_CONV = 5
GDN_CHUNK = 256

S5_WIDTH = 512
S5_GROUP = 16
S5_GROUPS = 32
S5_STATE = 64
S5_CHUNK = 16
S5_ROW = S5_CHUNK * S5_GROUP
S5_LANES = S5_GROUPS * S5_STATE

CONV_WIDTH = 512
CONV_K = 31
CONV_PAD = 16

HGRN_HEADS = 4
HGRN_DIM = 128
HGRN_WIDTH = HGRN_HEADS * HGRN_DIM
HGRN_CHUNK = 16

V7X_VMEM_LIMIT = 48 * 1024 * 1024


def _params(*sem):
    return pltpu.CompilerParams(dimension_semantics=sem, vmem_limit_bytes=V7X_VMEM_LIMIT)


def _silu(x):
    return x * jax.nn.sigmoid(x)


def _mm(a, b):
    return jnp.dot(a.astype(BF16), b.astype(BF16), preferred_element_type=F32)


def _mm_nt(a, b):
    return lax.dot_general(a.astype(BF16), b.astype(BF16), (((1,), (1,)), ((), ())),
                           preferred_element_type=F32)


def _mm_tn(a, b):
    return lax.dot_general(a.astype(BF16), b.astype(BF16), (((0,), (0,)), ((), ())),
                           preferred_element_type=F32)


def _split3(x):
    hi = x.astype(BF16)
    r1 = x - hi.astype(F32)
    mid = r1.astype(BF16)
    lo = (r1 - mid.astype(F32)).astype(BF16)
    return hi, mid, lo


def _mm_exact_lhs(a_bf16, x):
    hi, mid, lo = _split3(x)
    dot = functools.partial(jnp.dot, preferred_element_type=F32)
    return dot(a_bf16, hi) + dot(a_bf16, mid) + dot(a_bf16, lo)


def _mm_exact_rhs(x, b_bf16):
    hi, mid, lo = _split3(x)
    dot = functools.partial(jnp.dot, preferred_element_type=F32)
    return dot(hi, b_bf16) + dot(mid, b_bf16) + dot(lo, b_bf16)


def _mm_x3(a, b):
    ah = a.astype(BF16)
    al = (a - ah.astype(F32)).astype(BF16)
    bh = b.astype(BF16)
    bl = (b - bh.astype(F32)).astype(BF16)
    dot = functools.partial(jnp.dot, preferred_element_type=F32)
    return dot(ah, bh) + dot(ah, bl) + dot(al, bh)


def _ada_norm(x, g, scale, shift):
    y = x * lax.rsqrt(jnp.mean(x * x, axis=-1, keepdims=True) + EPS) * g
    return y * (1.0 + scale) + shift


def _ada_kernel(c_ref, w_ref, b_ref, o_ref):
    o_ref[...] = _mm(_silu(c_ref[...]), w_ref[...]) + b_ref[...]


def _ada_call(cvec, ada_w, ada_b):
    depth = ada_w.shape[0]
    ncol = ada_w.shape[2] // D_MODEL
    out = pl.pallas_call(
        _ada_kernel,
        grid=(depth, ncol),
        in_specs=[pl.BlockSpec((8, D_MODEL), lambda l, j: (0, 0)),
                  pl.BlockSpec((None, D_MODEL, D_MODEL), lambda l, j: (l, 0, j)),
                  pl.BlockSpec((None, 1, D_MODEL), lambda l, j: (l, 0, j))],
        out_specs=pl.BlockSpec((None, None, 8, D_MODEL), lambda l, j: (l, j, 0, 0)),
        out_shape=jax.ShapeDtypeStruct((depth, ncol, 8, D_MODEL), F32),
        compiler_params=_params("arbitrary", "arbitrary"),
        name="ada_mod",
    )(cvec, ada_w, ada_b.reshape(depth, 1, -1))
    return out.reshape(depth * ncol * 8, 1, D_MODEL)


def _mod_spec(layer, sub, kind, ctx):
    base = (layer * 9 + sub * 3 + kind) * 8
    if ctx:
        return pl.BlockSpec((None, 1, D_MODEL), lambda b, *_: (base + 2, 0, 0))
    return pl.BlockSpec((None, 1, D_MODEL), lambda b, *_: (base + b, 0, 0))


def _ffn_kernel(tf, final, h_ref, sh_ref, sc_ref, gt_ref, g_ref, wup_ref, wd_ref, *rest):
    if final:
        fg_ref, o_ref = rest
    else:
        (o_ref,) = rest
    x = h_ref[...]
    xn = _ada_norm(x, g_ref[...], sc_ref[...], sh_ref[...]).astype(BF16)
    acc = None
    for j in range(FFN_DIM // tf):
        gate = jnp.dot(xn, wup_ref[:, j * tf:(j + 1) * tf], preferred_element_type=F32)
        up = jnp.dot(xn, wup_ref[:, FFN_DIM + j * tf:FFN_DIM + (j + 1) * tf], preferred_element_type=F32)
        act = (_silu(gate) * up).astype(BF16)
        part = jnp.dot(act, wd_ref[j * tf:(j + 1) * tf, :], preferred_element_type=F32)
        acc = part if acc is None else acc + part
    y = x + 0.5 * gt_ref[...] * acc
    if final:
        y = y * lax.rsqrt(jnp.mean(y * y, axis=-1, keepdims=True) + EPS) * fg_ref[...]
    o_ref[...] = y


def _ffn_call(h, mod, layer, sub, ctx, norm_g_row, w_up, w_down, final_g=None):
    b, seq, _ = h.shape
    tm = min(512, seq)
    tf = 256
    final = final_g is not None
    resident = dict(pipeline_mode=pl.Buffered(1))
    in_specs = [
        pl.BlockSpec((None, tm, D_MODEL), lambda b, i: (b, i, 0)),
        _mod_spec(layer, sub, 0, ctx), _mod_spec(layer, sub, 1, ctx), _mod_spec(layer, sub, 2, ctx),
        pl.BlockSpec((1, D_MODEL), lambda b, i: (0, 0)),
        pl.BlockSpec((D_MODEL, 2 * FFN_DIM), lambda b, i: (0, 0), **resident),
        pl.BlockSpec((FFN_DIM, D_MODEL), lambda b, i: (0, 0), **resident),
    ]
    args = [h, mod, mod, mod, norm_g_row, w_up, w_down]
    if final:
        in_specs.append(pl.BlockSpec((1, D_MODEL), lambda b, i: (0, 0)))
        args.append(final_g)
    return pl.pallas_call(
        functools.partial(_ffn_kernel, tf, final),
        grid=(b, seq // tm),
        in_specs=in_specs,
        out_specs=pl.BlockSpec((None, tm, D_MODEL), lambda b, i: (b, i, 0)),
        out_shape=jax.ShapeDtypeStruct(h.shape, F32),
        compiler_params=_params("parallel", "parallel"),
        name="ffn",
    )(*args)


def _inproj_kernel(nw, glu, h_ref, sh_ref, sc_ref, g_ref, *refs):
    w_refs, o_refs = refs[:nw], refs[nw:]
    xn = _ada_norm(h_ref[...], g_ref[...], sc_ref[...], sh_ref[...]).astype(BF16)
    outs = [jnp.dot(xn, w[...], preferred_element_type=F32) for w in w_refs]
    if glu:
        outs = [outs[0] * jax.nn.sigmoid(outs[1])] + outs[2:]
    for o_ref, val in zip(o_refs, outs):
        o_ref[...] = val


def _inproj_call(h, mod, layer, ctx, norm_g_row, weights, glu):
    b, seq, _ = h.shape
    tm = min(256, seq)
    widths = [w.shape[1] for w in weights]
    out_widths = widths[1:] if glu else widths
    in_specs = [pl.BlockSpec((None, tm, D_MODEL), lambda b, i: (b, i, 0)),
                _mod_spec(layer, 1, 0, ctx), _mod_spec(layer, 1, 1, ctx),
                pl.BlockSpec((1, D_MODEL), lambda b, i: (0, 0))]
    in_specs += [pl.BlockSpec((D_MODEL, w), lambda b, i: (0, 0)) for w in widths]
    return pl.pallas_call(
        functools.partial(_inproj_kernel, len(weights), glu),
        grid=(b, seq // tm),
        in_specs=in_specs,
        out_specs=[pl.BlockSpec((None, tm, w), lambda b, i: (b, i, 0)) for w in out_widths],
        out_shape=[jax.ShapeDtypeStruct((b, seq, w), F32) for w in out_widths],
        compiler_params=_params("parallel", "parallel"),
        name="inproj",
    )(h, mod, mod, norm_g_row, *weights)


def _gdn_kernel(rev, t_blk, nb,
                qm, qp, qn, km, kp, kn, vm, vp, vn, cwq, cwk, cwv, gates_ref, alog_ref, dtb_ref, s0_ref,
                o_ref, sfin_ref,
                s_scr, pad_scr, q_scr, k_scr, v_scr, sel_scr, ut_scr, w_scr, qd_scr, kd_scr, qk_scr, cd_scr, gt_scr):
    i = pl.program_id(1)
    blk = (nb - 1 - i) if rev else i
    c = GDN_CHUNK
    nch = t_blk // c
    hh = GDN_HEADS
    width = hh * GDN_DIM

    @pl.when(i == 0)
    def _():
        s_scr[...] = s0_ref[...]

    def conv(main, prev, nxt, cw):
        pad_scr[pl.ds(0, 8), :] = jnp.where(blk == 0, 0.0, prev[...])
        pad_scr[pl.ds(8, t_blk), :] = main[...]
        pad_scr[pl.ds(8 + t_blk, 8), :] = jnp.where(blk == nb - 1, 0.0, nxt[...])
        w = cw[...]
        acc = jnp.zeros((t_blk, width), F32)
        for tap in range(GDN_CONV):
            acc = acc + pad_scr[pl.ds(8 - GDN_CONV // 2 + tap, t_blk), :] * w[tap:tap + 1, :]
        return _silu(acc)

    def l2n_heads(x, scale):
        outs = []
        for hd in range(hh):
            xh = x[:, hd * GDN_DIM:(hd + 1) * GDN_DIM]
            outs.append(xh * (lax.rsqrt(jnp.sum(xh * xh, axis=-1, keepdims=True) + EPS) * scale))
        return jnp.concatenate(outs, axis=1)

    q_scr[...] = l2n_heads(conv(qm, qp, qn, cwq), GDN_DIM ** -0.5)
    k_scr[...] = l2n_heads(conv(km, kp, kn, cwk), 1.0)
    v_scr[...] = conv(vm, vp, vn, cwv)

    gates = gates_ref[...]
    lane = lax.broadcasted_iota(jnp.int32, gates.shape, 1)
    log_a = -jnp.exp(alog_ref[...]) * jax.nn.softplus(gates + dtb_ref[...])
    trow = lax.broadcasted_iota(jnp.int32, (t_blk, t_blk), 0)
    tcol = lax.broadcasted_iota(jnp.int32, (t_blk, t_blk), 1)
    cum = ((trow // c) == (tcol // c)) & ((trow <= tcol) if rev else (trow >= tcol))
    g_all = _mm_exact_lhs(cum.astype(BF16), log_a)
    for m in range(nch):
        gt_scr[m] = g_all[m * c:(m + 1) * c, :].T
    mixed = jnp.where(lane < 2 * hh, g_all, jax.nn.sigmoid(gates))
    d_off = hh if rev else 0
    src = lax.broadcasted_iota(jnp.int32, (128, 2 * width), 0)
    dst = lax.broadcasted_iota(jnp.int32, (128, 2 * width), 1)
    want = d_off + dst // 256 + jnp.where((dst // 128) % 2 == 1, 2 * hh, 0)
    sel_scr[...] = _mm_exact_rhs(mixed, (src == want).astype(BF16))

    row = lax.broadcasted_iota(jnp.int32, (c, c), 0)
    col = lax.broadcasted_iota(jnp.int32, (c, c), 1)
    incl = (row <= col) if rev else (row >= col)
    strict = (row < col) if rev else (row > col)
    last = 0 if rev else c - 1

    def inverse_minus_identity(a_mat):
        p = jnp.where((row // 16) == (col // 16), -a_mat, 0.0)
        n = p
        for _ in range(3):
            p = _mm(p, p)
            n = n + p + _mm(n, p)
        d = 16
        while d < c:
            off = jnp.where(((row // (2 * d)) == (col // (2 * d))) & ((row // d) != (col // d)), a_mat, 0.0)
            x = off + _mm(n, off)
            n = n - (x + _mm(x, n))
            d *= 2
        return n

    def prep(m, carry):
        rows = pl.ds(pl.multiple_of(m * c, c), c)
        for hd in range(hh):
            lanes = pl.ds(hd * GDN_DIM, GDN_DIM)
            q = q_scr[rows, lanes]
            k = k_scr[rows, lanes]
            v = v_scr[rows, lanes]
            g = sel_scr[rows, pl.ds(hd * 256, 128)]
            beta = sel_scr[rows, pl.ds(hd * 256 + 128, 128)]
            g_row = gt_scr[m, pl.ds(d_off + hd, 1), :]
            diff = jnp.concatenate([g, g], axis=1) - g_row
            decay = jnp.where(incl, jnp.exp(jnp.where(incl, diff, 0.0)), 0.0)
            kq = _mm_nt(jnp.concatenate([q, k], axis=0), k)
            a_mat = jnp.where(strict, kq[c:] * decay, 0.0) * jnp.concatenate([beta, beta], axis=1)
            n_mat = inverse_minus_identity(a_mat)
            eg = jnp.exp(g)
            rhs = jnp.concatenate([v * beta, k * (beta * eg)], axis=1)
            sol = rhs + _mm_x3(n_mat, rhs)
            g_last = g[last:last + 1, :]
            ut_scr[rows, lanes] = sol[:, :GDN_DIM]
            w_scr[rows, lanes] = sol[:, GDN_DIM:]
            qd_scr[rows, lanes] = q * eg
            kd_scr[rows, lanes] = k * jnp.exp(g_last - g)
            qk_scr[rows, pl.ds(hd * c, c)] = jnp.where(incl, kq[:c] * decay, 0.0)
            cd_scr[m, hd] = jnp.broadcast_to(jnp.exp(g_last), (8, GDN_DIM))
        return carry

    lax.fori_loop(0, nch, prep, 0)

    def step(ci, carry):
        cidx = (nch - 1 - ci) if rev else ci
        rows = pl.ds(pl.multiple_of(cidx * c, c), c)
        for hd in range(hh):
            lanes = pl.ds(hd * GDN_DIM, GDN_DIM)
            s = s_scr[hd]
            ws = _mm(jnp.concatenate([w_scr[rows, lanes], qd_scr[rows, lanes]], axis=0), s)
            u = ut_scr[rows, lanes] - ws[:c]
            o_ref[rows, lanes] = ws[c:] + _mm(qk_scr[rows, pl.ds(hd * c, c)], u)
            s_scr[hd] = s * cd_scr[cidx, hd][0:1, :] + _mm_tn(kd_scr[rows, lanes], u)
        return carry

    lax.fori_loop(0, nch, step, 0)

    @pl.when(i == nb - 1)
    def _():
        sfin_ref[...] = s_scr[...]


def _gdn_call(qkv, gates, conv_w, alog_row, dtb_row, s0, rev):
    b, seq, _ = qkv.shape
    t_blk = min(512, seq)
    nb = seq // t_blk
    r8 = t_blk // 8
    hh = GDN_HEADS
    width = GDN_WIDTH

    def blk(i):
        return (nb - 1 - i) if rev else i

    def main(part):
        return pl.BlockSpec((None, t_blk, width), lambda b, i: (b, blk(i), part))

    def prev(part):
        return pl.BlockSpec((None, 8, width), lambda b, i: (b, jnp.maximum(blk(i) * r8 - 1, 0), part))

    def nxt(part):
        return pl.BlockSpec((None, 8, width), lambda b, i: (b, jnp.minimum((blk(i) + 1) * r8, seq // 8 - 1), part))

    def cw(part):
        return pl.BlockSpec((GDN_CONV, width), lambda b, i: (0, part))

    state_spec = pl.BlockSpec((None, hh, GDN_DIM, GDN_DIM), lambda b, i: (b, 0, 0, 0))
    in_specs = []
    args = []
    for part in range(3):
        in_specs += [main(part), prev(part), nxt(part)]
        args += [qkv, qkv, qkv]
    in_specs += [cw(0), cw(1), cw(2),
                 pl.BlockSpec((None, t_blk, 128), lambda b, i: (b, blk(i), 0)),
                 pl.BlockSpec((1, 128), lambda b, i: (0, 0)),
                 pl.BlockSpec((1, 128), lambda b, i: (0, 0)),
                 state_spec]
    args += [conv_w, conv_w, conv_w, gates, alog_row, dtb_row, s0]
    tok = lambda w: pltpu.VMEM((t_blk, w), F32)
    return pl.pallas_call(
        functools.partial(_gdn_kernel, rev, t_blk, nb),
        grid=(b, nb),
        in_specs=in_specs,
        out_specs=[pl.BlockSpec((None, t_blk, width), lambda b, i: (b, blk(i), 0)), state_spec],
        out_shape=[jax.ShapeDtypeStruct((b, seq, width), F32),
                   jax.ShapeDtypeStruct((b, hh, GDN_DIM, GDN_DIM), F32)],
        scratch_shapes=[pltpu.VMEM((hh, GDN_DIM, GDN_DIM), F32),
                        pltpu.VMEM((t_blk + 16, width), F32),
                        tok(width), tok(width), tok(width), tok(2 * width),
                        tok(width), tok(width), tok(width), tok(width), tok(hh * GDN_CHUNK),
                        pltpu.VMEM((t_blk // GDN_CHUNK, hh, 8, GDN_DIM), F32),
                        pltpu.VMEM((t_blk // GDN_CHUNK, 128, GDN_CHUNK), F32)],
        compiler_params=_params("parallel", "arbitrary"),
        name="gdn_bwd" if rev else "gdn_fwd",
    )(*args)


def _s5_ops_kernel(lr_ref, li_ref, ls_ref, btr_ref, bti_ref, cr_ref, ci_ref,
                   m_ref, winr_ref, wini_ref, woutr_ref, wouti_ref, a16r_ref, a16i_ref):
    d = pl.program_id(0)
    lr = lr_ref[...]
    li = li_ref[...]
    dt = jnp.exp(ls_ref[...])

    def apow(kk):
        mag = jnp.exp(lr * dt * kk)
        ang = li * dt * kk
        return mag * jnp.cos(ang), mag * jnp.sin(ang)

    ar, ai = apow(1.0)
    den = lr * lr + li * li
    nr, ni = ar - 1.0, ai
    zr = (nr * lr + ni * li) / den
    zi = (ni * lr - nr * li) / den
    btr, bti = btr_ref[...], bti_ref[...]
    bbr = zr * btr - zi * bti
    bbi = zr * bti + zi * btr
    cr, ci = cr_ref[...], ci_ref[...]

    tile = lambda x: jnp.concatenate([x] * S5_CHUNK, axis=0)
    t_row = lax.broadcasted_iota(jnp.int32, (S5_ROW, 1), 0) // S5_GROUP
    t_col = lax.broadcasted_iota(jnp.int32, (1, S5_ROW), 1) // S5_GROUP
    tv_row = jnp.where(d == 0, t_row, S5_CHUNK - 1 - t_row)
    tv_col = jnp.where(d == 0, t_col, S5_CHUNK - 1 - t_col)
    tvf = tv_row.astype(F32)

    bbr_t, bbi_t, cr_t, ci_t = tile(bbr), tile(bbi), tile(cr), tile(ci)
    pr, pi = apow(-tvf)
    xr = bbr_t * pr - bbi_t * pi
    xi = bbr_t * pi + bbi_t * pr
    pr, pi = apow(tvf)
    yr = cr_t * pr - ci_t * pi
    yi = cr_t * pi + ci_t * pr
    nt = lambda p, q: lax.dot_general(p, q, (((1,), (1,)), ((), ())), precision=HIGHEST,
                                      preferred_element_type=F32)
    m = nt(xr, yr) - nt(xi, yi)
    m_ref[...] = jnp.where(tv_col >= tv_row, m, 0.0)

    pr, pi = apow(S5_CHUNK - 1.0 - tvf)
    winr_ref[...] = bbr_t * pr - bbi_t * pi
    wini_ref[...] = bbr_t * pi + bbi_t * pr
    pr, pi = apow(tvf + 1.0)
    woutr_ref[...] = cr_t * pr - ci_t * pi
    wouti_ref[...] = -(cr_t * pi + ci_t * pr)
    a16r, a16i = apow(float(S5_CHUNK))
    a16r_ref[...] = a16r
    a16i_ref[...] = a16i


def _s5_ops_call(lam_re, lam_im, log_step, bt_re, bt_im, c_re, c_im):
    g, p, cg = S5_GROUPS, S5_STATE, S5_GROUP
    ls = jnp.broadcast_to(log_step[:, :, None, None], (2, g, 1, p))
    lam_spec = pl.BlockSpec((None, None, 1, p), lambda d, gi: (d, gi, 0, 0))
    par_spec = pl.BlockSpec((None, cg, p), lambda d, gi: (gi, 0, 0))
    out = lambda *shape: pl.BlockSpec((None, None) + shape, lambda d, gi: (d, gi, 0, 0))
    return pl.pallas_call(
        _s5_ops_kernel,
        grid=(2, g),
        in_specs=[lam_spec, lam_spec, lam_spec, par_spec, par_spec, par_spec, par_spec],
        out_specs=[out(S5_ROW, S5_ROW), out(S5_ROW, p), out(S5_ROW, p), out(S5_ROW, p), out(S5_ROW, p),
                   out(1, p), out(1, p)],
        out_shape=[jax.ShapeDtypeStruct((2, g, S5_ROW, S5_ROW), F32)]
        + [jax.ShapeDtypeStruct((2, g, S5_ROW, p), F32)] * 4
        + [jax.ShapeDtypeStruct((2, g, 1, p), F32)] * 2,
        compiler_params=_params("parallel", "parallel"),
        name="s5_ops",
    )(lam_re.reshape(2, g, 1, p), lam_im.reshape(2, g, 1, p), ls, bt_re, bt_im, c_re, c_im)


def _s5_in_kernel(u_ref, wr_ref, wi_ref, sr_ref, si_ref):
    dot = functools.partial(jnp.dot, preferred_element_type=F32)
    u0, u1 = u_ref[0], u_ref[1]
    sr_ref[...] = dot(u0, wr_ref[0]) + dot(u1, wr_ref[1])
    si_ref[...] = dot(u0, wi_ref[0]) + dot(u1, wi_ref[1])


def _s5_in_call(ug, win_r, win_i):
    b, g, nc, _ = ug.shape
    w_spec = pl.BlockSpec((None, 2, S5_ROW, 128), lambda d, b, gp: (d, gp, 0, 0))
    o_spec = pl.BlockSpec((None, None, nc, 128), lambda d, b, gp: (d, b, 0, gp))
    return pl.pallas_call(
        _s5_in_kernel,
        grid=(2, b, g // 2),
        in_specs=[pl.BlockSpec((None, 2, nc, S5_ROW), lambda d, b, gp: (b, gp, 0, 0)), w_spec, w_spec],
        out_specs=[o_spec, o_spec],
        out_shape=[jax.ShapeDtypeStruct((2, b, nc, S5_LANES), F32)] * 2,
        compiler_params=_params("parallel", "parallel", "parallel"),
        name="s5_in",
    )(ug, win_r, win_i)


def _s5_scan_kernel(nc, sr_ref, si_ref, ar_ref, ai_ref, h0r_ref, h0i_ref, hr_ref, hi_ref, fr_ref, fi_ref):
    d = pl.program_id(0)
    ar = ar_ref[...]
    ai = ai_ref[...]

    def body(i, carry):
        hr, hi = carry
        n = jnp.where(d == 0, i, nc - 1 - i)
        hr_ref[pl.ds(n, 1), :] = hr
        hi_ref[pl.ds(n, 1), :] = hi
        sr = sr_ref[pl.ds(n, 1), :]
        si = si_ref[pl.ds(n, 1), :]
        return ar * hr - ai * hi + sr, ar * hi + ai * hr + si

    hr, hi = lax.fori_loop(0, nc, body, (h0r_ref[...], h0i_ref[...]))
    fr_ref[...] = hr
    fi_ref[...] = hi


def _s5_scan_call(s_re, s_im, a16_re, a16_im, h0_re, h0_im):
    _, b, nc, _ = s_re.shape
    tl = 512
    big = pl.BlockSpec((None, None, nc, tl), lambda d, b, j: (d, b, 0, j))
    a_spec = pl.BlockSpec((None, 1, tl), lambda d, b, j: (d, 0, j))
    st_spec = pl.BlockSpec((None, None, 1, tl), lambda d, b, j: (d, b, 0, j))
    return pl.pallas_call(
        functools.partial(_s5_scan_kernel, nc),
        grid=(2, b, S5_LANES // tl),
        in_specs=[big, big, a_spec, a_spec, st_spec, st_spec],
        out_specs=[big, big, st_spec, st_spec],
        out_shape=[jax.ShapeDtypeStruct(s_re.shape, F32)] * 2
        + [jax.ShapeDtypeStruct((2, b, 1, S5_LANES), F32)] * 2,
        compiler_params=_params("parallel", "parallel", "parallel"),
        name="s5_scan",
    )(s_re, s_im, a16_re, a16_im, h0_re, h0_im)


def _s5_out_kernel(u_ref, m_ref, hr_ref, hi_ref, wr_ref, wi_ref, y_ref):
    u = u_ref[...]
    acc = None
    for d in range(2):
        y = (jnp.dot(u, m_ref[d], preferred_element_type=F32)
             + _mm(hr_ref[d], wr_ref[d]) + _mm(hi_ref[d], wi_ref[d]))
        acc = y if acc is None else acc + y
    y_ref[...] = acc


def _s5_out_call(ug, m, h_re, h_im, wout_r, wout_i):
    b, g, nc, _ = ug.shape
    h_spec = pl.BlockSpec((2, None, nc, 128), lambda b, gi: (0, b, 0, gi // 2))
    w_spec = pl.BlockSpec((2, None, 128, S5_ROW), lambda b, gi: (0, gi, 0, 0))
    return pl.pallas_call(
        _s5_out_kernel,
        grid=(b, g),
        in_specs=[pl.BlockSpec((None, None, nc, S5_ROW), lambda b, gi: (b, gi, 0, 0)),
                  pl.BlockSpec((2, None, S5_ROW, S5_ROW), lambda b, gi: (0, gi, 0, 0)),
                  h_spec, h_spec, w_spec, w_spec],
        out_specs=pl.BlockSpec((None, None, nc, S5_ROW), lambda b, gi: (b, gi, 0, 0)),
        out_shape=jax.ShapeDtypeStruct((b, g, nc, S5_ROW), F32),
        compiler_params=_params("parallel", "parallel"),
        name="s5_out",
    )(ug, m, h_re, h_im, wout_r, wout_i)


def _s5_prepare_ops(lam_re, lam_im, log_step, b_re, b_im, c_re, c_im):
    m, win_r, win_i, wout_r, wout_i, a16_r, a16_i = _s5_ops_call(
        lam_re, lam_im, log_step, jnp.swapaxes(b_re, 1, 2), jnp.swapaxes(b_im, 1, 2), c_re, c_im)
    p = S5_STATE
    odd = (jnp.arange(S5_GROUPS) % 2 == 1)[None, :, None, None]

    def pad_in(w):
        z = jnp.zeros_like(w)
        return jnp.where(odd, jnp.concatenate([z, w], -1), jnp.concatenate([w, z], -1)).astype(BF16)

    def pad_out(w):
        wt = jnp.swapaxes(w, 2, 3)
        z = jnp.zeros_like(wt)
        return jnp.where(odd, jnp.concatenate([z, wt], 2), jnp.concatenate([wt, z], 2)).astype(BF16)

    return dict(m=m.astype(BF16), win_r=pad_in(win_r), win_i=pad_in(win_i),
                wout_r=pad_out(wout_r), wout_i=pad_out(wout_i),
                a16_r=a16_r.reshape(2, 1, S5_LANES), a16_i=a16_i.reshape(2, 1, S5_LANES))


def _s5_mix(u, ops, h0_re, h0_im):
    b, seq, _ = u.shape
    nc = seq // S5_CHUNK
    ug = u.reshape(b, nc, S5_CHUNK, S5_GROUPS, S5_GROUP).transpose(0, 3, 1, 2, 4)
    ug = ug.reshape(b, S5_GROUPS, nc, S5_ROW).astype(BF16)
    s_re, s_im = _s5_in_call(ug, ops["win_r"], ops["win_i"])
    h_re, h_im, f_re, f_im = _s5_scan_call(s_re, s_im, ops["a16_r"], ops["a16_i"], h0_re, h0_im)
    y = _s5_out_call(ug, ops["m"], h_re, h_im, ops["wout_r"], ops["wout_i"])
    y = y.reshape(b, S5_GROUPS, nc, S5_CHUNK, S5_GROUP).transpose(0, 2, 3, 1, 4).reshape(b, seq, S5_WIDTH)
    return y, f_re, f_im


def _conv_kernel(rows, width, n_hor, has_ver, nt, *refs):
    if has_ver:
        (xh_ref, vp_ref, vc_ref, vn_ref, wh_ref, wv_ref, b_ref, g_ref, be_ref,
         o_ref, hs_scr, vs_scr) = refs
    else:
        xh_ref, wh_ref, b_ref, g_ref, be_ref, o_ref, hs_scr = refs
    i = pl.program_id(1)
    tokens = rows * width
    slot = width + 2 * CONV_PAD
    half = CONV_K // 2

    hs_scr[...] = jnp.zeros_like(hs_scr)
    for r in range(rows):
        hs_scr[r, pl.ds(CONV_PAD, width), :] = xh_ref[pl.ds(r * width, width), :]
    wh = wh_ref[...]
    acc_h = jnp.zeros((tokens, n_hor), F32)
    for tap in range(CONV_K):
        win = hs_scr[:, pl.ds(CONV_PAD - half + tap, width), :].reshape(tokens, n_hor)
        acc_h = acc_h + win * wh[tap:tap + 1, :]

    if has_ver:
        vs_scr[pl.ds(0, tokens), :] = jnp.where(i == 0, 0.0, vp_ref[...])
        vs_scr[pl.ds(tokens, tokens), :] = vc_ref[...]
        vs_scr[pl.ds(2 * tokens, tokens), :] = jnp.where(i == nt - 1, 0.0, vn_ref[...])
        wv = wv_ref[...]
        acc_v = jnp.zeros((tokens, CONV_WIDTH - n_hor), F32)
        for tap in range(CONV_K):
            acc_v = acc_v + vs_scr[pl.ds(tokens + (tap - half) * width, tokens), :] * wv[tap:tap + 1, :]
        x = jnp.concatenate([acc_h, acc_v], axis=1)
    else:
        x = acc_h
    x = x + b_ref[...]
    xc = x - jnp.mean(x, axis=-1, keepdims=True)
    y = xc * lax.rsqrt(jnp.mean(xc * xc, axis=-1, keepdims=True) + EPS) * g_ref[...] + be_ref[...]
    o_ref[...] = _silu(y)


def _conv_call(xc, dw_w, dw_b, ln_g, ln_b, grid_rows):
    b, seq, ch = xc.shape
    vec = pl.BlockSpec((1, ch), lambda b, i: (0, 0))
    if grid_rows > 0:
        width, rows, n_hor = GRID_W, 16, ch // 2
        tokens = rows * width
        nt = seq // tokens
        half_spec = lambda f: pl.BlockSpec((None, tokens, n_hor), f)
        in_specs = [half_spec(lambda b, i: (b, i, 0)),
                    half_spec(lambda b, i: (b, jnp.maximum(i - 1, 0), 1)),
                    half_spec(lambda b, i: (b, i, 1)),
                    half_spec(lambda b, i: (b, jnp.minimum(i + 1, nt - 1), 1)),
                    pl.BlockSpec((CONV_K, n_hor), lambda b, i: (0, 0)),
                    pl.BlockSpec((CONV_K, n_hor), lambda b, i: (0, 1)),
                    vec, vec, vec]
        args = [xc, xc, xc, xc, dw_w, dw_w, dw_b, ln_g, ln_b]
        scratch = [pltpu.VMEM((rows, width + 2 * CONV_PAD, n_hor), F32),
                   pltpu.VMEM((3 * tokens, ch - n_hor), F32)]
        has_ver = True
    else:
        width, rows, n_hor = seq, 1, ch
        tokens = seq
        nt = 1
        in_specs = [pl.BlockSpec((None, tokens, ch), lambda b, i: (b, 0, 0)),
                    pl.BlockSpec((CONV_K, ch), lambda b, i: (0, 0)),
                    vec, vec, vec]
        args = [xc, dw_w, dw_b, ln_g, ln_b]
        scratch = [pltpu.VMEM((rows, width + 2 * CONV_PAD, n_hor), F32)]
        has_ver = False
    return pl.pallas_call(
        functools.partial(_conv_kernel, rows, width, n_hor, has_ver, nt),
        grid=(b, nt),
        in_specs=in_specs,
        out_specs=pl.BlockSpec((None, tokens, ch), lambda b, i: (b, i, 0)),
        out_shape=jax.ShapeDtypeStruct((b, seq, ch), F32),
        scratch_shapes=scratch,
        compiler_params=_params("parallel", "parallel"),
        name="conformer_conv",
    )(*args)


def _lb_kernel(x_ref, o_ref):
    x = x_ref[...]
    n = x.shape[0]
    rows = [x[r:r + 1, :] for r in range(n)]
    mx = functools.reduce(jnp.maximum, rows)
    ex = [jnp.exp(r - mx) for r in rows]
    tot = functools.reduce(lambda p, q: p + q, ex)
    run = None
    for r in range(n):
        run = ex[r] / tot if run is None else run + ex[r] / tot
        o_ref[pl.ds(r, 1), :] = run


def _lb_call(logits):
    return pl.pallas_call(_lb_kernel, out_shape=jax.ShapeDtypeStruct(logits.shape, F32), name="hgrn_lb")(logits)


def _hgrn_kernel(rev, t_blk, nb, q_ref, f_ref, v_ref, lb_ref, s0_ref, o_ref, sfin_ref,
                 st_scr, kv_scr, sall_scr):
    i = pl.program_id(2)
    c = HGRN_CHUNK
    nch = t_blk // c

    @pl.when(i == 0)
    def _():
        st_scr[...] = s0_ref[...]

    lb = lb_ref[...]
    f = lb + (1.0 - lb) * jax.nn.sigmoid(f_ref[...])
    k = 1.0 - f
    log_f = jnp.log(f)
    row = lax.broadcasted_iota(jnp.int32, (t_blk, t_blk), 0)
    col = lax.broadcasted_iota(jnp.int32, (t_blk, t_blk), 1)
    same = (row // c) == (col // c)
    incl = same & ((row <= col) if rev else (row >= col))
    bc = _mm_exact_lhs(incl.astype(BF16), log_f)
    tot = _mm_exact_lhs(same.astype(BF16), log_f)
    q = q_ref[...]
    v = v_ref[...]
    q_in = q * jnp.exp(bc)
    k_in = k * jnp.exp(-bc)
    attn = jnp.where(incl, _mm_nt(q_in, k_in), 0.0)
    k_out = k * jnp.exp(tot - bc)
    o_intra = _mm(attn, v)
    cd = jnp.exp(tot)

    for n in range(nch):
        rows = slice(n * c, (n + 1) * c)
        kv_scr[n] = _mm_tn(v[rows], k_out[rows])
    st = st_scr[...]
    for n in (range(nch - 1, -1, -1) if rev else range(nch)):
        sall_scr[n] = st
        st = st * cd[n * c:n * c + 1, :] + kv_scr[n]
    st_scr[...] = st
    for n in range(nch):
        rows = slice(n * c, (n + 1) * c)
        o_ref[pl.ds(n * c, c), :] = _mm_nt(q_in[rows], sall_scr[n]) + o_intra[rows]

    @pl.when(i == nb - 1)
    def _():
        sfin_ref[...] = st_scr[...]


def _hgrn_call(q, fgate, v, lb_row, s0, rev):
    b, seq, _ = q.shape
    t_blk = min(256, seq)
    nb = seq // t_blk
    hh = HGRN_HEADS

    def blk(i):
        return (nb - 1 - i) if rev else i

    tok = pl.BlockSpec((None, t_blk, HGRN_DIM), lambda b, h, i: (b, blk(i), h))
    state_spec = pl.BlockSpec((None, None, HGRN_DIM, HGRN_DIM), lambda b, h, i: (b, h, 0, 0))
    return pl.pallas_call(
        functools.partial(_hgrn_kernel, rev, t_blk, nb),
        grid=(b, hh, nb),
        in_specs=[tok, tok, tok, pl.BlockSpec((1, HGRN_DIM), lambda b, h, i: (0, h)), state_spec],
        out_specs=[tok, state_spec],
        out_shape=[jax.ShapeDtypeStruct((b, seq, HGRN_WIDTH), F32),
                   jax.ShapeDtypeStruct((b, hh, HGRN_DIM, HGRN_DIM), F32)],
        scratch_shapes=[pltpu.VMEM((HGRN_DIM, HGRN_DIM), F32),
                        pltpu.VMEM((t_blk // HGRN_CHUNK, HGRN_DIM, HGRN_DIM), F32),
                        pltpu.VMEM((t_blk // HGRN_CHUNK, HGRN_DIM, HGRN_DIM), F32)],
        compiler_params=_params("parallel", "parallel", "arbitrary"),
        name="hgrn_bwd" if rev else "hgrn_fwd",
    )(q, fgate, v, lb_row, s0)


def _head_norm(o, gate, g):
    outs = []
    for hd in range(o.shape[1] // 128):
        x = o[:, hd * 128:(hd + 1) * 128]
        x = x * lax.rsqrt(jnp.mean(x * x, axis=-1, keepdims=True) + EPS) * g
        outs.append(x * _silu(gate[:, hd * 128:(hd + 1) * 128]))
    return jnp.concatenate(outs, axis=1)


def _out_ab_kernel(h_ref, gt_ref, of_ref, ob_ref, z_ref, y5_ref, u_ref, ng_ref, dsk_ref, glw_ref, glb_ref,
                   wout_ref, o_ref):
    a = _head_norm(of_ref[...] + ob_ref[...], z_ref[...], ng_ref[...])
    y = y5_ref[...] + dsk_ref[...] * u_ref[...]
    y = 0.5 * y * (1.0 + jnp.tanh(math.sqrt(2.0 / math.pi) * (y + 0.044715 * (y * y * y))))
    bmix = y * jax.nn.sigmoid(_mm(y, glw_ref[...]) + glb_ref[...])
    mix = _mm(jnp.concatenate([a, bmix], axis=1), wout_ref[...])
    o_ref[...] = h_ref[...] + gt_ref[...] * mix


def _out_cd_kernel(h_ref, gt_ref, c_ref, of_ref, ob_ref, g_ref, ng_ref, wout_ref, o_ref):
    dmix = _head_norm(of_ref[...] + ob_ref[...], g_ref[...], ng_ref[...])
    mix = _mm(jnp.concatenate([c_ref[...], dmix], axis=1), wout_ref[...])
    o_ref[...] = h_ref[...] + gt_ref[...] * mix


def _out_call(kern, name, h, mod, layer, ctx, toks, consts):
    b, seq, _ = h.shape
    tm = min(256, seq)
    in_specs = [pl.BlockSpec((None, tm, D_MODEL), lambda b, i: (b, i, 0)), _mod_spec(layer, 1, 2, ctx)]
    in_specs += [pl.BlockSpec((None, tm, t.shape[2]), lambda b, i: (b, i, 0)) for t in toks]
    in_specs += [pl.BlockSpec(cst.shape, lambda b, i: (0, 0)) for cst in consts]
    return pl.pallas_call(
        kern,
        grid=(b, seq // tm),
        in_specs=in_specs,
        out_specs=pl.BlockSpec((None, tm, D_MODEL), lambda b, i: (b, i, 0)),
        out_shape=jax.ShapeDtypeStruct(h.shape, F32),
        compiler_params=_params("parallel", "parallel"),
        name=name,
    )(h, mod, *toks, *consts)


def _mixer_ab(h, hc, mod, layer, need_ctx, prm):
    ng = prm["norm_g1"]
    w_in = prm["w_in"]
    outs = {}
    gdn_state = [jnp.zeros((h.shape[0], GDN_HEADS, GDN_DIM, GDN_DIM), F32)] * 2
    s5_state = [jnp.zeros((2, h.shape[0], 1, S5_LANES), F32)] * 2
    for ctx, x in ((True, hc), (False, h)):
        qkv, z, gates, u = _inproj_call(x, mod, layer, ctx, ng, w_in, glu=False)
        o_f, sf = _gdn_call(qkv, gates, prm["conv_w"], prm["alog"], prm["dtb"], gdn_state[0], rev=False)
        o_b, sb = _gdn_call(qkv, gates, prm["conv_w"], prm["alog"], prm["dtb"], gdn_state[1], rev=True)
        y5, f_re, f_im = _s5_mix(u, prm["s5"], s5_state[0], s5_state[1])
        gdn_state = [sf, sb]
        s5_state = [f_re, f_im]
        if ctx and not need_ctx:
            continue
        outs[ctx] = _out_call(_out_ab_kernel, "out_ab", x, mod, layer, ctx, [o_f, o_b, z, y5, u],
                              [prm["gdn_g"], prm["s5_d"], prm["glu_w"], prm["glu_b"], prm["w_out"]])
    return outs[False], outs.get(True)


def _mixer_cd(h, hc, mod, layer, need_ctx, prm, grid_rows):
    ng = prm["norm_g1"]
    outs = {}
    state = [jnp.zeros((h.shape[0], HGRN_HEADS, HGRN_DIM, HGRN_DIM), F32)] * 2
    for ctx, x in ((True, hc), (False, h)):
        xc, q, f_f, f_b, iv, g = _inproj_call(x, mod, layer, ctx, ng, prm["w_in"], glu=True)
        o_f, sf = _hgrn_call(q, f_f, iv, prm["lb"], state[0], rev=False)
        o_b, sb = _hgrn_call(q, f_b, iv, prm["lb"], state[1], rev=True)
        state = [sf, sb]
        if ctx and not need_ctx:
            continue
        cmix = _conv_call(xc, prm["dw_w"], prm["dw_b"], prm["ln_g"], prm["ln_b"], 0 if ctx else grid_rows)
        outs[ctx] = _out_call(_out_cd_kernel, "out_cd", x, mod, layer, ctx, [cmix, o_f, o_b, g],
                              [prm["hgrn_g"], prm["w_out"]])
    return outs[False], outs.get(True)


def kernel(x, c, ctx, c_ctx, ada_w, ada_b, norm_g, ffn_w_up, ffn_w_down, ab_w_in, ab_w_out, gdn_conv_w, gdn_a_log, gdn_dt_bias, gdn_norm_g, s5_lambda_re, s5_lambda_im, s5_log_step, s5_b_re, s5_b_im, s5_c_re, s5_c_im, s5_d, s5_glu_w, s5_glu_b, cd_w_in, cd_w_out, conv_dw_w, conv_dw_b, conv_ln_g, conv_ln_b, hgrn_lb_logits, hgrn_norm_g, final_norm_g):
    depth = ada_w.shape[0]
    batch = x.shape[0]
    grid_rows = x.shape[1] // GRID_W
    assert batch <= 2 and x.shape[1] % 1024 == 0 and ctx.shape[1] % 256 == 0

    cvec = jnp.zeros((8, D_MODEL), F32).at[:batch].set(c).at[2].set(c_ctx)
    mod = _ada_call(cvec, ada_w, ada_b)
    lb_all = _lb_call(hgrn_lb_logits)
    row = lambda v: v.reshape(1, -1)
    gw, kw = GDN_WIDTH, HGRN_WIDTH

    h, hc = x, ctx
    for l in range(depth):
        last = l == depth - 1
        w_up = ffn_w_up[l].astype(BF16)
        w_down = ffn_w_down[l].astype(BF16)
        h = _ffn_call(h, mod, l, 0, False, row(norm_g[l, 0]), w_up[0], w_down[0])
        hc = _ffn_call(hc, mod, l, 0, True, row(norm_g[l, 0]), w_up[0], w_down[0])
        if l % 2 == 0:
            e = l // 2
            w = ab_w_in[e].astype(BF16)
            w_gate = jnp.zeros((D_MODEL, 128), BF16).at[:, :4 * GDN_HEADS].set(w[:, 4 * gw:4 * gw + 4 * GDN_HEADS])
            pad8 = lambda v: jnp.zeros((1, 128), F32).at[0, :2 * GDN_HEADS].set(v.reshape(-1))
            prm = dict(
                norm_g1=row(norm_g[l, 1]),
                w_in=[w[:, :3 * gw], w[:, 3 * gw:4 * gw], w_gate, w[:, 4 * gw + 4 * GDN_HEADS:]],
                conv_w=gdn_conv_w[e], alog=pad8(gdn_a_log[e]), dtb=pad8(gdn_dt_bias[e]),
                gdn_g=row(gdn_norm_g[e]),
                s5=_s5_prepare_ops(s5_lambda_re[e], s5_lambda_im[e], s5_log_step[e], s5_b_re[e], s5_b_im[e],
                                   s5_c_re[e], s5_c_im[e]),
                s5_d=row(s5_d[e]), glu_w=s5_glu_w[e].astype(BF16), glu_b=row(s5_glu_b[e]),
                w_out=ab_w_out[e].astype(BF16))
            h, hc_new = _mixer_ab(h, hc, mod, l, not last, prm)
        else:
            o = l // 2
            w = cd_w_in[o].astype(BF16)
            cw = CONV_WIDTH
            prm = dict(
                norm_g1=row(norm_g[l, 1]),
                w_in=[w[:, :cw], w[:, cw:2 * cw]] + [w[:, 2 * cw + k * kw:2 * cw + (k + 1) * kw] for k in range(5)],
                lb=lb_all[o:o + 1], hgrn_g=row(hgrn_norm_g[o]),
                dw_w=conv_dw_w[o], dw_b=row(conv_dw_b[o]), ln_g=row(conv_ln_g[o]), ln_b=row(conv_ln_b[o]),
                w_out=cd_w_out[o].astype(BF16))
            h, hc_new = _mixer_cd(h, hc, mod, l, not last, prm, grid_rows)
        h = _ffn_call(h, mod, l, 2, False, row(norm_g[l, 2]), w_up[1], w_down[1],
                      final_g=row(final_norm_g) if last else None)
        if not last:
            hc = _ffn_call(hc_new, mod, l, 2, True, row(norm_g[l, 2]), w_up[1], w_down[1])
    return h
```

```python
import functools
import math

import jax
import jax.numpy as jnp
from jax import lax
from jax.experimental import pallas as pl
from jax.experimental.pallas import tpu as pltpu

F32 = jnp.float32
BF16 = jnp.bfloat16
HIGHEST = lax.Precision.HIGHEST
EPS = 1e-6

D_MODEL = 1024
GRID_W = 64
FFN_DIM = 2816

GDN_HEADS = 4
GDN_DIM = 128
GDN_WIDTH = GDN_HEADS * GDN_DIM
GDN_CONV = 5
GDN_CHUNK = 256

S5_WIDTH = 512
S5_GROUP = 16
S5_GROUPS = 32
S5_STATE = 64
S5_CHUNK = 16
S5_ROW = S5_CHUNK * S5_GROUP
S5_LANES = S5_GROUPS * S5_STATE

CONV_WIDTH = 512
CONV_K = 31
CONV_PAD = 16

HGRN_HEADS = 4
HGRN_DIM = 128
HGRN_WIDTH = HGRN_HEADS * HGRN_DIM
HGRN_CHUNK = 16

V7X_VMEM_LIMIT = 48 * 1024 * 1024


def _params(*sem):
    return pltpu.CompilerParams(dimension_semantics=sem, vmem_limit_bytes=V7X_VMEM_LIMIT)


def _silu(x):
    return x * jax.nn.sigmoid(x)


def _mm(a, b):
    return jnp.dot(a.astype(BF16), b.astype(BF16), preferred_element_type=F32)


def _mm_nt(a, b):
    return lax.dot_general(a.astype(BF16), b.astype(BF16), (((1,), (1,)), ((), ())),
                           preferred_element_type=F32)


def _mm_tn(a, b):
    return lax.dot_general(a.astype(BF16), b.astype(BF16), (((0,), (0,)), ((), ())),
                           preferred_element_type=F32)


def _split3(x):
    hi = x.astype(BF16)
    r1 = x - hi.astype(F32)
    mid = r1.astype(BF16)
    lo = (r1 - mid.astype(F32)).astype(BF16)
    return hi, mid, lo


def _mm_exact_lhs(a_bf16, x):
    return jnp.dot(jnp.concatenate([a_bf16] * 3, axis=1), jnp.concatenate(_split3(x), axis=0),
                   preferred_element_type=F32)


def _mm_exact_rhs(x, b_bf16):
    return jnp.dot(jnp.concatenate(_split3(x), axis=1), jnp.concatenate([b_bf16] * 3, axis=0),
                   preferred_element_type=F32)


def _mm_x3(a, b):
    ah = a.astype(BF16)
    al = (a - ah.astype(F32)).astype(BF16)
    bh = b.astype(BF16)
    bl = (b - bh.astype(F32)).astype(BF16)
    return jnp.dot(jnp.concatenate([ah, ah, al], axis=1), jnp.concatenate([bh, bl, bh], axis=0),
                   preferred_element_type=F32)


def _same_block(i, j, size):
    return (i ^ j) < size


def _ada_norm(x, g, scale, shift):
    y = x * lax.rsqrt(jnp.mean(x * x, axis=-1, keepdims=True) + EPS) * g
    return y * (1.0 + scale) + shift


def _ada_kernel(c_ref, w_ref, b_ref, o_ref):
    o_ref[...] = _mm(_silu(c_ref[...]), w_ref[...]) + b_ref[...]


def _ada_call(cvec, ada_w, ada_b):
    depth = ada_w.shape[0]
    ncol = ada_w.shape[2] // D_MODEL
    out = pl.pallas_call(
        _ada_kernel,
        grid=(depth, ncol),
        in_specs=[pl.BlockSpec((8, D_MODEL), lambda l, j: (0, 0)),
                  pl.BlockSpec((None, D_MODEL, D_MODEL), lambda l, j: (l, 0, j)),
                  pl.BlockSpec((None, 1, D_MODEL), lambda l, j: (l, 0, j))],
        out_specs=pl.BlockSpec((None, None, 8, D_MODEL), lambda l, j: (l, j, 0, 0)),
        out_shape=jax.ShapeDtypeStruct((depth, ncol, 8, D_MODEL), F32),
        compiler_params=_params("arbitrary", "arbitrary"),
        name="ada_mod",
    )(cvec, ada_w, ada_b.reshape(depth, 1, -1))
    return out.reshape(depth * ncol * 8, 1, D_MODEL)


def _mod_spec(layer, sub, kind, ctx):
    base = (layer * 9 + sub * 3 + kind) * 8
    if ctx:
        return pl.BlockSpec((None, 1, D_MODEL), lambda b, *_: (base + 2, 0, 0))
    return pl.BlockSpec((None, 1, D_MODEL), lambda b, *_: (base + b, 0, 0))


def _ffn_kernel(tf, final, h_ref, sh_ref, sc_ref, gt_ref, g_ref, wup_ref, wd_ref, *rest):
    if final:
        fg_ref, o_ref = rest
    else:
        (o_ref,) = rest
    x = h_ref[...]
    xn = _ada_norm(x, g_ref[...], sc_ref[...], sh_ref[...]).astype(BF16)
    acc = None
    for j in range(FFN_DIM // tf):
        gate = jnp.dot(xn, wup_ref[:, j * tf:(j + 1) * tf], preferred_element_type=F32)
        up = jnp.dot(xn, wup_ref[:, FFN_DIM + j * tf:FFN_DIM + (j + 1) * tf], preferred_element_type=F32)
        act = (_silu(gate) * up).astype(BF16)
        part = jnp.dot(act, wd_ref[j * tf:(j + 1) * tf, :], preferred_element_type=F32)
        acc = part if acc is None else acc + part
    y = x + 0.5 * gt_ref[...] * acc
    if final:
        y = y * lax.rsqrt(jnp.mean(y * y, axis=-1, keepdims=True) + EPS) * fg_ref[...]
    o_ref[...] = y


def _ffn_call(h, mod, layer, sub, ctx, norm_g_row, w_up, w_down, which, final_g=None):
    b, seq, _ = h.shape
    tm = min(512, seq)
    tf = 256
    final = final_g is not None
    resident = dict(pipeline_mode=pl.Buffered(1))
    in_specs = [
        pl.BlockSpec((None, tm, D_MODEL), lambda b, i: (b, i, 0)),
        _mod_spec(layer, sub, 0, ctx), _mod_spec(layer, sub, 1, ctx), _mod_spec(layer, sub, 2, ctx),
        pl.BlockSpec((1, D_MODEL), lambda b, i: (0, 0)),
        pl.BlockSpec((None, None, D_MODEL, 2 * FFN_DIM), lambda b, i: which + (0, 0), **resident),
        pl.BlockSpec((None, None, FFN_DIM, D_MODEL), lambda b, i: which + (0, 0), **resident),
    ]
    args = [h, mod, mod, mod, norm_g_row, w_up, w_down]
    if final:
        in_specs.append(pl.BlockSpec((1, D_MODEL), lambda b, i: (0, 0)))
        args.append(final_g)
    return pl.pallas_call(
        functools.partial(_ffn_kernel, tf, final),
        grid=(b, seq // tm),
        in_specs=in_specs,
        out_specs=pl.BlockSpec((None, tm, D_MODEL), lambda b, i: (b, i, 0)),
        out_shape=jax.ShapeDtypeStruct(h.shape, F32),
        compiler_params=_params("parallel", "parallel"),
        name="ffn",
    )(*args)


def _inproj_kernel(nw, glu, h_ref, sh_ref, sc_ref, g_ref, *refs):
    w_refs, o_refs = refs[:nw], refs[nw:]
    xn = _ada_norm(h_ref[...], g_ref[...], sc_ref[...], sh_ref[...]).astype(BF16)
    outs = [jnp.dot(xn, w[...], preferred_element_type=F32) for w in w_refs]
    if glu:
        outs = [outs[0] * jax.nn.sigmoid(outs[1])] + outs[2:]
    for o_ref, val in zip(o_refs, outs):
        o_ref[...] = val


def _inproj_call(h, mod, layer, ctx, norm_g_row, w, pieces, glu):
    b, seq, _ = h.shape
    tm = min(512, seq)
    widths = [wd for _, wd in pieces]
    out_widths = widths[1:] if glu else widths
    in_specs = [pl.BlockSpec((None, tm, D_MODEL), lambda b, i: (b, i, 0)),
                _mod_spec(layer, 1, 0, ctx), _mod_spec(layer, 1, 1, ctx),
                pl.BlockSpec((1, D_MODEL), lambda b, i: (0, 0))]
    for off, wd in pieces:
        assert off % wd == 0
        in_specs.append(pl.BlockSpec((D_MODEL, wd), functools.partial(lambda b, i, blk: (0, blk), blk=off // wd),
                                     pipeline_mode=pl.Buffered(1)))
    return pl.pallas_call(
        functools.partial(_inproj_kernel, len(pieces), glu),
        grid=(b, seq // tm),
        in_specs=in_specs,
        out_specs=[pl.BlockSpec((None, tm, wd), lambda b, i: (b, i, 0)) for wd in out_widths],
        out_shape=[jax.ShapeDtypeStruct((b, seq, wd), F32) for wd in out_widths],
        compiler_params=_params("parallel", "parallel"),
        name="inproj",
    )(h, mod, mod, norm_g_row, *([w] * len(pieces)))


def _inproj_ab_kernel(tm, nt, h_ref, hp_ref, hn_ref, sh_ref, sc_ref, g_ref, wqkv_ref, wz_ref, wu_ref, wg_ref, cw_ref,
                      q_ref, k_ref, v_ref, z_ref, u_ref, gates_ref, pad_scr):
    i = pl.program_id(1)
    norm = lambda x: _ada_norm(x, g_ref[...], sc_ref[...], sh_ref[...]).astype(BF16)
    dot = functools.partial(jnp.dot, preferred_element_type=F32)
    xn = norm(h_ref[...])
    z_ref[...] = dot(xn, wz_ref[...])
    u_ref[...] = dot(xn, wu_ref[...])
    gates_ref[...] = dot(xn, wg_ref[...])
    halo = dot(norm(jnp.concatenate([hp_ref[...], hn_ref[...]], axis=0)), wqkv_ref[...])
    pad_scr[pl.ds(0, 8), :] = jnp.where(i == 0, 0.0, halo[:8])
    pad_scr[pl.ds(8, tm), :] = dot(xn, wqkv_ref[...])
    pad_scr[pl.ds(8 + tm, 8), :] = jnp.where(i == nt - 1, 0.0, halo[8:])
    w = cw_ref[...]
    acc = None
    for tap in range(GDN_CONV):
        term = pad_scr[pl.ds(8 - GDN_CONV // 2 + tap, tm), :] * w[tap:tap + 1, :]
        acc = term if acc is None else acc + term
    y = _silu(acc)

    def l2n_heads(x, scale):
        outs = []
        for hd in range(GDN_HEADS):
            xh = x[:, hd * GDN_DIM:(hd + 1) * GDN_DIM]
            outs.append(xh * (lax.rsqrt(jnp.sum(xh * xh, axis=-1, keepdims=True) + EPS) * scale))
        return jnp.concatenate(outs, axis=1)

    q_ref[...] = l2n_heads(y[:, :GDN_WIDTH], GDN_DIM ** -0.5)
    k_ref[...] = l2n_heads(y[:, GDN_WIDTH:2 * GDN_WIDTH], 1.0)
    v_ref[...] = y[:, 2 * GDN_WIDTH:]


def _inproj_ab_call(h, mod, layer, ctx, norm_g_row, w, conv_w):
    b, seq, _ = h.shape
    tm = min(512, seq)
    nt = seq // tm
    r8 = tm // 8
    gw = GDN_WIDTH
    resident = dict(pipeline_mode=pl.Buffered(1))
    wspec = lambda wd, blk: pl.BlockSpec((D_MODEL, wd), lambda b, i: (0, blk), **resident)
    tok = lambda wd: pl.BlockSpec((None, tm, wd), lambda b, i: (b, i, 0))
    in_specs = [tok(D_MODEL),
                pl.BlockSpec((None, 8, D_MODEL), lambda b, i: (b, jnp.maximum(i * r8 - 1, 0), 0)),
                pl.BlockSpec((None, 8, D_MODEL), lambda b, i: (b, jnp.minimum((i + 1) * r8, seq // 8 - 1), 0)),
                _mod_spec(layer, 1, 0, ctx), _mod_spec(layer, 1, 1, ctx),
                pl.BlockSpec((1, D_MODEL), lambda b, i: (0, 0)),
                wspec(3 * gw, 0), wspec(gw, 3), wspec(S5_WIDTH, 4), wspec(128, (4 * gw + S5_WIDTH) // 128),
                pl.BlockSpec((GDN_CONV, 3 * gw), lambda b, i: (0, 0))]
    widths = [gw, gw, gw, gw, S5_WIDTH, 128]
    return pl.pallas_call(
        functools.partial(_inproj_ab_kernel, tm, nt),
        grid=(b, nt),
        in_specs=in_specs,
        out_specs=[tok(wd) for wd in widths],
        out_shape=[jax.ShapeDtypeStruct((b, seq, wd), F32) for wd in widths],
        scratch_shapes=[pltpu.VMEM((tm + 16, 3 * gw), F32)],
        compiler_params=_params("parallel", "parallel"),
        name="inproj_ab",
    )(h, h, h, mod, mod, norm_g_row, w, w, w, w, conv_w)


def _gdn_kernel(rev, t_blk, nb, q_scr, k_scr, v_scr, gates_ref, alog_ref, dtb_ref, s0_ref,
                o_ref, sfin_ref,
                s_scr, sel_scr, ut_scr, w_scr, qd_scr, kd_scr, qk_scr, cd_scr, gt_scr):
    i = pl.program_id(1)
    c = GDN_CHUNK
    nch = t_blk // c
    hh = GDN_HEADS
    width = hh * GDN_DIM

    @pl.when(i == 0)
    def _():
        s_scr[...] = s0_ref[...]

    gates = gates_ref[...]
    lane = lax.broadcasted_iota(jnp.int32, gates.shape, 1)
    log_a = -jnp.exp(alog_ref[...]) * jax.nn.softplus(gates + dtb_ref[...])
    trow = lax.broadcasted_iota(jnp.int32, (t_blk, t_blk), 0)
    tcol = lax.broadcasted_iota(jnp.int32, (t_blk, t_blk), 1)
    cum = _same_block(trow, tcol, c) & ((trow <= tcol) if rev else (trow >= tcol))
    g_all = _mm_exact_lhs(cum.astype(BF16), log_a)
    for m in range(nch):
        gt_scr[m] = g_all[m * c:(m + 1) * c, :].T
    mixed = jnp.where(lane < 2 * hh, g_all, jax.nn.sigmoid(gates))
    d_off = hh if rev else 0
    src = lax.broadcasted_iota(jnp.int32, (128, 2 * width), 0)
    dst = lax.broadcasted_iota(jnp.int32, (128, 2 * width), 1)
    want = d_off + (dst >> 8) + jnp.where((dst & 128) != 0, 2 * hh, 0)
    sel_scr[...] = _mm_exact_rhs(mixed, (src == want).astype(BF16))

    row = lax.broadcasted_iota(jnp.int32, (c, c), 0)
    col = lax.broadcasted_iota(jnp.int32, (c, c), 1)
    incl = (row <= col) if rev else (row >= col)
    strict = (row < col) if rev else (row > col)
    last = 0 if rev else c - 1
    diag16 = _same_block(row, col, 16)
    levels = []
    d = 16
    while d < c:
        levels.append(_same_block(row, col, 2 * d) & jnp.logical_not(_same_block(row, col, d)))
        d *= 2

    def lanes_c(x):
        return x[:, :c] if c <= 128 else jnp.concatenate([x] * (c // 128), axis=1)

    def inverse_minus_identity(a_mat):
        p = jnp.where(diag16, -a_mat, 0.0)
        n = p
        p = _mm(p, p)
        for it in range(3):
            if it < 2:
                both = _mm(jnp.concatenate([n, p], axis=0), p)
                n, p = n + p + both[:c], both[c:]
            else:
                n = n + p + _mm(n, p)
        for mask in levels:
            off = jnp.where(mask, a_mat, 0.0)
            x = off + _mm(n, off)
            n = n - (x + _mm(x, n))
        return n

    def prep(m, carry):
        rows = pl.ds(pl.multiple_of(m * c, c), c)
        for hd in range(hh):
            lanes = pl.ds(hd * GDN_DIM, GDN_DIM)
            q = q_scr[rows, lanes]
            k = k_scr[rows, lanes]
            v = v_scr[rows, lanes]
            g = sel_scr[rows, pl.ds(hd * 256, 128)]
            beta = sel_scr[rows, pl.ds(hd * 256 + 128, 128)]
            g_row = gt_scr[m, pl.ds(d_off + hd, 1), :]
            diff = lanes_c(g) - g_row
            decay = jnp.where(incl, jnp.exp(jnp.where(incl, diff, 0.0)), 0.0)
            kq = _mm_nt(jnp.concatenate([q, k], axis=0), k)
            a_mat = jnp.where(strict, kq[c:] * decay, 0.0) * lanes_c(beta)
            n_mat = inverse_minus_identity(a_mat)
            eg = jnp.exp(g)
            rhs = jnp.concatenate([v * beta, k * (beta * eg)], axis=1)
            sol = rhs + _mm_x3(n_mat, rhs)
            g_last = g[last:last + 1, :]
            ut_scr[rows, lanes] = sol[:, :GDN_DIM]
            w_scr[rows, lanes] = sol[:, GDN_DIM:]
            qd_scr[rows, lanes] = q * eg
            kd_scr[rows, lanes] = k * jnp.exp(g_last - g)
            qk_scr[rows, pl.ds(hd * c, c)] = jnp.where(incl, kq[:c] * decay, 0.0)
            cd_scr[m, hd] = jnp.broadcast_to(jnp.exp(g_last), (8, GDN_DIM))
        return carry

    lax.fori_loop(0, nch, prep, 0, unroll=True)

    def step(ci, carry):
        cidx = (nch - 1 - ci) if rev else ci
        rows = pl.ds(pl.multiple_of(cidx * c, c), c)
        for hd in range(hh):
            lanes = pl.ds(hd * GDN_DIM, GDN_DIM)
            s = s_scr[hd]
            ws = _mm(jnp.concatenate([w_scr[rows, lanes], qd_scr[rows, lanes]], axis=0), s)
            u = ut_scr[rows, lanes] - ws[:c]
            o_ref[rows, lanes] = ws[c:] + _mm(qk_scr[rows, pl.ds(hd * c, c)], u)
            s_scr[hd] = s * cd_scr[cidx, hd][0:1, :] + _mm_tn(kd_scr[rows, lanes], u)
        return carry

    lax.fori_loop(0, nch, step, 0)

    @pl.when(i == nb - 1)
    def _():
        sfin_ref[...] = s_scr[...]


def _gdn_call(q, k, v, gates, alog_row, dtb_row, s0, rev):
    b, seq, _ = q.shape
    t_blk = min(512, seq)
    nb = seq // t_blk
    hh = GDN_HEADS
    width = GDN_WIDTH

    def blk(i):
        return (nb - 1 - i) if rev else i

    tile = pl.BlockSpec((None, t_blk, width), lambda b, i: (b, blk(i), 0))
    state_spec = pl.BlockSpec((None, hh, GDN_DIM, GDN_DIM), lambda b, i: (b, 0, 0, 0))
    in_specs = [tile, tile, tile,
                pl.BlockSpec((None, t_blk, 128), lambda b, i: (b, blk(i), 0)),
                pl.BlockSpec((1, 128), lambda b, i: (0, 0)),
                pl.BlockSpec((1, 128), lambda b, i: (0, 0)),
                state_spec]
    args = [q, k, v, gates, alog_row, dtb_row, s0]
    tok = lambda w: pltpu.VMEM((t_blk, w), F32)
    return pl.pallas_call(
        functools.partial(_gdn_kernel, rev, t_blk, nb),
        grid=(b, nb),
        in_specs=in_specs,
        out_specs=[tile, state_spec],
        out_shape=[jax.ShapeDtypeStruct((b, seq, width), F32),
                   jax.ShapeDtypeStruct((b, hh, GDN_DIM, GDN_DIM), F32)],
        scratch_shapes=[pltpu.VMEM((hh, GDN_DIM, GDN_DIM), F32),
                        tok(2 * width),
                        tok(width), tok(width), tok(width), tok(width), tok(hh * GDN_CHUNK),
                        pltpu.VMEM((t_blk // GDN_CHUNK, hh, 8, GDN_DIM), F32),
                        pltpu.VMEM((t_blk // GDN_CHUNK, 128, GDN_CHUNK), F32)],
        compiler_params=_params("parallel", "arbitrary"),
        name="gdn_bwd" if rev else "gdn_fwd",
    )(*args)


def _s5_ops_kernel(lr_ref, li_ref, ls_ref, btr_ref, bti_ref, cr_ref, ci_ref,
                   m_ref, winr_ref, wini_ref, woutr_ref, wouti_ref, a16r_ref, a16i_ref):
    d = pl.program_id(0)
    lr = lr_ref[...]
    li = li_ref[...]
    dt = jnp.exp(ls_ref[...])

    def apow(kk):
        mag = jnp.exp(lr * dt * kk)
        ang = li * dt * kk
        return mag * jnp.cos(ang), mag * jnp.sin(ang)

    def cmul(xr, xi, yr, yi):
        return xr * yr - xi * yi, xr * yi + xi * yr

    ar, ai = apow(1.0)
    den = lr * lr + li * li
    nr, ni = ar - 1.0, ai
    zr = (nr * lr + ni * li) / den
    zi = (ni * lr - nr * li) / den
    bbr, bbi = cmul(zr, zi, btr_ref[...], bti_ref[...])
    cr, ci = cr_ref[...], ci_ref[...]

    t16 = lax.broadcasted_iota(jnp.int32, (S5_CHUNK, 1), 0)
    tv16 = jnp.where(d == 0, t16, S5_CHUNK - 1 - t16).astype(F32)
    t_row = lax.broadcasted_iota(jnp.int32, (S5_ROW, 1), 0) // S5_GROUP
    t_col = lax.broadcasted_iota(jnp.int32, (1, S5_ROW), 1) // S5_GROUP
    tv_row = jnp.where(d == 0, t_row, S5_CHUNK - 1 - t_row)
    tv_col = jnp.where(d == 0, t_col, S5_CHUNK - 1 - t_col)
    tile = lambda x: jnp.concatenate([x] * S5_CHUNK, axis=0)
    spread = (lax.broadcasted_iota(jnp.int32, (S5_ROW, S5_CHUNK), 0) // S5_GROUP
              == lax.broadcasted_iota(jnp.int32, (S5_ROW, S5_CHUNK), 1)).astype(BF16)
    rep = lambda x: _mm_exact_lhs(spread, x)
    pos_r, pos_i = apow(tv16)
    neg_r, neg_i = apow(-tv16)
    xr, xi = cmul(tile(bbr), tile(bbi), rep(neg_r), rep(neg_i))
    yr, yi = cmul(tile(cr), tile(ci), rep(pos_r), rep(pos_i))
    nt = lambda p, q: lax.dot_general(p, q, (((1,), (1,)), ((), ())), precision=HIGHEST,
                                      preferred_element_type=F32)
    m = nt(xr, yr) - nt(xi, yi)
    m_ref[...] = jnp.where(tv_col >= tv_row, m, 0.0)

    p15r, p15i = apow(S5_CHUNK - 1.0)
    winr_ref[...], wini_ref[...] = cmul(xr, xi, p15r, p15i)
    wr, wi = cmul(yr, yi, ar, ai)
    woutr_ref[...] = wr
    wouti_ref[...] = -wi
    a16r, a16i = apow(float(S5_CHUNK))
    a16r_ref[...] = a16r
    a16i_ref[...] = a16i


def _s5_ops_call(lam_re, lam_im, log_step, bt_re, bt_im, c_re, c_im):
    g, p, cg = S5_GROUPS, S5_STATE, S5_GROUP
    ls = jnp.broadcast_to(log_step[:, :, None, None], (2, g, 1, p))
    lam_spec = pl.BlockSpec((None, None, 1, p), lambda d, gi: (d, gi, 0, 0))
    par_spec = pl.BlockSpec((None, cg, p), lambda d, gi: (gi, 0, 0))
    out = lambda *shape: pl.BlockSpec((None, None) + shape, lambda d, gi: (d, gi, 0, 0))
    return pl.pallas_call(
        _s5_ops_kernel,
        grid=(2, g),
        in_specs=[lam_spec, lam_spec, lam_spec, par_spec, par_spec, par_spec, par_spec],
        out_specs=[out(S5_ROW, S5_ROW), out(S5_ROW, p), out(S5_ROW, p), out(S5_ROW, p), out(S5_ROW, p),
                   out(1, p), out(1, p)],
        out_shape=[jax.ShapeDtypeStruct((2, g, S5_ROW, S5_ROW), F32)]
        + [jax.ShapeDtypeStruct((2, g, S5_ROW, p), F32)] * 4
        + [jax.ShapeDtypeStruct((2, g, 1, p), F32)] * 2,
        compiler_params=_params("parallel", "parallel"),
        name="s5_ops",
    )(lam_re.reshape(2, g, 1, p), lam_im.reshape(2, g, 1, p), ls, bt_re, bt_im, c_re, c_im)


def _s5_in_kernel(u_ref, wr_ref, wi_ref, sr_ref, si_ref):
    dot = functools.partial(jnp.dot, preferred_element_type=F32)
    u0, u1 = u_ref[0], u_ref[1]
    sr_ref[...] = dot(u0, wr_ref[0]) + dot(u1, wr_ref[1])
    si_ref[...] = dot(u0, wi_ref[0]) + dot(u1, wi_ref[1])


def _s5_in_call(ug, win_r, win_i):
    b, g, nc, _ = ug.shape
    w_spec = pl.BlockSpec((None, 2, S5_ROW, 128), lambda d, b, gp: (d, gp, 0, 0))
    o_spec = pl.BlockSpec((None, None, nc, 128), lambda d, b, gp: (d, b, 0, gp))
    return pl.pallas_call(
        _s5_in_kernel,
        grid=(2, b, g // 2),
        in_specs=[pl.BlockSpec((None, 2, nc, S5_ROW), lambda d, b, gp: (b, gp, 0, 0)), w_spec, w_spec],
        out_specs=[o_spec, o_spec],
        out_shape=[jax.ShapeDtypeStruct((2, b, nc, S5_LANES), F32)] * 2,
        compiler_params=_params("parallel", "parallel", "parallel"),
        name="s5_in",
    )(ug, win_r, win_i)


def _s5_scan_kernel(nc, sr_ref, si_ref, ar_ref, ai_ref, h0r_ref, h0i_ref, hr_ref, hi_ref, fr_ref, fi_ref):
    coef = [(ar_ref[d], ai_ref[d]) for d in range(2)]

    def body(i, carry):
        out = []
        for d in range(2):
            hr, hi = carry[d]
            ar, ai = coef[d]
            n = i if d == 0 else nc - 1 - i
            hr_ref[d, pl.ds(n, 1), :] = hr
            hi_ref[d, pl.ds(n, 1), :] = hi
            sr = sr_ref[d, pl.ds(n, 1), :]
            si = si_ref[d, pl.ds(n, 1), :]
            out.append((ar * hr - ai * hi + sr, ar * hi + ai * hr + si))
        return tuple(out)

    fin = lax.fori_loop(0, nc, body, tuple((h0r_ref[d], h0i_ref[d]) for d in range(2)), unroll=4)
    for d in range(2):
        fr_ref[d] = fin[d][0]
        fi_ref[d] = fin[d][1]


def _s5_scan_call(s_re, s_im, a16_re, a16_im, h0_re, h0_im):
    _, b, nc, _ = s_re.shape
    tl = 512
    big = pl.BlockSpec((2, None, nc, tl), lambda b, j: (0, b, 0, j))
    a_spec = pl.BlockSpec((2, 1, tl), lambda b, j: (0, 0, j))
    st_spec = pl.BlockSpec((2, None, 1, tl), lambda b, j: (0, b, 0, j))
    return pl.pallas_call(
        functools.partial(_s5_scan_kernel, nc),
        grid=(b, S5_LANES // tl),
        in_specs=[big, big, a_spec, a_spec, st_spec, st_spec],
        out_specs=[big, big, st_spec, st_spec],
        out_shape=[jax.ShapeDtypeStruct(s_re.shape, F32)] * 2
        + [jax.ShapeDtypeStruct((2, b, 1, S5_LANES), F32)] * 2,
        compiler_params=_params("parallel", "parallel"),
        name="s5_scan",
    )(s_re, s_im, a16_re, a16_im, h0_re, h0_im)


def _s5_out_kernel(u_ref, m_ref, hr_ref, hi_ref, wr_ref, wi_ref, y_ref):
    u = u_ref[...]
    acc = None
    for d in range(2):
        y = (jnp.dot(u, m_ref[d], preferred_element_type=F32)
             + _mm(hr_ref[d], wr_ref[d]) + _mm(hi_ref[d], wi_ref[d]))
        acc = y if acc is None else acc + y
    y_ref[...] = acc


def _s5_out_call(ug, m, h_re, h_im, wout_r, wout_i):
    b, g, nc, _ = ug.shape
    h_spec = pl.BlockSpec((2, None, nc, 128), lambda b, gi: (0, b, 0, gi // 2))
    w_spec = pl.BlockSpec((2, None, 128, S5_ROW), lambda b, gi: (0, gi, 0, 0))
    return pl.pallas_call(
        _s5_out_kernel,
        grid=(b, g),
        in_specs=[pl.BlockSpec((None, None, nc, S5_ROW), lambda b, gi: (b, gi, 0, 0)),
                  pl.BlockSpec((2, None, S5_ROW, S5_ROW), lambda b, gi: (0, gi, 0, 0)),
                  h_spec, h_spec, w_spec, w_spec],
        out_specs=pl.BlockSpec((None, None, nc, S5_ROW), lambda b, gi: (b, gi, 0, 0)),
        out_shape=jax.ShapeDtypeStruct((b, g, nc, S5_ROW), F32),
        compiler_params=_params("parallel", "parallel"),
        name="s5_out",
    )(ug, m, h_re, h_im, wout_r, wout_i)


def _s5_prepare_ops(lam_re, lam_im, log_step, b_re, b_im, c_re, c_im):
    m, win_r, win_i, wout_r, wout_i, a16_r, a16_i = _s5_ops_call(
        lam_re, lam_im, log_step, jnp.swapaxes(b_re, 1, 2), jnp.swapaxes(b_im, 1, 2), c_re, c_im)
    p = S5_STATE
    odd = (jnp.arange(S5_GROUPS) % 2 == 1)[None, :, None, None]

    def pad_in(w):
        z = jnp.zeros_like(w)
        return jnp.where(odd, jnp.concatenate([z, w], -1), jnp.concatenate([w, z], -1)).astype(BF16)

    def pad_out(w):
        wt = jnp.swapaxes(w, 2, 3)
        z = jnp.zeros_like(wt)
        return jnp.where(odd, jnp.concatenate([z, wt], 2), jnp.concatenate([wt, z], 2)).astype(BF16)

    return dict(m=m.astype(BF16), win_r=pad_in(win_r), win_i=pad_in(win_i),
                wout_r=pad_out(wout_r), wout_i=pad_out(wout_i),
                a16_r=a16_r.reshape(2, 1, S5_LANES), a16_i=a16_i.reshape(2, 1, S5_LANES))


def _s5_mix(u, ops, h0_re, h0_im):
    b, seq, _ = u.shape
    nc = seq // S5_CHUNK
    ug = u.reshape(b, nc, S5_CHUNK, S5_GROUPS, S5_GROUP).transpose(0, 3, 1, 2, 4)
    ug = ug.reshape(b, S5_GROUPS, nc, S5_ROW).astype(BF16)
    s_re, s_im = _s5_in_call(ug, ops["win_r"], ops["win_i"])
    h_re, h_im, f_re, f_im = _s5_scan_call(s_re, s_im, ops["a16_r"], ops["a16_i"], h0_re, h0_im)
    y = _s5_out_call(ug, ops["m"], h_re, h_im, ops["wout_r"], ops["wout_i"])
    y = y.reshape(b, S5_GROUPS, nc, S5_CHUNK, S5_GROUP).transpose(0, 2, 3, 1, 4).reshape(b, seq, S5_WIDTH)
    return y, f_re, f_im


def _conv_kernel(rows, width, n_hor, has_ver, nt, *refs):
    if has_ver:
        (xh_ref, vp_ref, vc_ref, vn_ref, wh_ref, wv_ref, b_ref, g_ref, be_ref,
         o_ref, hs_scr, vs_scr) = refs
    else:
        xh_ref, wh_ref, b_ref, g_ref, be_ref, o_ref, hs_scr = refs
    i = pl.program_id(1)
    tokens = rows * width
    slot = width + 2 * CONV_PAD
    half = CONV_K // 2

    hs_scr[...] = jnp.zeros_like(hs_scr)
    for r in range(rows):
        hs_scr[r, pl.ds(CONV_PAD, width), :] = xh_ref[pl.ds(r * width, width), :]
    wh = wh_ref[...]
    acc_h = jnp.zeros((tokens, n_hor), F32)
    for tap in range(CONV_K):
        win = hs_scr[:, pl.ds(CONV_PAD - half + tap, width), :].reshape(tokens, n_hor)
        acc_h = acc_h + win * wh[tap:tap + 1, :]

    if has_ver:
        vs_scr[pl.ds(0, tokens), :] = jnp.where(i == 0, 0.0, vp_ref[...])
        vs_scr[pl.ds(tokens, tokens), :] = vc_ref[...]
        vs_scr[pl.ds(2 * tokens, tokens), :] = jnp.where(i == nt - 1, 0.0, vn_ref[...])
        wv = wv_ref[...]
        acc_v = jnp.zeros((tokens, CONV_WIDTH - n_hor), F32)
        for tap in range(CONV_K):
            acc_v = acc_v + vs_scr[pl.ds(tokens + (tap - half) * width, tokens), :] * wv[tap:tap + 1, :]
        x = jnp.concatenate([acc_h, acc_v], axis=1)
    else:
        x = acc_h
    x = x + b_ref[...]
    xc = x - jnp.mean(x, axis=-1, keepdims=True)
    y = xc * lax.rsqrt(jnp.mean(xc * xc, axis=-1, keepdims=True) + EPS) * g_ref[...] + be_ref[...]
    o_ref[...] = _silu(y)


def _conv_call(xc, dw_w, dw_b, ln_g, ln_b, grid_rows):
    b, seq, ch = xc.shape
    vec = pl.BlockSpec((1, ch), lambda b, i: (0, 0))
    if grid_rows > 0:
        width, rows, n_hor = GRID_W, 16, ch // 2
        tokens = rows * width
        nt = seq // tokens
        half_spec = lambda f: pl.BlockSpec((None, tokens, n_hor), f)
        in_specs = [half_spec(lambda b, i: (b, i, 0)),
                    half_spec(lambda b, i: (b, jnp.maximum(i - 1, 0), 1)),
                    half_spec(lambda b, i: (b, i, 1)),
                    half_spec(lambda b, i: (b, jnp.minimum(i + 1, nt - 1), 1)),
                    pl.BlockSpec((CONV_K, n_hor), lambda b, i: (0, 0)),
                    pl.BlockSpec((CONV_K, n_hor), lambda b, i: (0, 1)),
                    vec, vec, vec]
        args = [xc, xc, xc, xc, dw_w, dw_w, dw_b, ln_g, ln_b]
        scratch = [pltpu.VMEM((rows, width + 2 * CONV_PAD, n_hor), F32),
                   pltpu.VMEM((3 * tokens, ch - n_hor), F32)]
        has_ver = True
    else:
        width, rows, n_hor = seq, 1, ch
        tokens = seq
        nt = 1
        in_specs = [pl.BlockSpec((None, tokens, ch), lambda b, i: (b, 0, 0)),
                    pl.BlockSpec((CONV_K, ch), lambda b, i: (0, 0)),
                    vec, vec, vec]
        args = [xc, dw_w, dw_b, ln_g, ln_b]
        scratch = [pltpu.VMEM((rows, width + 2 * CONV_PAD, n_hor), F32)]
        has_ver = False
    return pl.pallas_call(
        functools.partial(_conv_kernel, rows, width, n_hor, has_ver, nt),
        grid=(b, nt),
        in_specs=in_specs,
        out_specs=pl.BlockSpec((None, tokens, ch), lambda b, i: (b, i, 0)),
        out_shape=jax.ShapeDtypeStruct((b, seq, ch), F32),
        scratch_shapes=scratch,
        compiler_params=_params("parallel", "parallel"),
        name="conformer_conv",
    )(*args)


def _lb_kernel(x_ref, o_ref):
    x = x_ref[...]
    n = x.shape[0]
    rows = [x[r:r + 1, :] for r in range(n)]
    mx = functools.reduce(jnp.maximum, rows)
    ex = [jnp.exp(r - mx) for r in rows]
    tot = functools.reduce(lambda p, q: p + q, ex)
    run = None
    for r in range(n):
        run = ex[r] / tot if run is None else run + ex[r] / tot
        o_ref[pl.ds(r, 1), :] = run


def _lb_call(logits):
    return pl.pallas_call(_lb_kernel, out_shape=jax.ShapeDtypeStruct(logits.shape, F32), name="hgrn_lb")(logits)


def _hgrn_kernel(rev, t_blk, nb, q_ref, f_ref, v_ref, lb_ref, s0_ref, o_ref, sfin_ref,
                 st_scr, kv_scr, sall_scr):
    i = pl.program_id(1)
    c = HGRN_CHUNK
    nch = t_blk // c
    hh = HGRN_HEADS

    @pl.when(i == 0)
    def _():
        st_scr[...] = s0_ref[...]

    lb = lb_ref[...]
    f = lb + (1.0 - lb) * jax.nn.sigmoid(f_ref[...])
    k = 1.0 - f
    log_f = jnp.log(f)
    row = lax.broadcasted_iota(jnp.int32, (t_blk, t_blk), 0)
    col = lax.broadcasted_iota(jnp.int32, (t_blk, t_blk), 1)
    same = _same_block(row, col, c)
    incl = same & ((row <= col) if rev else (row >= col))
    cums = _mm_exact_lhs(jnp.concatenate([incl, same], axis=0).astype(BF16), log_f)
    bc = cums[:t_blk]
    tot = cums[t_blk:]
    q_in = q_ref[...] * jnp.exp(bc)
    k_in = k * jnp.exp(-bc)
    k_out = k * jnp.exp(tot - bc)
    cd = jnp.exp(tot)
    v = v_ref[...]
    order = range(nch - 1, -1, -1) if rev else range(nch)
    for hd in range(hh):
        ls = slice(hd * HGRN_DIM, (hd + 1) * HGRN_DIM)
        attn = jnp.where(incl, _mm_nt(q_in[:, ls], k_in[:, ls]), 0.0)
        o_intra = _mm(attn, v[:, ls])
        for n in range(nch):
            rows = slice(n * c, (n + 1) * c)
            kv_scr[hd, n] = _mm_tn(v[rows, ls], k_out[rows, ls])
        st = st_scr[hd]
        for n in order:
            sall_scr[hd, n] = st
            st = st * cd[n * c:n * c + 1, ls] + kv_scr[hd, n]
        st_scr[hd] = st
        for n in range(nch):
            rows = slice(n * c, (n + 1) * c)
            o_ref[pl.ds(n * c, c), pl.ds(hd * HGRN_DIM, HGRN_DIM)] = (
                _mm_nt(q_in[rows, ls], sall_scr[hd, n]) + o_intra[rows])

    @pl.when(i == nb - 1)
    def _():
        sfin_ref[...] = st_scr[...]


def _hgrn_call(q, fgate, v, lb_row, s0, rev):
    b, seq, _ = q.shape
    t_blk = min(256, seq)
    nb = seq // t_blk
    hh = HGRN_HEADS

    def blk(i):
        return (nb - 1 - i) if rev else i

    tok = pl.BlockSpec((None, t_blk, HGRN_WIDTH), lambda b, i: (b, blk(i), 0))
    state_spec = pl.BlockSpec((None, hh, HGRN_DIM, HGRN_DIM), lambda b, i: (b, 0, 0, 0))
    chunk_states = pltpu.VMEM((hh, t_blk // HGRN_CHUNK, HGRN_DIM, HGRN_DIM), F32)
    return pl.pallas_call(
        functools.partial(_hgrn_kernel, rev, t_blk, nb),
        grid=(b, nb),
        in_specs=[tok, tok, tok, pl.BlockSpec((1, HGRN_WIDTH), lambda b, i: (0, 0)), state_spec],
        out_specs=[tok, state_spec],
        out_shape=[jax.ShapeDtypeStruct((b, seq, HGRN_WIDTH), F32),
                   jax.ShapeDtypeStruct((b, hh, HGRN_DIM, HGRN_DIM), F32)],
        scratch_shapes=[pltpu.VMEM((hh, HGRN_DIM, HGRN_DIM), F32), chunk_states, chunk_states],
        compiler_params=_params("parallel", "arbitrary"),
        name="hgrn_bwd" if rev else "hgrn_fwd",
    )(q, fgate, v, lb_row, s0)


def _head_norm(o, gate, g):
    outs = []
    for hd in range(o.shape[1] // 128):
        x = o[:, hd * 128:(hd + 1) * 128]
        x = x * lax.rsqrt(jnp.mean(x * x, axis=-1, keepdims=True) + EPS) * g
        outs.append(x * _silu(gate[:, hd * 128:(hd + 1) * 128]))
    return jnp.concatenate(outs, axis=1)


def _out_ab_kernel(h_ref, gt_ref, of_ref, ob_ref, z_ref, y5_ref, u_ref, ng_ref, dsk_ref, glw_ref, glb_ref,
                   wout_ref, o_ref):
    a = _head_norm(of_ref[...] + ob_ref[...], z_ref[...], ng_ref[...])
    y = y5_ref[...] + dsk_ref[...] * u_ref[...]
    y = 0.5 * y * (1.0 + jnp.tanh(math.sqrt(2.0 / math.pi) * (y + 0.044715 * (y * y * y))))
    bmix = y * jax.nn.sigmoid(_mm(y, glw_ref[...]) + glb_ref[...])
    mix = _mm(jnp.concatenate([a, bmix], axis=1), wout_ref[...])
    o_ref[...] = h_ref[...] + gt_ref[...] * mix


def _out_cd_kernel(h_ref, gt_ref, c_ref, of_ref, ob_ref, g_ref, ng_ref, wout_ref, o_ref):
    dmix = _head_norm(of_ref[...] + ob_ref[...], g_ref[...], ng_ref[...])
    mix = _mm(jnp.concatenate([c_ref[...], dmix], axis=1), wout_ref[...])
    o_ref[...] = h_ref[...] + gt_ref[...] * mix


def _out_call(kern, name, h, mod, layer, ctx, toks, consts):
    b, seq, _ = h.shape
    tm = min(512, seq)
    in_specs = [pl.BlockSpec((None, tm, D_MODEL), lambda b, i: (b, i, 0)), _mod_spec(layer, 1, 2, ctx)]
    in_specs += [pl.BlockSpec((None, tm, t.shape[2]), lambda b, i: (b, i, 0)) for t in toks]
    in_specs += [pl.BlockSpec(cst.shape, lambda b, i: (0, 0), pipeline_mode=pl.Buffered(1)) for cst in consts]
    return pl.pallas_call(
        kern,
        grid=(b, seq // tm),
        in_specs=in_specs,
        out_specs=pl.BlockSpec((None, tm, D_MODEL), lambda b, i: (b, i, 0)),
        out_shape=jax.ShapeDtypeStruct(h.shape, F32),
        compiler_params=_params("parallel", "parallel"),
        name=name,
    )(h, mod, *toks, *consts)


def _mixer_ab(h, hc, mod, layer, need_ctx, prm):
    ng = prm["norm_g1"]
    outs = {}
    gdn_state = [jnp.zeros((h.shape[0], GDN_HEADS, GDN_DIM, GDN_DIM), F32)] * 2
    s5_state = [jnp.zeros((2, h.shape[0], 1, S5_LANES), F32)] * 2
    for ctx, x in ((True, hc), (False, h)):
        q, k, v, z, u, gates = _inproj_ab_call(x, mod, layer, ctx, ng, prm["w_in"], prm["conv_w"])
        o_f, sf = _gdn_call(q, k, v, gates, prm["alog"], prm["dtb"], gdn_state[0], rev=False)
        o_b, sb = _gdn_call(q, k, v, gates, prm["alog"], prm["dtb"], gdn_state[1], rev=True)
        y5, f_re, f_im = _s5_mix(u, prm["s5"], s5_state[0], s5_state[1])
        gdn_state = [sf, sb]
        s5_state = [f_re, f_im]
        if ctx and not need_ctx:
            continue
        outs[ctx] = _out_call(_out_ab_kernel, "out_ab", x, mod, layer, ctx, [o_f, o_b, z, y5, u],
                              [prm["gdn_g"], prm["s5_d"], prm["glu_w"], prm["glu_b"], prm["w_out"]])
    return outs[False], outs.get(True)


def _mixer_cd(h, hc, mod, layer, need_ctx, prm, grid_rows):
    ng = prm["norm_g1"]
    outs = {}
    state = [jnp.zeros((h.shape[0], HGRN_HEADS, HGRN_DIM, HGRN_DIM), F32)] * 2
    for ctx, x in ((True, hc), (False, h)):
        xc, q, f_f, f_b, iv, g = _inproj_call(x, mod, layer, ctx, ng, prm["w_in"], prm["pieces"], glu=True)
        o_f, sf = _hgrn_call(q, f_f, iv, prm["lb"], state[0], rev=False)
        o_b, sb = _hgrn_call(q, f_b, iv, prm["lb"], state[1], rev=True)
        state = [sf, sb]
        if ctx and not need_ctx:
            continue
        cmix = _conv_call(xc, prm["dw_w"], prm["dw_b"], prm["ln_g"], prm["ln_b"], 0 if ctx else grid_rows)
        outs[ctx] = _out_call(_out_cd_kernel, "out_cd", x, mod, layer, ctx, [cmix, o_f, o_b, g],
                              [prm["hgrn_g"], prm["w_out"]])
    return outs[False], outs.get(True)


def kernel(x, c, ctx, c_ctx, ada_w, ada_b, norm_g, ffn_w_up, ffn_w_down, ab_w_in, ab_w_out, gdn_conv_w, gdn_a_log, gdn_dt_bias, gdn_norm_g, s5_lambda_re, s5_lambda_im, s5_log_step, s5_b_re, s5_b_im, s5_c_re, s5_c_im, s5_d, s5_glu_w, s5_glu_b, cd_w_in, cd_w_out, conv_dw_w, conv_dw_b, conv_ln_g, conv_ln_b, hgrn_lb_logits, hgrn_norm_g, final_norm_g):
    depth = ada_w.shape[0]
    batch = x.shape[0]
    grid_rows = x.shape[1] // GRID_W
    assert batch <= 2 and x.shape[1] % 1024 == 0 and ctx.shape[1] % 256 == 0

    cvec = jnp.zeros((8, D_MODEL), F32).at[:batch].set(c).at[2].set(c_ctx)
    mod = _ada_call(cvec, ada_w, ada_b)
    lb_all = _lb_call(hgrn_lb_logits)
    row = lambda v: v.reshape(1, -1)
    gw, kw = GDN_WIDTH, HGRN_WIDTH

    w_up = ffn_w_up.astype(BF16)
    w_down = ffn_w_down.astype(BF16)
    h, hc = x, ctx
    for l in range(depth):
        last = l == depth - 1
        h = _ffn_call(h, mod, l, 0, False, row(norm_g[l, 0]), w_up, w_down, (l, 0))
        hc = _ffn_call(hc, mod, l, 0, True, row(norm_g[l, 0]), w_up, w_down, (l, 0))
        if l % 2 == 0:
            e = l // 2
            w = ab_w_in[e]
            ng4 = 4 * GDN_HEADS
            w = jnp.concatenate([w[:, :4 * gw], w[:, 4 * gw + ng4:], w[:, 4 * gw:4 * gw + ng4],
                                 jnp.zeros((D_MODEL, 128 - ng4), F32)], axis=1).astype(BF16)
            pad8 = lambda v: jnp.zeros((1, 128), F32).at[0, :2 * GDN_HEADS].set(v.reshape(-1))
            prm = dict(
                norm_g1=row(norm_g[l, 1]),
                w_in=w, conv_w=gdn_conv_w[e], alog=pad8(gdn_a_log[e]), dtb=pad8(gdn_dt_bias[e]),
                gdn_g=row(gdn_norm_g[e]),
                s5=_s5_prepare_ops(s5_lambda_re[e], s5_lambda_im[e], s5_log_step[e], s5_b_re[e], s5_b_im[e],
                                   s5_c_re[e], s5_c_im[e]),
                s5_d=row(s5_d[e]), glu_w=s5_glu_w[e].astype(BF16), glu_b=row(s5_glu_b[e]),
                w_out=ab_w_out[e].astype(BF16))
            h, hc_new = _mixer_ab(h, hc, mod, l, not last, prm)
        else:
            o = l // 2
            cw = CONV_WIDTH
            prm = dict(
                norm_g1=row(norm_g[l, 1]),
                w_in=cd_w_in[o].astype(BF16),
                pieces=[(k * cw, cw) for k in range(2)] + [(2 * cw + k * kw, kw) for k in range(5)],
                lb=lb_all[o:o + 1], hgrn_g=row(hgrn_norm_g[o]),
                dw_w=conv_dw_w[o], dw_b=row(conv_dw_b[o]), ln_g=row(conv_ln_g[o]), ln_b=row(conv_ln_b[o]),
                w_out=cd_w_out[o].astype(BF16))
            h, hc_new = _mixer_cd(h, hc, mod, l, not last, prm, grid_rows)
        h = _ffn_call(h, mod, l, 2, False, row(norm_g[l, 2]), w_up, w_down, (l, 1),
                      final_g=row(final_norm_g) if last else None)
        if not last:
            hc = _ffn_call(hc_new, mod, l, 2, True, row(norm_g[l, 2]), w_up, w_down, (l, 1))
    return h
```

```python
import functools
import math

import jax
import jax.numpy as jnp
from jax import lax
from jax.experimental import pallas as pl
from jax.experimental.pallas import tpu as pltpu

F32 = jnp.float32
BF16 = jnp.bfloat16
HIGHEST = lax.Precision.HIGHEST
EPS = 1e-6

D_MODEL = 1024
GRID_W = 64
FFN_DIM = 2816

GDN_HEADS = 4
GDN_DIM = 128
GDN_WIDTH = GDN_HEADS * GDN_DIM
GDN_CONV = 5
GDN_CHUNK = 256

S5_WIDTH = 512
S5_GROUP = 16
S5_GROUPS = 32
S5_STATE = 64
S5_CHUNK = 16
S5_ROW = S5_CHUNK * S5_GROUP
S5_LANES = S5_GROUPS * S5_STATE

CONV_WIDTH = 512
CONV_K = 31
CONV_PAD = 16

HGRN_HEADS = 4
HGRN_DIM = 128
HGRN_WIDTH = HGRN_HEADS * HGRN_DIM
HGRN_CHUNK = 16

V7X_VMEM_LIMIT = 48 * 1024 * 1024


def _params(*sem):
    return pltpu.CompilerParams(dimension_semantics=sem, vmem_limit_bytes=V7X_VMEM_LIMIT)


def _silu(x):
    return x * jax.nn.sigmoid(x)


def _mm(a, b):
    return jnp.dot(a.astype(BF16), b.astype(BF16), preferred_element_type=F32)


def _mm_nt(a, b):
    return lax.dot_general(a.astype(BF16), b.astype(BF16), (((1,), (1,)), ((), ())),
                           preferred_element_type=F32)


def _mm_tn(a, b):
    return lax.dot_general(a.astype(BF16), b.astype(BF16), (((0,), (0,)), ((), ())),
                           preferred_element_type=F32)


def _split3(x):
    hi = x.astype(BF16)
    r1 = x - hi.astype(F32)
    mid = r1.astype(BF16)
    lo = (r1 - mid.astype(F32)).astype(BF16)
    return hi, mid, lo


def _mm_exact_lhs(a_bf16, x):
    return jnp.dot(jnp.concatenate([a_bf16] * 3, axis=1), jnp.concatenate(_split3(x), axis=0),
                   preferred_element_type=F32)


def _mm_exact_rhs(x, b_bf16):
    return jnp.dot(jnp.concatenate(_split3(x), axis=1), jnp.concatenate([b_bf16] * 3, axis=0),
                   preferred_element_type=F32)


def _mm_x3(a, b):
    ah = a.astype(BF16)
    al = (a - ah.astype(F32)).astype(BF16)
    bh = b.astype(BF16)
    bl = (b - bh.astype(F32)).astype(BF16)
    return jnp.dot(jnp.concatenate([ah, ah, al], axis=1), jnp.concatenate([bh, bl, bh], axis=0),
                   preferred_element_type=F32)


def _same_block(i, j, size):
    return (i ^ j) < size


def _ada_norm(x, g, scale, shift):
    y = x * lax.rsqrt(jnp.mean(x * x, axis=-1, keepdims=True) + EPS) * g
    return y * (1.0 + scale) + shift


def _ada_kernel(c_ref, w_ref, b_ref, o_ref):
    o_ref[...] = _mm(_silu(c_ref[...]), w_ref[...]) + b_ref[...]


def _ada_call(cvec, ada_w, ada_b):
    depth = ada_w.shape[0]
    ncol = ada_w.shape[2] // D_MODEL
    out = pl.pallas_call(
        _ada_kernel,
        grid=(depth, ncol),
        in_specs=[pl.BlockSpec((8, D_MODEL), lambda l, j: (0, 0)),
                  pl.BlockSpec((None, D_MODEL, D_MODEL), lambda l, j: (l, 0, j)),
                  pl.BlockSpec((None, 1, D_MODEL), lambda l, j: (l, 0, j))],
        out_specs=pl.BlockSpec((None, None, 8, D_MODEL), lambda l, j: (l, j, 0, 0)),
        out_shape=jax.ShapeDtypeStruct((depth, ncol, 8, D_MODEL), F32),
        compiler_params=_params("arbitrary", "arbitrary"),
        name="ada_mod",
    )(cvec, ada_w, ada_b.reshape(depth, 1, -1))
    return out.reshape(depth * ncol * 8, 1, D_MODEL)


def _mod_spec(layer, sub, kind, ctx):
    base = (layer * 9 + sub * 3 + kind) * 8
    if ctx:
        return pl.BlockSpec((None, 1, D_MODEL), lambda b, *_: (base + 2, 0, 0))
    return pl.BlockSpec((None, 1, D_MODEL), lambda b, *_: (base + b, 0, 0))


def _ffn_kernel(tf, final, h_ref, sh_ref, sc_ref, gt_ref, g_ref, wup_ref, wd_ref, *rest):
    if final:
        fg_ref, o_ref = rest
    else:
        (o_ref,) = rest
    x = h_ref[...]
    xn = _ada_norm(x, g_ref[...], sc_ref[...], sh_ref[...]).astype(BF16)
    acc = None
    for j in range(FFN_DIM // tf):
        gate = jnp.dot(xn, wup_ref[:, j * tf:(j + 1) * tf], preferred_element_type=F32)
        up = jnp.dot(xn, wup_ref[:, FFN_DIM + j * tf:FFN_DIM + (j + 1) * tf], preferred_element_type=F32)
        act = (_silu(gate) * up).astype(BF16)
        part = jnp.dot(act, wd_ref[j * tf:(j + 1) * tf, :], preferred_element_type=F32)
        acc = part if acc is None else acc + part
    y = x + 0.5 * gt_ref[...] * acc
    if final:
        y = y * lax.rsqrt(jnp.mean(y * y, axis=-1, keepdims=True) + EPS) * fg_ref[...]
    o_ref[...] = y


def _ffn_call(h, mod, layer, sub, ctx, norm_g_row, w_up, w_down, which, final_g=None):
    b, seq, _ = h.shape
    tm = min(512, seq)
    tf = 256
    final = final_g is not None
    resident = dict(pipeline_mode=pl.Buffered(1))
    in_specs = [
        pl.BlockSpec((None, tm, D_MODEL), lambda b, i: (b, i, 0)),
        _mod_spec(layer, sub, 0, ctx), _mod_spec(layer, sub, 1, ctx), _mod_spec(layer, sub, 2, ctx),
        pl.BlockSpec((1, D_MODEL), lambda b, i: (0, 0)),
        pl.BlockSpec((None, None, D_MODEL, 2 * FFN_DIM), lambda b, i: which + (0, 0), **resident),
        pl.BlockSpec((None, None, FFN_DIM, D_MODEL), lambda b, i: which + (0, 0), **resident),
    ]
    args = [h, mod, mod, mod, norm_g_row, w_up, w_down]
    if final:
        in_specs.append(pl.BlockSpec((1, D_MODEL), lambda b, i: (0, 0)))
        args.append(final_g)
    return pl.pallas_call(
        functools.partial(_ffn_kernel, tf, final),
        grid=(b, seq // tm),
        in_specs=in_specs,
        out_specs=pl.BlockSpec((None, tm, D_MODEL), lambda b, i: (b, i, 0)),
        out_shape=jax.ShapeDtypeStruct(h.shape, F32),
        compiler_params=_params("parallel", "parallel"),
        name="ffn",
    )(*args)


def _inproj_kernel(nw, glu, h_ref, sh_ref, sc_ref, g_ref, *refs):
    w_refs, o_refs = refs[:nw], refs[nw:]
    xn = _ada_norm(h_ref[...], g_ref[...], sc_ref[...], sh_ref[...]).astype(BF16)
    outs = [jnp.dot(xn, w[...], preferred_element_type=F32) for w in w_refs]
    if glu:
        outs = [outs[0] * jax.nn.sigmoid(outs[1])] + outs[2:]
    for o_ref, val in zip(o_refs, outs):
        o_ref[...] = val


def _inproj_call(h, mod, layer, ctx, norm_g_row, w, pieces, glu):
    b, seq, _ = h.shape
    tm = min(512, seq)
    widths = [wd for _, wd in pieces]
    out_widths = widths[1:] if glu else widths
    in_specs = [pl.BlockSpec((None, tm, D_MODEL), lambda b, i: (b, i, 0)),
                _mod_spec(layer, 1, 0, ctx), _mod_spec(layer, 1, 1, ctx),
                pl.BlockSpec((1, D_MODEL), lambda b, i: (0, 0))]
    for off, wd in pieces:
        assert off % wd == 0
        in_specs.append(pl.BlockSpec((D_MODEL, wd), functools.partial(lambda b, i, blk: (0, blk), blk=off // wd),
                                     pipeline_mode=pl.Buffered(1)))
    return pl.pallas_call(
        functools.partial(_inproj_kernel, len(pieces), glu),
        grid=(b, seq // tm),
        in_specs=in_specs,
        out_specs=[pl.BlockSpec((None, tm, wd), lambda b, i: (b, i, 0)) for wd in out_widths],
        out_shape=[jax.ShapeDtypeStruct((b, seq, wd), F32) for wd in out_widths],
        compiler_params=_params("parallel", "parallel"),
        name="inproj",
    )(h, mod, mod, norm_g_row, *([w] * len(pieces)))


def _inproj_ab_kernel(tm, nt, h_ref, hp_ref, hn_ref, sh_ref, sc_ref, g_ref, wqkv_ref, wz_ref, wu_ref, wg_ref, cw_ref,
                      q_ref, k_ref, v_ref, z_ref, u_ref, gates_ref, pad_scr):
    i = pl.program_id(1)
    norm = lambda x: _ada_norm(x, g_ref[...], sc_ref[...], sh_ref[...]).astype(BF16)
    dot = functools.partial(jnp.dot, preferred_element_type=F32)
    xn = norm(h_ref[...])
    z_ref[...] = dot(xn, wz_ref[...])
    u_ref[...] = dot(xn, wu_ref[...])
    gates_ref[...] = dot(xn, wg_ref[...])
    halo = dot(norm(jnp.concatenate([hp_ref[...], hn_ref[...]], axis=0)), wqkv_ref[...])
    pad_scr[pl.ds(0, 8), :] = jnp.where(i == 0, 0.0, halo[:8])
    pad_scr[pl.ds(8, tm), :] = dot(xn, wqkv_ref[...])
    pad_scr[pl.ds(8 + tm, 8), :] = jnp.where(i == nt - 1, 0.0, halo[8:])
    w = cw_ref[...]
    acc = None
    for tap in range(GDN_CONV):
        term = pad_scr[pl.ds(8 - GDN_CONV // 2 + tap, tm), :] * w[tap:tap + 1, :]
        acc = term if acc is None else acc + term
    y = _silu(acc)

    def l2n_heads(x, scale):
        outs = []
        for hd in range(GDN_HEADS):
            xh = x[:, hd * GDN_DIM:(hd + 1) * GDN_DIM]
            outs.append(xh * (lax.rsqrt(jnp.sum(xh * xh, axis=-1, keepdims=True) + EPS) * scale))
        return jnp.concatenate(outs, axis=1)

    q_ref[...] = l2n_heads(y[:, :GDN_WIDTH], GDN_DIM ** -0.5)
    k_ref[...] = l2n_heads(y[:, GDN_WIDTH:2 * GDN_WIDTH], 1.0)
    v_ref[...] = y[:, 2 * GDN_WIDTH:]


def _inproj_ab_call(h, mod, layer, ctx, norm_g_row, w, conv_w):
    b, seq, _ = h.shape
    tm = min(512, seq)
    nt = seq // tm
    r8 = tm // 8
    gw = GDN_WIDTH
    resident = dict(pipeline_mode=pl.Buffered(1))
    wspec = lambda wd, blk: pl.BlockSpec((D_MODEL, wd), lambda b, i: (0, blk), **resident)
    tok = lambda wd: pl.BlockSpec((None, tm, wd), lambda b, i: (b, i, 0))
    in_specs = [tok(D_MODEL),
                pl.BlockSpec((None, 8, D_MODEL), lambda b, i: (b, jnp.maximum(i * r8 - 1, 0), 0)),
                pl.BlockSpec((None, 8, D_MODEL), lambda b, i: (b, jnp.minimum((i + 1) * r8, seq // 8 - 1), 0)),
                _mod_spec(layer, 1, 0, ctx), _mod_spec(layer, 1, 1, ctx),
                pl.BlockSpec((1, D_MODEL), lambda b, i: (0, 0)),
                wspec(3 * gw, 0), wspec(gw, 3), wspec(S5_WIDTH, 4), wspec(128, (4 * gw + S5_WIDTH) // 128),
                pl.BlockSpec((GDN_CONV, 3 * gw), lambda b, i: (0, 0))]
    widths = [gw, gw, gw, gw, S5_WIDTH, 128]
    return pl.pallas_call(
        functools.partial(_inproj_ab_kernel, tm, nt),
        grid=(b, nt),
        in_specs=in_specs,
        out_specs=[tok(wd) for wd in widths],
        out_shape=[jax.ShapeDtypeStruct((b, seq, wd), F32) for wd in widths],
        scratch_shapes=[pltpu.VMEM((tm + 16, 3 * gw), F32)],
        compiler_params=_params("parallel", "parallel"),
        name="inproj_ab",
    )(h, h, h, mod, mod, norm_g_row, w, w, w, w, conv_w)


def _gdn_kernel(rev, t_blk, nb, q_scr, k_scr, v_scr, gates_ref, alog_ref, dtb_ref, s0_ref,
                o_ref, sfin_ref,
                s_scr, sel_scr, ut_scr, w_scr, qd_scr, kd_scr, qk_scr, cd_scr, gt_scr):
    i = pl.program_id(1)
    c = GDN_CHUNK
    nch = t_blk // c
    hh = GDN_HEADS
    width = hh * GDN_DIM

    @pl.when(i == 0)
    def _():
        s_scr[...] = s0_ref[...]

    gates = gates_ref[...]
    lane = lax.broadcasted_iota(jnp.int32, gates.shape, 1)
    log_a = -jnp.exp(alog_ref[...]) * jax.nn.softplus(gates + dtb_ref[...])
    trow = lax.broadcasted_iota(jnp.int32, (t_blk, t_blk), 0)
    tcol = lax.broadcasted_iota(jnp.int32, (t_blk, t_blk), 1)
    cum = _same_block(trow, tcol, c) & ((trow <= tcol) if rev else (trow >= tcol))
    g_all = _mm_exact_lhs(cum.astype(BF16), log_a)
    for m in range(nch):
        gt_scr[m] = g_all[m * c:(m + 1) * c, :].T
    mixed = jnp.where(lane < 2 * hh, g_all, jax.nn.sigmoid(gates))
    d_off = hh if rev else 0
    src = lax.broadcasted_iota(jnp.int32, (128, 2 * width), 0)
    dst = lax.broadcasted_iota(jnp.int32, (128, 2 * width), 1)
    want = d_off + (dst >> 8) + jnp.where((dst & 128) != 0, 2 * hh, 0)
    sel_scr[...] = _mm_exact_rhs(mixed, (src == want).astype(BF16))

    row = lax.broadcasted_iota(jnp.int32, (c, c), 0)
    col = lax.broadcasted_iota(jnp.int32, (c, c), 1)
    incl = (row <= col) if rev else (row >= col)
    strict = (row < col) if rev else (row > col)
    last = 0 if rev else c - 1
    diag16 = _same_block(row, col, 16)
    levels = []
    d = 16
    while d < c:
        levels.append(_same_block(row, col, 2 * d) & jnp.logical_not(_same_block(row, col, d)))
        d *= 2

    def lanes_c(x):
        return x[:, :c] if c <= 128 else jnp.concatenate([x] * (c // 128), axis=1)

    heads = range(hh)

    def inverse_minus_identity(a_mats):
        p = [jnp.where(diag16, -a, 0.0) for a in a_mats]
        n = list(p)
        p = [_mm(x, x) for x in p]
        for it in range(3):
            if it < 2:
                both = [_mm(jnp.concatenate([n[i], p[i]], axis=0), p[i]) for i in heads]
                n = [n[i] + p[i] + both[i][:c] for i in heads]
                p = [both[i][c:] for i in heads]
            else:
                n = [n[i] + p[i] + _mm(n[i], p[i]) for i in heads]
        for mask in levels:
            off = [jnp.where(mask, a, 0.0) for a in a_mats]
            x = [off[i] + _mm(n[i], off[i]) for i in heads]
            n = [n[i] - (x[i] + _mm(x[i], n[i])) for i in heads]
        return n

    def prep(m, carry):
        rows = pl.ds(pl.multiple_of(m * c, c), c)
        lanes = [pl.ds(hd * GDN_DIM, GDN_DIM) for hd in heads]
        q = [q_scr[rows, lanes[hd]] for hd in heads]
        k = [k_scr[rows, lanes[hd]] for hd in heads]
        g = [sel_scr[rows, pl.ds(hd * 256, 128)] for hd in heads]
        beta = [sel_scr[rows, pl.ds(hd * 256 + 128, 128)] for hd in heads]
        decay = []
        for hd in heads:
            diff = lanes_c(g[hd]) - gt_scr[m, pl.ds(d_off + hd, 1), :]
            decay.append(jnp.where(incl, jnp.exp(jnp.where(incl, diff, 0.0)), 0.0))
        kq = [_mm_nt(jnp.concatenate([q[hd], k[hd]], axis=0), k[hd]) for hd in heads]
        a_mats = [jnp.where(strict, kq[hd][c:] * decay[hd], 0.0) * lanes_c(beta[hd]) for hd in heads]
        n_mats = inverse_minus_identity(a_mats)
        for hd in heads:
            eg = jnp.exp(g[hd])
            rhs = jnp.concatenate([v_scr[rows, lanes[hd]] * beta[hd], k[hd] * (beta[hd] * eg)], axis=1)
            sol = rhs + _mm_x3(n_mats[hd], rhs)
            g_last = g[hd][last:last + 1, :]
            ut_scr[rows, lanes[hd]] = sol[:, :GDN_DIM]
            w_scr[rows, lanes[hd]] = sol[:, GDN_DIM:]
            qd_scr[rows, lanes[hd]] = q[hd] * eg
            kd_scr[rows, lanes[hd]] = k[hd] * jnp.exp(g_last - g[hd])
            qk_scr[rows, pl.ds(hd * c, c)] = jnp.where(incl, kq[hd][:c] * decay[hd], 0.0)
            cd_scr[m, hd] = jnp.broadcast_to(jnp.exp(g_last), (8, GDN_DIM))
        return carry

    lax.fori_loop(0, nch, prep, 0, unroll=True)

    def step(ci, carry):
        cidx = (nch - 1 - ci) if rev else ci
        rows = pl.ds(pl.multiple_of(cidx * c, c), c)
        lanes = [pl.ds(hd * GDN_DIM, GDN_DIM) for hd in heads]
        s = [s_scr[hd] for hd in heads]
        ws = [_mm(jnp.concatenate([w_scr[rows, lanes[hd]], qd_scr[rows, lanes[hd]]], axis=0), s[hd]) for hd in heads]
        u = [ut_scr[rows, lanes[hd]] - ws[hd][:c] for hd in heads]
        intra = [_mm(qk_scr[rows, pl.ds(hd * c, c)], u[hd]) for hd in heads]
        outer = [_mm_tn(kd_scr[rows, lanes[hd]], u[hd]) for hd in heads]
        for hd in heads:
            o_ref[rows, lanes[hd]] = ws[hd][c:] + intra[hd]
            s_scr[hd] = s[hd] * cd_scr[cidx, hd][0:1, :] + outer[hd]
        return carry

    lax.fori_loop(0, nch, step, 0)

    @pl.when(i == nb - 1)
    def _():
        sfin_ref[...] = s_scr[...]


def _gdn_call(q, k, v, gates, alog_row, dtb_row, s0, rev):
    b, seq, _ = q.shape
    t_blk = min(512, seq)
    nb = seq // t_blk
    hh = GDN_HEADS
    width = GDN_WIDTH

    def blk(i):
        return (nb - 1 - i) if rev else i

    tile = pl.BlockSpec((None, t_blk, width), lambda b, i: (b, blk(i), 0))
    state_spec = pl.BlockSpec((None, hh, GDN_DIM, GDN_DIM), lambda b, i: (b, 0, 0, 0))
    in_specs = [tile, tile, tile,
                pl.BlockSpec((None, t_blk, 128), lambda b, i: (b, blk(i), 0)),
                pl.BlockSpec((1, 128), lambda b, i: (0, 0)),
                pl.BlockSpec((1, 128), lambda b, i: (0, 0)),
                state_spec]
    args = [q, k, v, gates, alog_row, dtb_row, s0]
    tok = lambda w: pltpu.VMEM((t_blk, w), F32)
    return pl.pallas_call(
        functools.partial(_gdn_kernel, rev, t_blk, nb),
        grid=(b, nb),
        in_specs=in_specs,
        out_specs=[tile, state_spec],
        out_shape=[jax.ShapeDtypeStruct((b, seq, width), F32),
                   jax.ShapeDtypeStruct((b, hh, GDN_DIM, GDN_DIM), F32)],
        scratch_shapes=[pltpu.VMEM((hh, GDN_DIM, GDN_DIM), F32),
                        tok(2 * width),
                        tok(width), tok(width), tok(width), tok(width), tok(hh * GDN_CHUNK),
                        pltpu.VMEM((t_blk // GDN_CHUNK, hh, 8, GDN_DIM), F32),
                        pltpu.VMEM((t_blk // GDN_CHUNK, 128, GDN_CHUNK), F32)],
        compiler_params=_params("parallel", "arbitrary"),
        name="gdn_bwd" if rev else "gdn_fwd",
    )(*args)


def _s5_ops_kernel(lr_ref, li_ref, ls_ref, btr_ref, bti_ref, cr_ref, ci_ref,
                   m_ref, winr_ref, wini_ref, woutr_ref, wouti_ref, a16r_ref, a16i_ref):
    d = pl.program_id(0)
    lr = lr_ref[...]
    li = li_ref[...]
    dt = jnp.exp(ls_ref[...])

    def apow(kk):
        mag = jnp.exp(lr * dt * kk)
        ang = li * dt * kk
        return mag * jnp.cos(ang), mag * jnp.sin(ang)

    def cmul(xr, xi, yr, yi):
        return xr * yr - xi * yi, xr * yi + xi * yr

    ar, ai = apow(1.0)
    den = lr * lr + li * li
    nr, ni = ar - 1.0, ai
    zr = (nr * lr + ni * li) / den
    zi = (ni * lr - nr * li) / den
    bbr, bbi = cmul(zr, zi, btr_ref[...], bti_ref[...])
    cr, ci = cr_ref[...], ci_ref[...]

    t16 = lax.broadcasted_iota(jnp.int32, (S5_CHUNK, 1), 0)
    tv16 = jnp.where(d == 0, t16, S5_CHUNK - 1 - t16).astype(F32)
    t_row = lax.broadcasted_iota(jnp.int32, (S5_ROW, 1), 0) // S5_GROUP
    t_col = lax.broadcasted_iota(jnp.int32, (1, S5_ROW), 1) // S5_GROUP
    tv_row = jnp.where(d == 0, t_row, S5_CHUNK - 1 - t_row)
    tv_col = jnp.where(d == 0, t_col, S5_CHUNK - 1 - t_col)
    tile = lambda x: jnp.concatenate([x] * S5_CHUNK, axis=0)
    spread = (lax.broadcasted_iota(jnp.int32, (S5_ROW, S5_CHUNK), 0) // S5_GROUP
              == lax.broadcasted_iota(jnp.int32, (S5_ROW, S5_CHUNK), 1)).astype(BF16)
    rep = lambda x: _mm_exact_lhs(spread, x)
    pos_r, pos_i = apow(tv16)
    neg_r, neg_i = apow(-tv16)
    xr, xi = cmul(tile(bbr), tile(bbi), rep(neg_r), rep(neg_i))
    yr, yi = cmul(tile(cr), tile(ci), rep(pos_r), rep(pos_i))
    nt = lambda p, q: lax.dot_general(p, q, (((1,), (1,)), ((), ())), precision=HIGHEST,
                                      preferred_element_type=F32)
    m = nt(xr, yr) - nt(xi, yi)
    m_ref[...] = jnp.where(tv_col >= tv_row, m, 0.0)

    p15r, p15i = apow(S5_CHUNK - 1.0)
    winr_ref[...], wini_ref[...] = cmul(xr, xi, p15r, p15i)
    wr, wi = cmul(yr, yi, ar, ai)
    woutr_ref[...] = wr
    wouti_ref[...] = -wi
    a16r, a16i = apow(float(S5_CHUNK))
    a16r_ref[...] = a16r
    a16i_ref[...] = a16i


def _s5_ops_call(lam_re, lam_im, log_step, bt_re, bt_im, c_re, c_im):
    g, p, cg = S5_GROUPS, S5_STATE, S5_GROUP
    ls = jnp.broadcast_to(log_step[:, :, None, None], (2, g, 1, p))
    lam_spec = pl.BlockSpec((None, None, 1, p), lambda d, gi: (d, gi, 0, 0))
    par_spec = pl.BlockSpec((None, cg, p), lambda d, gi: (gi, 0, 0))
    out = lambda *shape: pl.BlockSpec((None, None) + shape, lambda d, gi: (d, gi, 0, 0))
    return pl.pallas_call(
        _s5_ops_kernel,
        grid=(2, g),
        in_specs=[lam_spec, lam_spec, lam_spec, par_spec, par_spec, par_spec, par_spec],
        out_specs=[out(S5_ROW, S5_ROW), out(S5_ROW, p), out(S5_ROW, p), out(S5_ROW, p), out(S5_ROW, p),
                   out(1, p), out(1, p)],
        out_shape=[jax.ShapeDtypeStruct((2, g, S5_ROW, S5_ROW), F32)]
        + [jax.ShapeDtypeStruct((2, g, S5_ROW, p), F32)] * 4
        + [jax.ShapeDtypeStruct((2, g, 1, p), F32)] * 2,
        compiler_params=_params("parallel", "parallel"),
        name="s5_ops",
    )(lam_re.reshape(2, g, 1, p), lam_im.reshape(2, g, 1, p), ls, bt_re, bt_im, c_re, c_im)


def _s5_in_kernel(u_ref, wr_ref, wi_ref, sr_ref, si_ref):
    dot = functools.partial(jnp.dot, preferred_element_type=F32)
    u0, u1 = u_ref[0], u_ref[1]
    sr_ref[...] = dot(u0, wr_ref[0]) + dot(u1, wr_ref[1])
    si_ref[...] = dot(u0, wi_ref[0]) + dot(u1, wi_ref[1])


def _s5_in_call(ug, win_r, win_i):
    b, g, nc, _ = ug.shape
    w_spec = pl.BlockSpec((None, 2, S5_ROW, 128), lambda d, b, gp: (d, gp, 0, 0))
    o_spec = pl.BlockSpec((None, None, nc, 128), lambda d, b, gp: (d, b, 0, gp))
    return pl.pallas_call(
        _s5_in_kernel,
        grid=(2, b, g // 2),
        in_specs=[pl.BlockSpec((None, 2, nc, S5_ROW), lambda d, b, gp: (b, gp, 0, 0)), w_spec, w_spec],
        out_specs=[o_spec, o_spec],
        out_shape=[jax.ShapeDtypeStruct((2, b, nc, S5_LANES), F32)] * 2,
        compiler_params=_params("parallel", "parallel", "parallel"),
        name="s5_in",
    )(ug, win_r, win_i)


def _s5_scan_kernel(nc, sr_ref, si_ref, ar_ref, ai_ref, h0r_ref, h0i_ref, hr_ref, hi_ref, fr_ref, fi_ref):
    coef = [(ar_ref[d], ai_ref[d]) for d in range(2)]

    def body(i, carry):
        out = []
        for d in range(2):
            hr, hi = carry[d]
            ar, ai = coef[d]
            n = i if d == 0 else nc - 1 - i
            hr_ref[d, pl.ds(n, 1), :] = hr
            hi_ref[d, pl.ds(n, 1), :] = hi
            sr = sr_ref[d, pl.ds(n, 1), :]
            si = si_ref[d, pl.ds(n, 1), :]
            out.append((ar * hr - ai * hi + sr, ar * hi + ai * hr + si))
        return tuple(out)

    fin = lax.fori_loop(0, nc, body, tuple((h0r_ref[d], h0i_ref[d]) for d in range(2)), unroll=4)
    for d in range(2):
        fr_ref[d] = fin[d][0]
        fi_ref[d] = fin[d][1]


def _s5_scan_call(s_re, s_im, a16_re, a16_im, h0_re, h0_im):
    _, b, nc, _ = s_re.shape
    tl = 512
    big = pl.BlockSpec((2, None, nc, tl), lambda b, j: (0, b, 0, j))
    a_spec = pl.BlockSpec((2, 1, tl), lambda b, j: (0, 0, j))
    st_spec = pl.BlockSpec((2, None, 1, tl), lambda b, j: (0, b, 0, j))
    return pl.pallas_call(
        functools.partial(_s5_scan_kernel, nc),
        grid=(b, S5_LANES // tl),
        in_specs=[big, big, a_spec, a_spec, st_spec, st_spec],
        out_specs=[big, big, st_spec, st_spec],
        out_shape=[jax.ShapeDtypeStruct(s_re.shape, F32)] * 2
        + [jax.ShapeDtypeStruct((2, b, 1, S5_LANES), F32)] * 2,
        compiler_params=_params("parallel", "parallel"),
        name="s5_scan",
    )(s_re, s_im, a16_re, a16_im, h0_re, h0_im)


def _s5_out_kernel(u_ref, m_ref, hr_ref, hi_ref, wr_ref, wi_ref, y_ref):
    u = u_ref[...]
    acc = None
    for d in range(2):
        y = (jnp.dot(u, m_ref[d], preferred_element_type=F32)
             + _mm(hr_ref[d], wr_ref[d]) + _mm(hi_ref[d], wi_ref[d]))
        acc = y if acc is None else acc + y
    y_ref[...] = acc


def _s5_out_call(ug, m, h_re, h_im, wout_r, wout_i):
    b, g, nc, _ = ug.shape
    h_spec = pl.BlockSpec((2, None, nc, 128), lambda b, gi: (0, b, 0, gi // 2))
    w_spec = pl.BlockSpec((2, None, 128, S5_ROW), lambda b, gi: (0, gi, 0, 0))
    return pl.pallas_call(
        _s5_out_kernel,
        grid=(b, g),
        in_specs=[pl.BlockSpec((None, None, nc, S5_ROW), lambda b, gi: (b, gi, 0, 0)),
                  pl.BlockSpec((2, None, S5_ROW, S5_ROW), lambda b, gi: (0, gi, 0, 0)),
                  h_spec, h_spec, w_spec, w_spec],
        out_specs=pl.BlockSpec((None, None, nc, S5_ROW), lambda b, gi: (b, gi, 0, 0)),
        out_shape=jax.ShapeDtypeStruct((b, g, nc, S5_ROW), F32),
        compiler_params=_params("parallel", "parallel"),
        name="s5_out",
    )(ug, m, h_re, h_im, wout_r, wout_i)


def _s5_prepare_ops(lam_re, lam_im, log_step, b_re, b_im, c_re, c_im):
    m, win_r, win_i, wout_r, wout_i, a16_r, a16_i = _s5_ops_call(
        lam_re, lam_im, log_step, jnp.swapaxes(b_re, 1, 2), jnp.swapaxes(b_im, 1, 2), c_re, c_im)
    p = S5_STATE
    odd = (jnp.arange(S5_GROUPS) % 2 == 1)[None, :, None, None]

    def pad_in(w):
        z = jnp.zeros_like(w)
        return jnp.where(odd, jnp.concatenate([z, w], -1), jnp.concatenate([w, z], -1)).astype(BF16)

    def pad_out(w):
        wt = jnp.swapaxes(w, 2, 3)
        z = jnp.zeros_like(wt)
        return jnp.where(odd, jnp.concatenate([z, wt], 2), jnp.concatenate([wt, z], 2)).astype(BF16)

    return dict(m=m.astype(BF16), win_r=pad_in(win_r), win_i=pad_in(win_i),
                wout_r=pad_out(wout_r), wout_i=pad_out(wout_i),
                a16_r=a16_r.reshape(2, 1, S5_LANES), a16_i=a16_i.reshape(2, 1, S5_LANES))


def _s5_mix(u, ops, h0_re, h0_im):
    b, seq, _ = u.shape
    nc = seq // S5_CHUNK
    ug = u.reshape(b, nc, S5_CHUNK, S5_GROUPS, S5_GROUP).transpose(0, 3, 1, 2, 4)
    ug = ug.reshape(b, S5_GROUPS, nc, S5_ROW).astype(BF16)
    s_re, s_im = _s5_in_call(ug, ops["win_r"], ops["win_i"])
    h_re, h_im, f_re, f_im = _s5_scan_call(s_re, s_im, ops["a16_r"], ops["a16_i"], h0_re, h0_im)
    y = _s5_out_call(ug, ops["m"], h_re, h_im, ops["wout_r"], ops["wout_i"])
    y = y.reshape(b, S5_GROUPS, nc, S5_CHUNK, S5_GROUP).transpose(0, 2, 3, 1, 4).reshape(b, seq, S5_WIDTH)
    return y, f_re, f_im


def _conv_kernel(rows, width, n_hor, has_ver, nt, *refs):
    if has_ver:
        (xh_ref, vp_ref, vc_ref, vn_ref, wh_ref, wv_ref, b_ref, g_ref, be_ref,
         o_ref, hs_scr, vs_scr) = refs
    else:
        xh_ref, wh_ref, b_ref, g_ref, be_ref, o_ref, hs_scr = refs
    i = pl.program_id(1)
    tokens = rows * width
    slot = width + 2 * CONV_PAD
    half = CONV_K // 2

    hs_scr[...] = jnp.zeros_like(hs_scr)
    for r in range(rows):
        hs_scr[r, pl.ds(CONV_PAD, width), :] = xh_ref[pl.ds(r * width, width), :]
    wh = wh_ref[...]
    acc_h = jnp.zeros((tokens, n_hor), F32)
    for tap in range(CONV_K):
        win = hs_scr[:, pl.ds(CONV_PAD - half + tap, width), :].reshape(tokens, n_hor)
        acc_h = acc_h + win * wh[tap:tap + 1, :]

    if has_ver:
        vs_scr[pl.ds(0, tokens), :] = jnp.where(i == 0, 0.0, vp_ref[...])
        vs_scr[pl.ds(tokens, tokens), :] = vc_ref[...]
        vs_scr[pl.ds(2 * tokens, tokens), :] = jnp.where(i == nt - 1, 0.0, vn_ref[...])
        wv = wv_ref[...]
        acc_v = jnp.zeros((tokens, CONV_WIDTH - n_hor), F32)
        for tap in range(CONV_K):
            acc_v = acc_v + vs_scr[pl.ds(tokens + (tap - half) * width, tokens), :] * wv[tap:tap + 1, :]
        x = jnp.concatenate([acc_h, acc_v], axis=1)
    else:
        x = acc_h
    x = x + b_ref[...]
    xc = x - jnp.mean(x, axis=-1, keepdims=True)
    y = xc * lax.rsqrt(jnp.mean(xc * xc, axis=-1, keepdims=True) + EPS) * g_ref[...] + be_ref[...]
    o_ref[...] = _silu(y)


def _conv_call(xc, dw_w, dw_b, ln_g, ln_b, grid_rows):
    b, seq, ch = xc.shape
    vec = pl.BlockSpec((1, ch), lambda b, i: (0, 0))
    if grid_rows > 0:
        width, rows, n_hor = GRID_W, 16, ch // 2
        tokens = rows * width
        nt = seq // tokens
        half_spec = lambda f: pl.BlockSpec((None, tokens, n_hor), f)
        in_specs = [half_spec(lambda b, i: (b, i, 0)),
                    half_spec(lambda b, i: (b, jnp.maximum(i - 1, 0), 1)),
                    half_spec(lambda b, i: (b, i, 1)),
                    half_spec(lambda b, i: (b, jnp.minimum(i + 1, nt - 1), 1)),
                    pl.BlockSpec((CONV_K, n_hor), lambda b, i: (0, 0)),
                    pl.BlockSpec((CONV_K, n_hor), lambda b, i: (0, 1)),
                    vec, vec, vec]
        args = [xc, xc, xc, xc, dw_w, dw_w, dw_b, ln_g, ln_b]
        scratch = [pltpu.VMEM((rows, width + 2 * CONV_PAD, n_hor), F32),
                   pltpu.VMEM((3 * tokens, ch - n_hor), F32)]
        has_ver = True
    else:
        width, rows, n_hor = seq, 1, ch
        tokens = seq
        nt = 1
        in_specs = [pl.BlockSpec((None, tokens, ch), lambda b, i: (b, 0, 0)),
                    pl.BlockSpec((CONV_K, ch), lambda b, i: (0, 0)),
                    vec, vec, vec]
        args = [xc, dw_w, dw_b, ln_g, ln_b]
        scratch = [pltpu.VMEM((rows, width + 2 * CONV_PAD, n_hor), F32)]
        has_ver = False
    return pl.pallas_call(
        functools.partial(_conv_kernel, rows, width, n_hor, has_ver, nt),
        grid=(b, nt),
        in_specs=in_specs,
        out_specs=pl.BlockSpec((None, tokens, ch), lambda b, i: (b, i, 0)),
        out_shape=jax.ShapeDtypeStruct((b, seq, ch), F32),
        scratch_shapes=scratch,
        compiler_params=_params("parallel", "parallel"),
        name="conformer_conv",
    )(*args)


def _lb_kernel(x_ref, o_ref):
    x = x_ref[...]
    n = x.shape[0]
    rows = [x[r:r + 1, :] for r in range(n)]
    mx = functools.reduce(jnp.maximum, rows)
    ex = [jnp.exp(r - mx) for r in rows]
    tot = functools.reduce(lambda p, q: p + q, ex)
    run = None
    for r in range(n):
        run = ex[r] / tot if run is None else run + ex[r] / tot
        o_ref[pl.ds(r, 1), :] = run


def _lb_call(logits):
    return pl.pallas_call(_lb_kernel, out_shape=jax.ShapeDtypeStruct(logits.shape, F32), name="hgrn_lb")(logits)


def _hgrn_kernel(rev, t_blk, nb, q_ref, f_ref, v_ref, lb_ref, s0_ref, o_ref, sfin_ref,
                 st_scr, kv_scr, sall_scr):
    i = pl.program_id(1)
    c = HGRN_CHUNK
    nch = t_blk // c
    hh = HGRN_HEADS

    @pl.when(i == 0)
    def _():
        st_scr[...] = s0_ref[...]

    lb = lb_ref[...]
    f = lb + (1.0 - lb) * jax.nn.sigmoid(f_ref[...])
    k = 1.0 - f
    log_f = jnp.log(f)
    row = lax.broadcasted_iota(jnp.int32, (t_blk, t_blk), 0)
    col = lax.broadcasted_iota(jnp.int32, (t_blk, t_blk), 1)
    same = _same_block(row, col, c)
    incl = same & ((row <= col) if rev else (row >= col))
    cums = _mm_exact_lhs(jnp.concatenate([incl, same], axis=0).astype(BF16), log_f)
    bc = cums[:t_blk]
    tot = cums[t_blk:]
    q_in = q_ref[...] * jnp.exp(bc)
    k_in = k * jnp.exp(-bc)
    k_out = k * jnp.exp(tot - bc)
    cd = jnp.exp(tot)
    v = v_ref[...]
    order = range(nch - 1, -1, -1) if rev else range(nch)
    heads = range(hh)
    ls = [slice(hd * HGRN_DIM, (hd + 1) * HGRN_DIM) for hd in heads]
    attn = [jnp.where(incl, _mm_nt(q_in[:, ls[hd]], k_in[:, ls[hd]]), 0.0) for hd in heads]
    o_intra = [_mm(attn[hd], v[:, ls[hd]]) for hd in heads]
    for n in range(nch):
        rows = slice(n * c, (n + 1) * c)
        for hd in heads:
            kv_scr[hd, n] = _mm_tn(v[rows, ls[hd]], k_out[rows, ls[hd]])
    st = [st_scr[hd] for hd in heads]
    for n in order:
        for hd in heads:
            sall_scr[hd, n] = st[hd]
            st[hd] = st[hd] * cd[n * c:n * c + 1, ls[hd]] + kv_scr[hd, n]
    for hd in heads:
        st_scr[hd] = st[hd]
    for n in range(nch):
        rows = slice(n * c, (n + 1) * c)
        for hd in heads:
            o_ref[pl.ds(n * c, c), pl.ds(hd * HGRN_DIM, HGRN_DIM)] = (
                _mm_nt(q_in[rows, ls[hd]], sall_scr[hd, n]) + o_intra[hd][rows])

    @pl.when(i == nb - 1)
    def _():
        sfin_ref[...] = st_scr[...]


def _hgrn_call(q, fgate, v, lb_row, s0, rev):
    b, seq, _ = q.shape
    t_blk = min(256, seq)
    nb = seq // t_blk
    hh = HGRN_HEADS

    def blk(i):
        return (nb - 1 - i) if rev else i

    tok = pl.BlockSpec((None, t_blk, HGRN_WIDTH), lambda b, i: (b, blk(i), 0))
    state_spec = pl.BlockSpec((None, hh, HGRN_DIM, HGRN_DIM), lambda b, i: (b, 0, 0, 0))
    chunk_states = pltpu.VMEM((hh, t_blk // HGRN_CHUNK, HGRN_DIM, HGRN_DIM), F32)
    return pl.pallas_call(
        functools.partial(_hgrn_kernel, rev, t_blk, nb),
        grid=(b, nb),
        in_specs=[tok, tok, tok, pl.BlockSpec((1, HGRN_WIDTH), lambda b, i: (0, 0)), state_spec],
        out_specs=[tok, state_spec],
        out_shape=[jax.ShapeDtypeStruct((b, seq, HGRN_WIDTH), F32),
                   jax.ShapeDtypeStruct((b, hh, HGRN_DIM, HGRN_DIM), F32)],
        scratch_shapes=[pltpu.VMEM((hh, HGRN_DIM, HGRN_DIM), F32), chunk_states, chunk_states],
        compiler_params=_params("parallel", "arbitrary"),
        name="hgrn_bwd" if rev else "hgrn_fwd",
    )(q, fgate, v, lb_row, s0)


def _head_norm(o, gate, g):
    outs = []
    for hd in range(o.shape[1] // 128):
        x = o[:, hd * 128:(hd + 1) * 128]
        x = x * lax.rsqrt(jnp.mean(x * x, axis=-1, keepdims=True) + EPS) * g
        outs.append(x * _silu(gate[:, hd * 128:(hd + 1) * 128]))
    return jnp.concatenate(outs, axis=1)


def _out_ab_kernel(h_ref, gt_ref, of_ref, ob_ref, z_ref, y5_ref, u_ref, ng_ref, dsk_ref, glw_ref, glb_ref,
                   wout_ref, o_ref):
    a = _head_norm(of_ref[...] + ob_ref[...], z_ref[...], ng_ref[...])
    y = y5_ref[...] + dsk_ref[...] * u_ref[...]
    y = 0.5 * y * (1.0 + jnp.tanh(math.sqrt(2.0 / math.pi) * (y + 0.044715 * (y * y * y))))
    bmix = y * jax.nn.sigmoid(_mm(y, glw_ref[...]) + glb_ref[...])
    mix = _mm(jnp.concatenate([a, bmix], axis=1), wout_ref[...])
    o_ref[...] = h_ref[...] + gt_ref[...] * mix


def _out_cd_kernel(h_ref, gt_ref, c_ref, of_ref, ob_ref, g_ref, ng_ref, wout_ref, o_ref):
    dmix = _head_norm(of_ref[...] + ob_ref[...], g_ref[...], ng_ref[...])
    mix = _mm(jnp.concatenate([c_ref[...], dmix], axis=1), wout_ref[...])
    o_ref[...] = h_ref[...] + gt_ref[...] * mix


def _out_call(kern, name, h, mod, layer, ctx, toks, consts):
    b, seq, _ = h.shape
    tm = min(512, seq)
    in_specs = [pl.BlockSpec((None, tm, D_MODEL), lambda b, i: (b, i, 0)), _mod_spec(layer, 1, 2, ctx)]
    in_specs += [pl.BlockSpec((None, tm, t.shape[2]), lambda b, i: (b, i, 0)) for t in toks]
    in_specs += [pl.BlockSpec(cst.shape, lambda b, i: (0, 0), pipeline_mode=pl.Buffered(1)) for cst in consts]
    return pl.pallas_call(
        kern,
        grid=(b, seq // tm),
        in_specs=in_specs,
        out_specs=pl.BlockSpec((None, tm, D_MODEL), lambda b, i: (b, i, 0)),
        out_shape=jax.ShapeDtypeStruct(h.shape, F32),
        compiler_params=_params("parallel", "parallel"),
        name=name,
    )(h, mod, *toks, *consts)


def _mixer_ab(h, hc, mod, layer, need_ctx, prm):
    ng = prm["norm_g1"]
    outs = {}
    gdn_state = [jnp.zeros((h.shape[0], GDN_HEADS, GDN_DIM, GDN_DIM), F32)] * 2
    s5_state = [jnp.zeros((2, h.shape[0], 1, S5_LANES), F32)] * 2
    for ctx, x in ((True, hc), (False, h)):
        q, k, v, z, u, gates = _inproj_ab_call(x, mod, layer, ctx, ng, prm["w_in"], prm["conv_w"])
        o_f, sf = _gdn_call(q, k, v, gates, prm["alog"], prm["dtb"], gdn_state[0], rev=False)
        o_b, sb = _gdn_call(q, k, v, gates, prm["alog"], prm["dtb"], gdn_state[1], rev=True)
        y5, f_re, f_im = _s5_mix(u, prm["s5"], s5_state[0], s5_state[1])
        gdn_state = [sf, sb]
        s5_state = [f_re, f_im]
        if ctx and not need_ctx:
            continue
        outs[ctx] = _out_call(_out_ab_kernel, "out_ab", x, mod, layer, ctx, [o_f, o_b, z, y5, u],
                              [prm["gdn_g"], prm["s5_d"], prm["glu_w"], prm["glu_b"], prm["w_out"]])
    return outs[False], outs.get(True)


def _mixer_cd(h, hc, mod, layer, need_ctx, prm, grid_rows):
    ng = prm["norm_g1"]
    outs = {}
    state = [jnp.zeros((h.shape[0], HGRN_HEADS, HGRN_DIM, HGRN_DIM), F32)] * 2
    for ctx, x in ((True, hc), (False, h)):
        xc, q, f_f, f_b, iv, g = _inproj_call(x, mod, layer, ctx, ng, prm["w_in"], prm["pieces"], glu=True)
        o_f, sf = _hgrn_call(q, f_f, iv, prm["lb"], state[0], rev=False)
        o_b, sb = _hgrn_call(q, f_b, iv, prm["lb"], state[1], rev=True)
        state = [sf, sb]
        if ctx and not need_ctx:
            continue
        cmix = _conv_call(xc, prm["dw_w"], prm["dw_b"], prm["ln_g"], prm["ln_b"], 0 if ctx else grid_rows)
        outs[ctx] = _out_call(_out_cd_kernel, "out_cd", x, mod, layer, ctx, [cmix, o_f, o_b, g],
                              [prm["hgrn_g"], prm["w_out"]])
    return outs[False], outs.get(True)


def kernel(x, c, ctx, c_ctx, ada_w, ada_b, norm_g, ffn_w_up, ffn_w_down, ab_w_in, ab_w_out, gdn_conv_w, gdn_a_log, gdn_dt_bias, gdn_norm_g, s5_lambda_re, s5_lambda_im, s5_log_step, s5_b_re, s5_b_im, s5_c_re, s5_c_im, s5_d, s5_glu_w, s5_glu_b, cd_w_in, cd_w_out, conv_dw_w, conv_dw_b, conv_ln_g, conv_ln_b, hgrn_lb_logits, hgrn_norm_g, final_norm_g):
    depth = ada_w.shape[0]
    batch = x.shape[0]
    grid_rows = x.shape[1] // GRID_W
    assert batch <= 2 and x.shape[1] % 1024 == 0 and ctx.shape[1] % 256 == 0

    cvec = jnp.zeros((8, D_MODEL), F32).at[:batch].set(c).at[2].set(c_ctx)
    mod = _ada_call(cvec, ada_w, ada_b)
    lb_all = _lb_call(hgrn_lb_logits)
    row = lambda v: v.reshape(1, -1)
    gw, kw = GDN_WIDTH, HGRN_WIDTH

    w_up = ffn_w_up.astype(BF16)
    w_down = ffn_w_down.astype(BF16)
    h, hc = x, ctx
    for l in range(depth):
        last = l == depth - 1
        h = _ffn_call(h, mod, l, 0, False, row(norm_g[l, 0]), w_up, w_down, (l, 0))
        hc = _ffn_call(hc, mod, l, 0, True, row(norm_g[l, 0]), w_up, w_down, (l, 0))
        if l % 2 == 0:
            e = l // 2
            w = ab_w_in[e]
            ng4 = 4 * GDN_HEADS
            w = jnp.concatenate([w[:, :4 * gw], w[:, 4 * gw + ng4:], w[:, 4 * gw:4 * gw + ng4],
                                 jnp.zeros((D_MODEL, 128 - ng4), F32)], axis=1).astype(BF16)
            pad8 = lambda v: jnp.zeros((1, 128), F32).at[0, :2 * GDN_HEADS].set(v.reshape(-1))
            prm = dict(
                norm_g1=row(norm_g[l, 1]),
                w_in=w, conv_w=gdn_conv_w[e], alog=pad8(gdn_a_log[e]), dtb=pad8(gdn_dt_bias[e]),
                gdn_g=row(gdn_norm_g[e]),
                s5=_s5_prepare_ops(s5_lambda_re[e], s5_lambda_im[e], s5_log_step[e], s5_b_re[e], s5_b_im[e],
                                   s5_c_re[e], s5_c_im[e]),
                s5_d=row(s5_d[e]), glu_w=s5_glu_w[e].astype(BF16), glu_b=row(s5_glu_b[e]),
                w_out=ab_w_out[e].astype(BF16))
            h, hc_new = _mixer_ab(h, hc, mod, l, not last, prm)
        else:
            o = l // 2
            cw = CONV_WIDTH
            prm = dict(
                norm_g1=row(norm_g[l, 1]),
                w_in=cd_w_in[o].astype(BF16),
                pieces=[(k * cw, cw) for k in range(2)] + [(2 * cw + k * kw, kw) for k in range(5)],
                lb=lb_all[o:o + 1], hgrn_g=row(hgrn_norm_g[o]),
                dw_w=conv_dw_w[o], dw_b=row(conv_dw_b[o]), ln_g=row(conv_ln_g[o]), ln_b=row(conv_ln_b[o]),
                w_out=cd_w_out[o].astype(BF16))
            h, hc_new = _mixer_cd(h, hc, mod, l, not last, prm, grid_rows)
        h = _ffn_call(h, mod, l, 2, False, row(norm_g[l, 2]), w_up, w_down, (l, 1),
                      final_g=row(final_norm_g) if last else None)
        if not last:
            hc = _ffn_call(hc_new, mod, l, 2, True, row(norm_g[l, 2]), w_up, w_down, (l, 1))
    return h
```

```python
import functools
import math

import jax
import jax.numpy as jnp
from jax import lax
from jax.experimental import pallas as pl
from jax.experimental.pallas import tpu as pltpu

F32 = jnp.float32
BF16 = jnp.bfloat16
HIGHEST = lax.Precision.HIGHEST
EPS = 1e-6

D_MODEL = 1024
GRID_W = 64
FFN_DIM = 2816

GDN_HEADS = 4
GDN_DIM = 128
GDN_WIDTH = GDN_HEADS * GDN_DIM
GDN_CONV = 5
GDN_CHUNK = 256

S5_WIDTH = 512
S5_GROUP = 16
S5_GROUPS = 32
S5_STATE = 64
S5_CHUNK = 16
S5_ROW = S5_CHUNK * S5_GROUP
S5_LANES = S5_GROUPS * S5_STATE

CONV_WIDTH = 512
CONV_K = 31
CONV_PAD = 16

HGRN_HEADS = 4
HGRN_DIM = 128
HGRN_WIDTH = HGRN_HEADS * HGRN_DIM
HGRN_CHUNK = 16

V7X_VMEM_LIMIT = 48 * 1024 * 1024


def _params(*sem):
    return pltpu.CompilerParams(dimension_semantics=sem, vmem_limit_bytes=V7X_VMEM_LIMIT)


def _silu(x):
    return x * jax.nn.sigmoid(x)


def _mm(a, b):
    return jnp.dot(a.astype(BF16), b.astype(BF16), preferred_element_type=F32)


def _mm_nt(a, b):
    return lax.dot_general(a.astype(BF16), b.astype(BF16), (((1,), (1,)), ((), ())),
                           preferred_element_type=F32)


def _mm_tn(a, b):
    return lax.dot_general(a.astype(BF16), b.astype(BF16), (((0,), (0,)), ((), ())),
                           preferred_element_type=F32)


def _split3(x):
    hi = x.astype(BF16)
    r1 = x - hi.astype(F32)
    mid = r1.astype(BF16)
    lo = (r1 - mid.astype(F32)).astype(BF16)
    return hi, mid, lo


def _mm_exact_lhs(a_bf16, x):
    return jnp.dot(jnp.concatenate([a_bf16] * 3, axis=1), jnp.concatenate(_split3(x), axis=0),
                   preferred_element_type=F32)


def _mm_exact_rhs(x, b_bf16):
    return jnp.dot(jnp.concatenate(_split3(x), axis=1), jnp.concatenate([b_bf16] * 3, axis=0),
                   preferred_element_type=F32)


def _mm_x3(a, b):
    ah = a.astype(BF16)
    al = (a - ah.astype(F32)).astype(BF16)
    bh = b.astype(BF16)
    bl = (b - bh.astype(F32)).astype(BF16)
    return jnp.dot(jnp.concatenate([ah, ah, al], axis=1), jnp.concatenate([bh, bl, bh], axis=0),
                   preferred_element_type=F32)


def _same_block(i, j, size):
    return (i ^ j) < size


def _ada_norm(x, g, scale, shift):
    y = x * lax.rsqrt(jnp.mean(x * x, axis=-1, keepdims=True) + EPS) * g
    return y * (1.0 + scale) + shift


def _ada_kernel(c_ref, w_ref, b_ref, o_ref):
    o_ref[...] = _mm(_silu(c_ref[...]), w_ref[...]) + b_ref[...]


def _ada_call(cvec, ada_w, ada_b):
    depth = ada_w.shape[0]
    ncol = ada_w.shape[2] // D_MODEL
    out = pl.pallas_call(
        _ada_kernel,
        grid=(depth, ncol),
        in_specs=[pl.BlockSpec((8, D_MODEL), lambda l, j: (0, 0)),
                  pl.BlockSpec((None, D_MODEL, D_MODEL), lambda l, j: (l, 0, j)),
                  pl.BlockSpec((None, 1, D_MODEL), lambda l, j: (l, 0, j))],
        out_specs=pl.BlockSpec((None, None, 8, D_MODEL), lambda l, j: (l, j, 0, 0)),
        out_shape=jax.ShapeDtypeStruct((depth, ncol, 8, D_MODEL), F32),
        compiler_params=_params("arbitrary", "arbitrary"),
        name="ada_mod",
    )(cvec, ada_w, ada_b.reshape(depth, 1, -1))
    return out.reshape(depth * ncol * 8, 1, D_MODEL)


def _mod_spec(layer, sub, kind, ctx):
    base = (layer * 9 + sub * 3 + kind) * 8
    if ctx:
        return pl.BlockSpec((None, 1, D_MODEL), lambda b, *_: (base + 2, 0, 0))
    return pl.BlockSpec((None, 1, D_MODEL), lambda b, *_: (base + b, 0, 0))


def _ffn_kernel(tf, final, h_ref, sh_ref, sc_ref, gt_ref, g_ref, wup_ref, wd_ref, *rest):
    if final:
        fg_ref, o_ref = rest
    else:
        (o_ref,) = rest
    x = h_ref[...]
    xn = _ada_norm(x, g_ref[...], sc_ref[...], sh_ref[...]).astype(BF16)
    acc = None
    for j in range(FFN_DIM // tf):
        gate = jnp.dot(xn, wup_ref[:, j * tf:(j + 1) * tf], preferred_element_type=F32)
        up = jnp.dot(xn, wup_ref[:, FFN_DIM + j * tf:FFN_DIM + (j + 1) * tf], preferred_element_type=F32)
        act = (_silu(gate) * up).astype(BF16)
        part = jnp.dot(act, wd_ref[j * tf:(j + 1) * tf, :], preferred_element_type=F32)
        acc = part if acc is None else acc + part
    y = x + 0.5 * gt_ref[...] * acc
    if final:
        y = y * lax.rsqrt(jnp.mean(y * y, axis=-1, keepdims=True) + EPS) * fg_ref[...]
    o_ref[...] = y


def _ffn_call(h, mod, layer, sub, ctx, norm_g_row, w_up, w_down, which, final_g=None):
    b, seq, _ = h.shape
    tm = min(512, seq)
    tf = 256
    final = final_g is not None
    resident = dict(pipeline_mode=pl.Buffered(1))
    in_specs = [
        pl.BlockSpec((None, tm, D_MODEL), lambda b, i: (b, i, 0)),
        _mod_spec(layer, sub, 0, ctx), _mod_spec(layer, sub, 1, ctx), _mod_spec(layer, sub, 2, ctx),
        pl.BlockSpec((1, D_MODEL), lambda b, i: (0, 0)),
        pl.BlockSpec((None, None, D_MODEL, 2 * FFN_DIM), lambda b, i: which + (0, 0), **resident),
        pl.BlockSpec((None, None, FFN_DIM, D_MODEL), lambda b, i: which + (0, 0), **resident),
    ]
    args = [h, mod, mod, mod, norm_g_row, w_up, w_down]
    if final:
        in_specs.append(pl.BlockSpec((1, D_MODEL), lambda b, i: (0, 0)))
        args.append(final_g)
    return pl.pallas_call(
        functools.partial(_ffn_kernel, tf, final),
        grid=(b, seq // tm),
        in_specs=in_specs,
        out_specs=pl.BlockSpec((None, tm, D_MODEL), lambda b, i: (b, i, 0)),
        out_shape=jax.ShapeDtypeStruct(h.shape, F32),
        compiler_params=_params("parallel", "parallel"),
        name="ffn",
    )(*args)


def _inproj_kernel(nw, glu, h_ref, sh_ref, sc_ref, g_ref, *refs):
    w_refs, o_refs = refs[:nw], refs[nw:]
    xn = _ada_norm(h_ref[...], g_ref[...], sc_ref[...], sh_ref[...]).astype(BF16)
    outs = [jnp.dot(xn, w[...], preferred_element_type=F32) for w in w_refs]
    if glu:
        outs = [outs[0] * jax.nn.sigmoid(outs[1])] + outs[2:]
    for o_ref, val in zip(o_refs, outs):
        o_ref[...] = val


def _inproj_call(h, mod, layer, ctx, norm_g_row, w, pieces, glu):
    b, seq, _ = h.shape
    tm = min(512, seq)
    widths = [wd for _, wd in pieces]
    out_widths = widths[1:] if glu else widths
    in_specs = [pl.BlockSpec((None, tm, D_MODEL), lambda b, i: (b, i, 0)),
                _mod_spec(layer, 1, 0, ctx), _mod_spec(layer, 1, 1, ctx),
                pl.BlockSpec((1, D_MODEL), lambda b, i: (0, 0))]
    for off, wd in pieces:
        assert off % wd == 0
        in_specs.append(pl.BlockSpec((D_MODEL, wd), functools.partial(lambda b, i, blk: (0, blk), blk=off // wd),
                                     pipeline_mode=pl.Buffered(1)))
    return pl.pallas_call(
        functools.partial(_inproj_kernel, len(pieces), glu),
        grid=(b, seq // tm),
        in_specs=in_specs,
        out_specs=[pl.BlockSpec((None, tm, wd), lambda b, i: (b, i, 0)) for wd in out_widths],
        out_shape=[jax.ShapeDtypeStruct((b, seq, wd), F32) for wd in out_widths],
        compiler_params=_params("parallel", "parallel"),
        name="inproj",
    )(h, mod, mod, norm_g_row, *([w] * len(pieces)))


def _inproj_ab_kernel(tm, nt, h_ref, hp_ref, hn_ref, sh_ref, sc_ref, g_ref, wqkv_ref, wz_ref, wu_ref, wg_ref, cw_ref,
                      q_ref, k_ref, v_ref, z_ref, u_ref, gates_ref, pad_scr):
    i = pl.program_id(1)
    norm = lambda x: _ada_norm(x, g_ref[...], sc_ref[...], sh_ref[...]).astype(BF16)
    dot = functools.partial(jnp.dot, preferred_element_type=F32)
    xn = norm(h_ref[...])
    z_ref[...] = dot(xn, wz_ref[...])
    u_ref[...] = dot(xn, wu_ref[...])
    gates_ref[...] = dot(xn, wg_ref[...])
    halo = dot(norm(jnp.concatenate([hp_ref[...], hn_ref[...]], axis=0)), wqkv_ref[...])
    pad_scr[pl.ds(0, 8), :] = jnp.where(i == 0, 0.0, halo[:8])
    pad_scr[pl.ds(8, tm), :] = dot(xn, wqkv_ref[...])
    pad_scr[pl.ds(8 + tm, 8), :] = jnp.where(i == nt - 1, 0.0, halo[8:])
    w = cw_ref[...]
    acc = None
    for tap in range(GDN_CONV):
        term = pad_scr[pl.ds(8 - GDN_CONV // 2 + tap, tm), :] * w[tap:tap + 1, :]
        acc = term if acc is None else acc + term
    y = _silu(acc)

    def l2n_heads(x, scale):
        outs = []
        for hd in range(GDN_HEADS):
            xh = x[:, hd * GDN_DIM:(hd + 1) * GDN_DIM]
            outs.append(xh * (lax.rsqrt(jnp.sum(xh * xh, axis=-1, keepdims=True) + EPS) * scale))
        return jnp.concatenate(outs, axis=1)

    q_ref[...] = l2n_heads(y[:, :GDN_WIDTH], GDN_DIM ** -0.5)
    k_ref[...] = l2n_heads(y[:, GDN_WIDTH:2 * GDN_WIDTH], 1.0)
    v_ref[...] = y[:, 2 * GDN_WIDTH:]


def _inproj_ab_call(h, mod, layer, ctx, norm_g_row, w, conv_w):
    b, seq, _ = h.shape
    tm = min(512, seq)
    nt = seq // tm
    r8 = tm // 8
    gw = GDN_WIDTH
    resident = dict(pipeline_mode=pl.Buffered(1))
    wspec = lambda wd, blk: pl.BlockSpec((D_MODEL, wd), lambda b, i: (0, blk), **resident)
    tok = lambda wd: pl.BlockSpec((None, tm, wd), lambda b, i: (b, i, 0))
    in_specs = [tok(D_MODEL),
                pl.BlockSpec((None, 8, D_MODEL), lambda b, i: (b, jnp.maximum(i * r8 - 1, 0), 0)),
                pl.BlockSpec((None, 8, D_MODEL), lambda b, i: (b, jnp.minimum((i + 1) * r8, seq // 8 - 1), 0)),
                _mod_spec(layer, 1, 0, ctx), _mod_spec(layer, 1, 1, ctx),
                pl.BlockSpec((1, D_MODEL), lambda b, i: (0, 0)),
                wspec(3 * gw, 0), wspec(gw, 3), wspec(S5_WIDTH, 4), wspec(128, (4 * gw + S5_WIDTH) // 128),
                pl.BlockSpec((GDN_CONV, 3 * gw), lambda b, i: (0, 0))]
    widths = [gw, gw, gw, gw, S5_WIDTH, 128]
    return pl.pallas_call(
        functools.partial(_inproj_ab_kernel, tm, nt),
        grid=(b, nt),
        in_specs=in_specs,
        out_specs=[tok(wd) for wd in widths],
        out_shape=[jax.ShapeDtypeStruct((b, seq, wd), F32) for wd in widths],
        scratch_shapes=[pltpu.VMEM((tm + 16, 3 * gw), F32)],
        compiler_params=_params("parallel", "parallel"),
        name="inproj_ab",
    )(h, h, h, mod, mod, norm_g_row, w, w, w, w, conv_w)


def _gdn_kernel(rev, t_blk, nb, q_scr, k_scr, v_scr, gates_ref, alog_ref, dtb_ref, s0_ref,
                o_ref, sfin_ref,
                s_scr, sel_scr, ut_scr, w_scr, qd_scr, kd_scr, qk_scr, cd_scr, gt_scr):
    i = pl.program_id(1)
    c = GDN_CHUNK
    nch = t_blk // c
    hh = GDN_HEADS
    width = hh * GDN_DIM

    @pl.when(i == 0)
    def _():
        s_scr[...] = s0_ref[...]

    gates = gates_ref[...]
    lane = lax.broadcasted_iota(jnp.int32, gates.shape, 1)
    log_a = -jnp.exp(alog_ref[...]) * jax.nn.softplus(gates + dtb_ref[...])
    trow = lax.broadcasted_iota(jnp.int32, (t_blk, t_blk), 0)
    tcol = lax.broadcasted_iota(jnp.int32, (t_blk, t_blk), 1)
    cum = _same_block(trow, tcol, c) & ((trow <= tcol) if rev else (trow >= tcol))
    g_all = _mm_exact_lhs(cum.astype(BF16), log_a)
    for m in range(nch):
        gt_scr[m] = g_all[m * c:(m + 1) * c, :].T
    mixed = jnp.where(lane < 2 * hh, g_all, jax.nn.sigmoid(gates))
    d_off = hh if rev else 0
    src = lax.broadcasted_iota(jnp.int32, (128, 2 * width), 0)
    dst = lax.broadcasted_iota(jnp.int32, (128, 2 * width), 1)
    want = d_off + (dst >> 8) + jnp.where((dst & 128) != 0, 2 * hh, 0)
    sel_scr[...] = _mm_exact_rhs(mixed, (src == want).astype(BF16))

    row = lax.broadcasted_iota(jnp.int32, (c, c), 0)
    col = lax.broadcasted_iota(jnp.int32, (c, c), 1)
    incl = (row <= col) if rev else (row >= col)
    strict = (row < col) if rev else (row > col)
    last = 0 if rev else c - 1
    diag16 = _same_block(row, col, 16)
    levels = []
    d = 16
    while d < c:
        levels.append(_same_block(row, col, 2 * d) & jnp.logical_not(_same_block(row, col, d)))
        d *= 2

    def lanes_c(x):
        return x[:, :c] if c <= 128 else jnp.concatenate([x] * (c // 128), axis=1)

    heads = range(hh)

    def inverse_minus_identity(a_mats):
        idx = range(len(a_mats))
        p = [jnp.where(diag16, -a, 0.0) for a in a_mats]
        n = list(p)
        p = [_mm(x, x) for x in p]
        for it in range(3):
            if it < 2:
                both = [_mm(jnp.concatenate([n[i], p[i]], axis=0), p[i]) for i in idx]
                n = [n[i] + p[i] + both[i][:c] for i in idx]
                p = [both[i][c:] for i in idx]
            else:
                n = [n[i] + p[i] + _mm(n[i], p[i]) for i in idx]
        for mask in levels:
            off = [jnp.where(mask, a, 0.0) for a in a_mats]
            x = [off[i] + _mm(n[i], off[i]) for i in idx]
            n = [n[i] - (x[i] + _mm(x[i], n[i])) for i in idx]
        return n

    def prep(m, carry):
        rows = pl.ds(pl.multiple_of(m * c, c), c)
        lanes = [pl.ds(hd * GDN_DIM, GDN_DIM) for hd in heads]
        q = [q_scr[rows, lanes[hd]] for hd in heads]
        k = [k_scr[rows, lanes[hd]] for hd in heads]
        g = [sel_scr[rows, pl.ds(hd * 256, 128)] for hd in heads]
        beta = [sel_scr[rows, pl.ds(hd * 256 + 128, 128)] for hd in heads]
        decay = []
        for hd in heads:
            diff = lanes_c(g[hd]) - gt_scr[m, pl.ds(d_off + hd, 1), :]
            decay.append(jnp.where(incl, jnp.exp(jnp.where(incl, diff, 0.0)), 0.0))
        kq = [_mm_nt(jnp.concatenate([q[hd], k[hd]], axis=0), k[hd]) for hd in heads]
        a_mats = [jnp.where(strict, kq[hd][c:] * decay[hd], 0.0) * lanes_c(beta[hd]) for hd in heads]
        n_mats = inverse_minus_identity(a_mats)
        for hd in heads:
            eg = jnp.exp(g[hd])
            rhs = jnp.concatenate([v_scr[rows, lanes[hd]] * beta[hd], k[hd] * (beta[hd] * eg)], axis=1)
            sol = rhs + _mm_x3(n_mats[hd], rhs)
            g_last = g[hd][last:last + 1, :]
            ut_scr[rows, lanes[hd]] = sol[:, :GDN_DIM]
            w_scr[rows, lanes[hd]] = sol[:, GDN_DIM:]
            qd_scr[rows, lanes[hd]] = q[hd] * eg
            kd_scr[rows, lanes[hd]] = k[hd] * jnp.exp(g_last - g[hd])
            qk_scr[rows, pl.ds(hd * c, c)] = jnp.where(incl, kq[hd][:c] * decay[hd], 0.0)
            cd_scr[m, hd] = jnp.broadcast_to(jnp.exp(g_last), (8, GDN_DIM))
        return carry

    lax.fori_loop(0, nch, prep, 0, unroll=True)


    def step(ci, carry):
        cidx = (nch - 1 - ci) if rev else ci
        rows = pl.ds(pl.multiple_of(cidx * c, c), c)
        lanes = [pl.ds(hd * GDN_DIM, GDN_DIM) for hd in heads]
        s = [s_scr[hd] for hd in heads]
        ws = [_mm(jnp.concatenate([w_scr[rows, lanes[hd]], qd_scr[rows, lanes[hd]]], axis=0), s[hd]) for hd in heads]
        u = [ut_scr[rows, lanes[hd]] - ws[hd][:c] for hd in heads]
        intra = [_mm(qk_scr[rows, pl.ds(hd * c, c)], u[hd]) for hd in heads]
        outer = [_mm_tn(kd_scr[rows, lanes[hd]], u[hd]) for hd in heads]
        for hd in heads:
            o_ref[rows, lanes[hd]] = ws[hd][c:] + intra[hd]
            s_scr[hd] = s[hd] * cd_scr[cidx, hd][0:1, :] + outer[hd]
        return carry

    lax.fori_loop(0, nch, step, 0)

    @pl.when(i == nb - 1)
    def _():
        sfin_ref[...] = s_scr[...]


def _gdn_call(q, k, v, gates, alog_row, dtb_row, s0, rev):
    b, seq, _ = q.shape
    t_blk = min(512, seq)
    nb = seq // t_blk
    hh = GDN_HEADS
    width = GDN_WIDTH

    def blk(i):
        return (nb - 1 - i) if rev else i

    tile = pl.BlockSpec((None, t_blk, width), lambda b, i: (b, blk(i), 0))
    state_spec = pl.BlockSpec((None, hh, GDN_DIM, GDN_DIM), lambda b, i: (b, 0, 0, 0))
    in_specs = [tile, tile, tile,
                pl.BlockSpec((None, t_blk, 128), lambda b, i: (b, blk(i), 0)),
                pl.BlockSpec((1, 128), lambda b, i: (0, 0)),
                pl.BlockSpec((1, 128), lambda b, i: (0, 0)),
                state_spec]
    args = [q, k, v, gates, alog_row, dtb_row, s0]
    tok = lambda w: pltpu.VMEM((t_blk, w), F32)
    return pl.pallas_call(
        functools.partial(_gdn_kernel, rev, t_blk, nb),
        grid=(b, nb),
        in_specs=in_specs,
        out_specs=[tile, state_spec],
        out_shape=[jax.ShapeDtypeStruct((b, seq, width), F32),
                   jax.ShapeDtypeStruct((b, hh, GDN_DIM, GDN_DIM), F32)],
        scratch_shapes=[pltpu.VMEM((hh, GDN_DIM, GDN_DIM), F32),
                        tok(2 * width),
                        tok(width), tok(width), tok(width), tok(width), tok(hh * GDN_CHUNK),
                        pltpu.VMEM((t_blk // GDN_CHUNK, hh, 8, GDN_DIM), F32),
                        pltpu.VMEM((t_blk // GDN_CHUNK, 128, GDN_CHUNK), F32)],
        compiler_params=_params("parallel", "arbitrary"),
        name="gdn_bwd" if rev else "gdn_fwd",
    )(*args)


def _s5_ops_kernel(lr_ref, li_ref, ls_ref, btr_ref, bti_ref, cr_ref, ci_ref,
                   m_ref, winr_ref, wini_ref, woutr_ref, wouti_ref, a16r_ref, a16i_ref):
    d = pl.program_id(0)
    lr = lr_ref[...]
    li = li_ref[...]
    dt = jnp.exp(ls_ref[...])

    def apow(kk):
        mag = jnp.exp(lr * dt * kk)
        ang = li * dt * kk
        return mag * jnp.cos(ang), mag * jnp.sin(ang)

    def cmul(xr, xi, yr, yi):
        return xr * yr - xi * yi, xr * yi + xi * yr

    ar, ai = apow(1.0)
    den = lr * lr + li * li
    nr, ni = ar - 1.0, ai
    zr = (nr * lr + ni * li) / den
    zi = (ni * lr - nr * li) / den
    bbr, bbi = cmul(zr, zi, btr_ref[...], bti_ref[...])
    cr, ci = cr_ref[...], ci_ref[...]

    t16 = lax.broadcasted_iota(jnp.int32, (S5_CHUNK, 1), 0)
    tv16 = jnp.where(d == 0, t16, S5_CHUNK - 1 - t16).astype(F32)
    t_row = lax.broadcasted_iota(jnp.int32, (S5_ROW, 1), 0) // S5_GROUP
    t_col = lax.broadcasted_iota(jnp.int32, (1, S5_ROW), 1) // S5_GROUP
    tv_row = jnp.where(d == 0, t_row, S5_CHUNK - 1 - t_row)
    tv_col = jnp.where(d == 0, t_col, S5_CHUNK - 1 - t_col)
    tile = lambda x: jnp.concatenate([x] * S5_CHUNK, axis=0)
    spread = (lax.broadcasted_iota(jnp.int32, (S5_ROW, S5_CHUNK), 0) // S5_GROUP
              == lax.broadcasted_iota(jnp.int32, (S5_ROW, S5_CHUNK), 1)).astype(BF16)
    rep = lambda x: _mm_exact_lhs(spread, x)
    pos_r, pos_i = apow(tv16)
    neg_r, neg_i = apow(-tv16)
    xr, xi = cmul(tile(bbr), tile(bbi), rep(neg_r), rep(neg_i))
    yr, yi = cmul(tile(cr), tile(ci), rep(pos_r), rep(pos_i))
    nt = lambda p, q: lax.dot_general(p, q, (((1,), (1,)), ((), ())), precision=HIGHEST,
                                      preferred_element_type=F32)
    m = nt(xr, yr) - nt(xi, yi)
    m_ref[...] = jnp.where(tv_col >= tv_row, m, 0.0)

    p15r, p15i = apow(S5_CHUNK - 1.0)
    winr_ref[...], wini_ref[...] = cmul(xr, xi, p15r, p15i)
    wr, wi = cmul(yr, yi, ar, ai)
    woutr_ref[...] = wr
    wouti_ref[...] = -wi
    a16r, a16i = apow(float(S5_CHUNK))
    a16r_ref[...] = a16r
    a16i_ref[...] = a16i


def _s5_ops_call(lam_re, lam_im, log_step, bt_re, bt_im, c_re, c_im):
    g, p, cg = S5_GROUPS, S5_STATE, S5_GROUP
    ls = jnp.broadcast_to(log_step[:, :, None, None], (2, g, 1, p))
    lam_spec = pl.BlockSpec((None, None, 1, p), lambda d, gi: (d, gi, 0, 0))
    par_spec = pl.BlockSpec((None, cg, p), lambda d, gi: (gi, 0, 0))
    out = lambda *shape: pl.BlockSpec((None, None) + shape, lambda d, gi: (d, gi, 0, 0))
    return pl.pallas_call(
        _s5_ops_kernel,
        grid=(2, g),
        in_specs=[lam_spec, lam_spec, lam_spec, par_spec, par_spec, par_spec, par_spec],
        out_specs=[out(S5_ROW, S5_ROW), out(S5_ROW, p), out(S5_ROW, p), out(S5_ROW, p), out(S5_ROW, p),
                   out(1, p), out(1, p)],
        out_shape=[jax.ShapeDtypeStruct((2, g, S5_ROW, S5_ROW), F32)]
        + [jax.ShapeDtypeStruct((2, g, S5_ROW, p), F32)] * 4
        + [jax.ShapeDtypeStruct((2, g, 1, p), F32)] * 2,
        compiler_params=_params("parallel", "parallel"),
        name="s5_ops",
    )(lam_re.reshape(2, g, 1, p), lam_im.reshape(2, g, 1, p), ls, bt_re, bt_im, c_re, c_im)


def _regroup_kernel(to_groups, x_ref, o_ref):
    units = 128 // S5_GROUP
    cols_per_t = S5_WIDTH // 128
    cols_per_g = S5_ROW // 128
    unit = lax.broadcasted_iota(jnp.int32, (x_ref.shape[0], 128), 1) // S5_GROUP
    for j in range(x_ref.shape[1] // 128):
        acc = None
        for p in range(units):
            if to_groups:
                g, t = j // cols_per_g, units * (j % cols_per_g) + p
                src_col, src_unit = t * cols_per_t + g // units, g % units
            else:
                t, g = j // cols_per_t, units * (j % cols_per_t) + p
                src_col, src_unit = g * cols_per_g + t // units, t % units
            piece = x_ref[:, src_col * 128:(src_col + 1) * 128]
            shift = ((p - src_unit) * S5_GROUP) % 128
            if shift:
                piece = pltpu.roll(piece, shift, axis=1)
            acc = piece if acc is None else jnp.where(unit == p, piece, acc)
        o_ref[:, j * 128:(j + 1) * 128] = acc.astype(o_ref.dtype)


def _regroup_call(x, to_groups, dtype):
    b, nc, width = x.shape
    rows = min(64, nc)
    spec = pl.BlockSpec((None, rows, width), lambda b, i: (b, i, 0))
    return pl.pallas_call(
        functools.partial(_regroup_kernel, to_groups),
        grid=(b, nc // rows),
        in_specs=[spec],
        out_specs=spec,
        out_shape=jax.ShapeDtypeStruct(x.shape, dtype),
        compiler_params=_params("parallel", "parallel"),
        name="s5_to_groups" if to_groups else "s5_to_tokens",
    )(x)


def _s5_in_kernel(u_ref, wr_ref, wi_ref, sr_ref, si_ref):
    dot = functools.partial(jnp.dot, preferred_element_type=F32)
    u0, u1 = u_ref[:, :S5_ROW], u_ref[:, S5_ROW:]
    sr_ref[...] = dot(u0, wr_ref[0]) + dot(u1, wr_ref[1])
    si_ref[...] = dot(u0, wi_ref[0]) + dot(u1, wi_ref[1])


def _s5_in_call(ug, win_r, win_i):
    b, nc, _ = ug.shape
    g = S5_GROUPS
    w_spec = pl.BlockSpec((None, 2, S5_ROW, 128), lambda d, b, gp: (d, gp, 0, 0))
    o_spec = pl.BlockSpec((None, None, nc, 128), lambda d, b, gp: (d, b, 0, gp))
    return pl.pallas_call(
        _s5_in_kernel,
        grid=(2, b, g // 2),
        in_specs=[pl.BlockSpec((None, nc, 2 * S5_ROW), lambda d, b, gp: (b, 0, gp)), w_spec, w_spec],
        out_specs=[o_spec, o_spec],
        out_shape=[jax.ShapeDtypeStruct((2, b, nc, S5_LANES), F32)] * 2,
        compiler_params=_params("parallel", "parallel", "parallel"),
        name="s5_in",
    )(ug, win_r, win_i)


def _s5_scan_kernel(nc, sr_ref, si_ref, ar_ref, ai_ref, h0r_ref, h0i_ref, hr_ref, hi_ref, fr_ref, fi_ref):
    coef = [(ar_ref[d], ai_ref[d]) for d in range(2)]

    def body(i, carry):
        out = []
        for d in range(2):
            hr, hi = carry[d]
            ar, ai = coef[d]
            n = i if d == 0 else nc - 1 - i
            hr_ref[d, pl.ds(n, 1), :] = hr
            hi_ref[d, pl.ds(n, 1), :] = hi
            sr = sr_ref[d, pl.ds(n, 1), :]
            si = si_ref[d, pl.ds(n, 1), :]
            out.append((ar * hr - ai * hi + sr, ar * hi + ai * hr + si))
        return tuple(out)

    fin = lax.fori_loop(0, nc, body, tuple((h0r_ref[d], h0i_ref[d]) for d in range(2)), unroll=4)
    for d in range(2):
        fr_ref[d] = fin[d][0]
        fi_ref[d] = fin[d][1]


def _s5_scan_call(s_re, s_im, a16_re, a16_im, h0_re, h0_im):
    _, b, nc, _ = s_re.shape
    tl = 512
    big = pl.BlockSpec((2, None, nc, tl), lambda b, j: (0, b, 0, j))
    a_spec = pl.BlockSpec((2, 1, tl), lambda b, j: (0, 0, j))
    st_spec = pl.BlockSpec((2, None, 1, tl), lambda b, j: (0, b, 0, j))
    return pl.pallas_call(
        functools.partial(_s5_scan_kernel, nc),
        grid=(b, S5_LANES // tl),
        in_specs=[big, big, a_spec, a_spec, st_spec, st_spec],
        out_specs=[big, big, st_spec, st_spec],
        out_shape=[jax.ShapeDtypeStruct(s_re.shape, F32)] * 2
        + [jax.ShapeDtypeStruct((2, b, 1, S5_LANES), F32)] * 2,
        compiler_params=_params("parallel", "parallel"),
        name="s5_scan",
    )(s_re, s_im, a16_re, a16_im, h0_re, h0_im)


def _s5_out_kernel(u_ref, m_ref, hr_ref, hi_ref, wr_ref, wi_ref, y_ref):
    u = u_ref[...]
    acc = None
    for d in range(2):
        y = (jnp.dot(u, m_ref[d], preferred_element_type=F32)
             + _mm(hr_ref[d], wr_ref[d]) + _mm(hi_ref[d], wi_ref[d]))
        acc = y if acc is None else acc + y
    y_ref[...] = acc


def _s5_out_call(ug, m, h_re, h_im, wout_r, wout_i):
    b, nc, _ = ug.shape
    g = S5_GROUPS
    h_spec = pl.BlockSpec((2, None, nc, 128), lambda b, gi: (0, b, 0, gi // 2))
    w_spec = pl.BlockSpec((2, None, 128, S5_ROW), lambda b, gi: (0, gi, 0, 0))
    u_spec = pl.BlockSpec((None, nc, S5_ROW), lambda b, gi: (b, 0, gi))
    return pl.pallas_call(
        _s5_out_kernel,
        grid=(b, g),
        in_specs=[u_spec,
                  pl.BlockSpec((2, None, S5_ROW, S5_ROW), lambda b, gi: (0, gi, 0, 0)),
                  h_spec, h_spec, w_spec, w_spec],
        out_specs=u_spec,
        out_shape=jax.ShapeDtypeStruct(ug.shape, F32),
        compiler_params=_params("parallel", "parallel"),
        name="s5_out",
    )(ug, m, h_re, h_im, wout_r, wout_i)


def _s5_prepare_ops(lam_re, lam_im, log_step, b_re, b_im, c_re, c_im):
    m, win_r, win_i, wout_r, wout_i, a16_r, a16_i = _s5_ops_call(
        lam_re, lam_im, log_step, jnp.swapaxes(b_re, 1, 2), jnp.swapaxes(b_im, 1, 2), c_re, c_im)
    p = S5_STATE
    odd = (jnp.arange(S5_GROUPS) % 2 == 1)[None, :, None, None]

    def pad_in(w):
        z = jnp.zeros_like(w)
        return jnp.where(odd, jnp.concatenate([z, w], -1), jnp.concatenate([w, z], -1)).astype(BF16)

    def pad_out(w):
        wt = jnp.swapaxes(w, 2, 3)
        z = jnp.zeros_like(wt)
        return jnp.where(odd, jnp.concatenate([z, wt], 2), jnp.concatenate([wt, z], 2)).astype(BF16)

    return dict(m=m.astype(BF16), win_r=pad_in(win_r), win_i=pad_in(win_i),
                wout_r=pad_out(wout_r), wout_i=pad_out(wout_i),
                a16_r=a16_r.reshape(2, 1, S5_LANES), a16_i=a16_i.reshape(2, 1, S5_LANES))


def _s5_mix(u, ops, h0_re, h0_im):
    b, seq, _ = u.shape
    nc = seq // S5_CHUNK
    ug = _regroup_call(u.reshape(b, nc, S5_CHUNK * S5_WIDTH), True, BF16)
    s_re, s_im = _s5_in_call(ug, ops["win_r"], ops["win_i"])
    h_re, h_im, f_re, f_im = _s5_scan_call(s_re, s_im, ops["a16_r"], ops["a16_i"], h0_re, h0_im)
    y = _s5_out_call(ug, ops["m"], h_re, h_im, ops["wout_r"], ops["wout_i"])
    y = _regroup_call(y, False, F32).reshape(b, seq, S5_WIDTH)
    return y, f_re, f_im


def _conv_kernel(rows, width, n_hor, has_ver, nt, *refs):
    if has_ver:
        (xh_ref, vp_ref, vc_ref, vn_ref, wh_ref, wv_ref, b_ref, g_ref, be_ref,
         o_ref, hs_scr, vs_scr) = refs
    else:
        xh_ref, wh_ref, b_ref, g_ref, be_ref, o_ref, hs_scr = refs
    i = pl.program_id(1)
    tokens = rows * width
    slot = width + 2 * CONV_PAD
    half = CONV_K // 2

    hs_scr[...] = jnp.zeros_like(hs_scr)
    for r in range(rows):
        hs_scr[r, pl.ds(CONV_PAD, width), :] = xh_ref[pl.ds(r * width, width), :]
    wh = wh_ref[...]
    acc_h = jnp.zeros((tokens, n_hor), F32)
    for tap in range(CONV_K):
        win = hs_scr[:, pl.ds(CONV_PAD - half + tap, width), :].reshape(tokens, n_hor)
        acc_h = acc_h + win * wh[tap:tap + 1, :]

    if has_ver:
        vs_scr[pl.ds(0, tokens), :] = jnp.where(i == 0, 0.0, vp_ref[...])
        vs_scr[pl.ds(tokens, tokens), :] = vc_ref[...]
        vs_scr[pl.ds(2 * tokens, tokens), :] = jnp.where(i == nt - 1, 0.0, vn_ref[...])
        wv = wv_ref[...]
        acc_v = jnp.zeros((tokens, CONV_WIDTH - n_hor), F32)
        for tap in range(CONV_K):
            acc_v = acc_v + vs_scr[pl.ds(tokens + (tap - half) * width, tokens), :] * wv[tap:tap + 1, :]
        x = jnp.concatenate([acc_h, acc_v], axis=1)
    else:
        x = acc_h
    x = x + b_ref[...]
    xc = x - jnp.mean(x, axis=-1, keepdims=True)
    y = xc * lax.rsqrt(jnp.mean(xc * xc, axis=-1, keepdims=True) + EPS) * g_ref[...] + be_ref[...]
    o_ref[...] = _silu(y)


def _conv_call(xc, dw_w, dw_b, ln_g, ln_b, grid_rows):
    b, seq, ch = xc.shape
    vec = pl.BlockSpec((1, ch), lambda b, i: (0, 0))
    if grid_rows > 0:
        width, rows, n_hor = GRID_W, 16, ch // 2
        tokens = rows * width
        nt = seq // tokens
        half_spec = lambda f: pl.BlockSpec((None, tokens, n_hor), f)
        in_specs = [half_spec(lambda b, i: (b, i, 0)),
                    half_spec(lambda b, i: (b, jnp.maximum(i - 1, 0), 1)),
                    half_spec(lambda b, i: (b, i, 1)),
                    half_spec(lambda b, i: (b, jnp.minimum(i + 1, nt - 1), 1)),
                    pl.BlockSpec((CONV_K, n_hor), lambda b, i: (0, 0)),
                    pl.BlockSpec((CONV_K, n_hor), lambda b, i: (0, 1)),
                    vec, vec, vec]
        args = [xc, xc, xc, xc, dw_w, dw_w, dw_b, ln_g, ln_b]
        scratch = [pltpu.VMEM((rows, width + 2 * CONV_PAD, n_hor), F32),
                   pltpu.VMEM((3 * tokens, ch - n_hor), F32)]
        has_ver = True
    else:
        width, rows, n_hor = seq, 1, ch
        tokens = seq
        nt = 1
        in_specs = [pl.BlockSpec((None, tokens, ch), lambda b, i: (b, 0, 0)),
                    pl.BlockSpec((CONV_K, ch), lambda b, i: (0, 0)),
                    vec, vec, vec]
        args = [xc, dw_w, dw_b, ln_g, ln_b]
        scratch = [pltpu.VMEM((rows, width + 2 * CONV_PAD, n_hor), F32)]
        has_ver = False
    return pl.pallas_call(
        functools.partial(_conv_kernel, rows, width, n_hor, has_ver, nt),
        grid=(b, nt),
        in_specs=in_specs,
        out_specs=pl.BlockSpec((None, tokens, ch), lambda b, i: (b, i, 0)),
        out_shape=jax.ShapeDtypeStruct((b, seq, ch), F32),
        scratch_shapes=scratch,
        compiler_params=_params("parallel", "parallel"),
        name="conformer_conv",
    )(*args)


def _lb_kernel(x_ref, o_ref):
    x = x_ref[...]
    n = x.shape[0]
    rows = [x[r:r + 1, :] for r in range(n)]
    mx = functools.reduce(jnp.maximum, rows)
    ex = [jnp.exp(r - mx) for r in rows]
    tot = functools.reduce(lambda p, q: p + q, ex)
    run = None
    for r in range(n):
        run = ex[r] / tot if run is None else run + ex[r] / tot
        o_ref[pl.ds(r, 1), :] = run


def _lb_call(logits):
    return pl.pallas_call(_lb_kernel, out_shape=jax.ShapeDtypeStruct(logits.shape, F32), name="hgrn_lb")(logits)


def _hgrn_kernel(rev, t_blk, nb, q_ref, f_ref, v_ref, lb_ref, s0_ref, o_ref, sfin_ref,
                 st_scr, kv_scr, sall_scr):
    i = pl.program_id(1)
    c = HGRN_CHUNK
    nch = t_blk // c
    hh = HGRN_HEADS

    @pl.when(i == 0)
    def _():
        st_scr[...] = s0_ref[...]

    lb = lb_ref[...]
    f = lb + (1.0 - lb) * jax.nn.sigmoid(f_ref[...])
    k = 1.0 - f
    log_f = jnp.log(f)
    row = lax.broadcasted_iota(jnp.int32, (t_blk, t_blk), 0)
    col = lax.broadcasted_iota(jnp.int32, (t_blk, t_blk), 1)
    same = _same_block(row, col, c)
    incl = same & ((row <= col) if rev else (row >= col))
    bc = _mm_exact_lhs(incl.astype(BF16), log_f)
    last = 0 if rev else c - 1
    bc3 = bc.reshape(nch, c, bc.shape[1])
    tot = jnp.broadcast_to(bc3[:, last:last + 1, :], bc3.shape).reshape(bc.shape)
    q_in = q_ref[...] * jnp.exp(bc)
    k_in = k * jnp.exp(-bc)
    k_out = k * jnp.exp(tot - bc)
    cd = jnp.exp(tot)
    v = v_ref[...]
    order = range(nch - 1, -1, -1) if rev else range(nch)
    heads = range(hh)
    ls = [slice(hd * HGRN_DIM, (hd + 1) * HGRN_DIM) for hd in heads]
    attn = [jnp.where(incl, _mm_nt(q_in[:, ls[hd]], k_in[:, ls[hd]]), 0.0) for hd in heads]
    o_intra = [_mm(attn[hd], v[:, ls[hd]]) for hd in heads]
    for n in range(nch):
        rows = slice(n * c, (n + 1) * c)
        for hd in heads:
            kv_scr[hd, n] = _mm_tn(v[rows, ls[hd]], k_out[rows, ls[hd]])
    st = [st_scr[hd] for hd in heads]
    for n in order:
        for hd in heads:
            sall_scr[hd, n] = st[hd]
            st[hd] = st[hd] * cd[n * c:n * c + 1, ls[hd]] + kv_scr[hd, n]
    for hd in heads:
        st_scr[hd] = st[hd]
    for n in range(nch):
        rows = slice(n * c, (n + 1) * c)
        for hd in heads:
            o_ref[pl.ds(n * c, c), pl.ds(hd * HGRN_DIM, HGRN_DIM)] = (
                _mm_nt(q_in[rows, ls[hd]], sall_scr[hd, n]) + o_intra[hd][rows])

    @pl.when(i == nb - 1)
    def _():
        sfin_ref[...] = st_scr[...]


def _hgrn_call(q, fgate, v, lb_row, s0, rev):
    b, seq, _ = q.shape
    t_blk = min(256, seq)
    nb = seq // t_blk
    hh = HGRN_HEADS

    def blk(i):
        return (nb - 1 - i) if rev else i

    tok = pl.BlockSpec((None, t_blk, HGRN_WIDTH), lambda b, i: (b, blk(i), 0))
    state_spec = pl.BlockSpec((None, hh, HGRN_DIM, HGRN_DIM), lambda b, i: (b, 0, 0, 0))
    chunk_states = pltpu.VMEM((hh, t_blk // HGRN_CHUNK, HGRN_DIM, HGRN_DIM), F32)
    return pl.pallas_call(
        functools.partial(_hgrn_kernel, rev, t_blk, nb),
        grid=(b, nb),
        in_specs=[tok, tok, tok, pl.BlockSpec((1, HGRN_WIDTH), lambda b, i: (0, 0)), state_spec],
        out_specs=[tok, state_spec],
        out_shape=[jax.ShapeDtypeStruct((b, seq, HGRN_WIDTH), F32),
                   jax.ShapeDtypeStruct((b, hh, HGRN_DIM, HGRN_DIM), F32)],
        scratch_shapes=[pltpu.VMEM((hh, HGRN_DIM, HGRN_DIM), F32), chunk_states, chunk_states],
        compiler_params=_params("parallel", "arbitrary"),
        name="hgrn_bwd" if rev else "hgrn_fwd",
    )(q, fgate, v, lb_row, s0)


def _head_norm(o, gate, g):
    outs = []
    for hd in range(o.shape[1] // 128):
        x = o[:, hd * 128:(hd + 1) * 128]
        x = x * lax.rsqrt(jnp.mean(x * x, axis=-1, keepdims=True) + EPS) * g
        outs.append(x * _silu(gate[:, hd * 128:(hd + 1) * 128]))
    return jnp.concatenate(outs, axis=1)


def _out_ab_kernel(h_ref, gt_ref, of_ref, ob_ref, z_ref, y5_ref, u_ref, ng_ref, dsk_ref, glw_ref, glb_ref,
                   wout_ref, o_ref):
    a = _head_norm(of_ref[...] + ob_ref[...], z_ref[...], ng_ref[...])
    y = y5_ref[...] + dsk_ref[...] * u_ref[...]
    y = 0.5 * y * (1.0 + jnp.tanh(math.sqrt(2.0 / math.pi) * (y + 0.044715 * (y * y * y))))
    bmix = y * jax.nn.sigmoid(_mm(y, glw_ref[...]) + glb_ref[...])
    mix = _mm(jnp.concatenate([a, bmix], axis=1), wout_ref[...])
    o_ref[...] = h_ref[...] + gt_ref[...] * mix


def _out_cd_kernel(h_ref, gt_ref, c_ref, of_ref, ob_ref, g_ref, ng_ref, wout_ref, o_ref):
    dmix = _head_norm(of_ref[...] + ob_ref[...], g_ref[...], ng_ref[...])
    mix = _mm(jnp.concatenate([c_ref[...], dmix], axis=1), wout_ref[...])
    o_ref[...] = h_ref[...] + gt_ref[...] * mix


def _out_call(kern, name, h, mod, layer, ctx, toks, consts):
    b, seq, _ = h.shape
    tm = min(512, seq)
    in_specs = [pl.BlockSpec((None, tm, D_MODEL), lambda b, i: (b, i, 0)), _mod_spec(layer, 1, 2, ctx)]
    in_specs += [pl.BlockSpec((None, tm, t.shape[2]), lambda b, i: (b, i, 0)) for t in toks]
    in_specs += [pl.BlockSpec(cst.shape, lambda b, i: (0, 0), pipeline_mode=pl.Buffered(1)) for cst in consts]
    return pl.pallas_call(
        kern,
        grid=(b, seq // tm),
        in_specs=in_specs,
        out_specs=pl.BlockSpec((None, tm, D_MODEL), lambda b, i: (b, i, 0)),
        out_shape=jax.ShapeDtypeStruct(h.shape, F32),
        compiler_params=_params("parallel", "parallel"),
        name=name,
    )(h, mod, *toks, *consts)


def _mixer_ab(h, hc, mod, layer, need_ctx, prm):
    ng = prm["norm_g1"]
    outs = {}
    gdn_state = [jnp.zeros((h.shape[0], GDN_HEADS, GDN_DIM, GDN_DIM), F32)] * 2
    s5_state = [jnp.zeros((2, h.shape[0], 1, S5_LANES), F32)] * 2
    for ctx, x in ((True, hc), (False, h)):
        q, k, v, z, u, gates = _inproj_ab_call(x, mod, layer, ctx, ng, prm["w_in"], prm["conv_w"])
        o_f, sf = _gdn_call(q, k, v, gates, prm["alog"], prm["dtb"], gdn_state[0], rev=False)
        o_b, sb = _gdn_call(q, k, v, gates, prm["alog"], prm["dtb"], gdn_state[1], rev=True)
        y5, f_re, f_im = _s5_mix(u, prm["s5"], s5_state[0], s5_state[1])
        gdn_state = [sf, sb]
        s5_state = [f_re, f_im]
        if ctx and not need_ctx:
            continue
        outs[ctx] = _out_call(_out_ab_kernel, "out_ab", x, mod, layer, ctx, [o_f, o_b, z, y5, u],
                              [prm["gdn_g"], prm["s5_d"], prm["glu_w"], prm["glu_b"], prm["w_out"]])
    return outs[False], outs.get(True)


def _mixer_cd(h, hc, mod, layer, need_ctx, prm, grid_rows):
    ng = prm["norm_g1"]
    outs = {}
    state = [jnp.zeros((h.shape[0], HGRN_HEADS, HGRN_DIM, HGRN_DIM), F32)] * 2
    for ctx, x in ((True, hc), (False, h)):
        xc, q, f_f, f_b, iv, g = _inproj_call(x, mod, layer, ctx, ng, prm["w_in"], prm["pieces"], glu=True)
        o_f, sf = _hgrn_call(q, f_f, iv, prm["lb"], state[0], rev=False)
        o_b, sb = _hgrn_call(q, f_b, iv, prm["lb"], state[1], rev=True)
        state = [sf, sb]
        if ctx and not need_ctx:
            continue
        cmix = _conv_call(xc, prm["dw_w"], prm["dw_b"], prm["ln_g"], prm["ln_b"], 0 if ctx else grid_rows)
        outs[ctx] = _out_call(_out_cd_kernel, "out_cd", x, mod, layer, ctx, [cmix, o_f, o_b, g],
                              [prm["hgrn_g"], prm["w_out"]])
    return outs[False], outs.get(True)


def kernel(x, c, ctx, c_ctx, ada_w, ada_b, norm_g, ffn_w_up, ffn_w_down, ab_w_in, ab_w_out, gdn_conv_w, gdn_a_log, gdn_dt_bias, gdn_norm_g, s5_lambda_re, s5_lambda_im, s5_log_step, s5_b_re, s5_b_im, s5_c_re, s5_c_im, s5_d, s5_glu_w, s5_glu_b, cd_w_in, cd_w_out, conv_dw_w, conv_dw_b, conv_ln_g, conv_ln_b, hgrn_lb_logits, hgrn_norm_g, final_norm_g):
    depth = ada_w.shape[0]
    batch = x.shape[0]
    grid_rows = x.shape[1] // GRID_W
    assert batch <= 2 and x.shape[1] % 1024 == 0 and ctx.shape[1] % 256 == 0

    cvec = jnp.zeros((8, D_MODEL), F32).at[:batch].set(c).at[2].set(c_ctx)
    mod = _ada_call(cvec, ada_w, ada_b)
    lb_all = _lb_call(hgrn_lb_logits)
    row = lambda v: v.reshape(1, -1)
    gw, kw = GDN_WIDTH, HGRN_WIDTH

    w_up = ffn_w_up.astype(BF16)
    w_down = ffn_w_down.astype(BF16)
    h, hc = x, ctx
    for l in range(depth):
        last = l == depth - 1
        h = _ffn_call(h, mod, l, 0, False, row(norm_g[l, 0]), w_up, w_down, (l, 0))
        hc = _ffn_call(hc, mod, l, 0, True, row(norm_g[l, 0]), w_up, w_down, (l, 0))
        if l % 2 == 0:
            e = l // 2
            w = ab_w_in[e]
            ng4 = 4 * GDN_HEADS
            w = jnp.concatenate([w[:, :4 * gw], w[:, 4 * gw + ng4:], w[:, 4 * gw:4 * gw + ng4],
                                 jnp.zeros((D_MODEL, 128 - ng4), F32)], axis=1).astype(BF16)
            pad8 = lambda v: jnp.zeros((1, 128), F32).at[0, :2 * GDN_HEADS].set(v.reshape(-1))
            prm = dict(
                norm_g1=row(norm_g[l, 1]),
                w_in=w, conv_w=gdn_conv_w[e], alog=pad8(gdn_a_log[e]), dtb=pad8(gdn_dt_bias[e]),
                gdn_g=row(gdn_norm_g[e]),
                s5=_s5_prepare_ops(s5_lambda_re[e], s5_lambda_im[e], s5_log_step[e], s5_b_re[e], s5_b_im[e],
                                   s5_c_re[e], s5_c_im[e]),
                s5_d=row(s5_d[e]), glu_w=s5_glu_w[e].astype(BF16), glu_b=row(s5_glu_b[e]),
                w_out=ab_w_out[e].astype(BF16))
            h, hc_new = _mixer_ab(h, hc, mod, l, not last, prm)
        else:
            o = l // 2
            cw = CONV_WIDTH
            prm = dict(
                norm_g1=row(norm_g[l, 1]),
                w_in=cd_w_in[o].astype(BF16),
                pieces=[(k * cw, cw) for k in range(2)] + [(2 * cw + k * kw, kw) for k in range(5)],
                lb=lb_all[o:o + 1], hgrn_g=row(hgrn_norm_g[o]),
                dw_w=conv_dw_w[o], dw_b=row(conv_dw_b[o]), ln_g=row(conv_ln_g[o]), ln_b=row(conv_ln_b[o]),
                w_out=cd_w_out[o].astype(BF16))
            h, hc_new = _mixer_cd(h, hc, mod, l, not last, prm, grid_rows)
        h = _ffn_call(h, mod, l, 2, False, row(norm_g[l, 2]), w_up, w_down, (l, 1),
                      final_g=row(final_norm_g) if last else None)
        if not last:
            hc = _ffn_call(hc_new, mod, l, 2, True, row(norm_g[l, 2]), w_up, w_down, (l, 1))
    return h
```

```python
import functools
import math

import jax
import jax.numpy as jnp
from jax import lax
from jax.experimental import pallas as pl
from jax.experimental.pallas import tpu as pltpu

F32 = jnp.float32
BF16 = jnp.bfloat16
HIGHEST = lax.Precision.HIGHEST
EPS = 1e-6

D_MODEL = 1024
GRID_W = 64
FFN_DIM = 2816

GDN_HEADS = 4
GDN_DIM = 128
GDN_WIDTH = GDN_HEADS * GDN_DIM
GDN_CONV = 5
GDN_CHUNK = 256

S5_WIDTH = 512
S5_GROUP = 16
S5_GROUPS = 32
S5_STATE = 64
S5_CHUNK = 16
S5_ROW = S5_CHUNK * S5_GROUP
S5_LANES = S5_GROUPS * S5_STATE

CONV_WIDTH = 512
CONV_K = 31
CONV_PAD = 16

HGRN_HEADS = 4
HGRN_DIM = 128
HGRN_WIDTH = HGRN_HEADS * HGRN_DIM
HGRN_CHUNK = 16

V7X_VMEM_LIMIT = 48 * 1024 * 1024


def _params(*sem):
    return pltpu.CompilerParams(dimension_semantics=sem, vmem_limit_bytes=V7X_VMEM_LIMIT)


def _silu(x):
    return x * jax.nn.sigmoid(x)


def _mm(a, b):
    return jnp.dot(a.astype(BF16), b.astype(BF16), preferred_element_type=F32)


def _mm_nt(a, b):
    return lax.dot_general(a.astype(BF16), b.astype(BF16), (((1,), (1,)), ((), ())),
                           preferred_element_type=F32)


def _mm_tn(a, b):
    return lax.dot_general(a.astype(BF16), b.astype(BF16), (((0,), (0,)), ((), ())),
                           preferred_element_type=F32)


def _split3(x):
    hi = x.astype(BF16)
    r1 = x - hi.astype(F32)
    mid = r1.astype(BF16)
    lo = (r1 - mid.astype(F32)).astype(BF16)
    return hi, mid, lo


def _mm_exact_lhs(a_bf16, x):
    return jnp.dot(jnp.concatenate([a_bf16] * 3, axis=1), jnp.concatenate(_split3(x), axis=0),
                   preferred_element_type=F32)


def _mm_exact_rhs(x, b_bf16):
    return jnp.dot(jnp.concatenate(_split3(x), axis=1), jnp.concatenate([b_bf16] * 3, axis=0),
                   preferred_element_type=F32)


def _mm_x3(a, b):
    ah = a.astype(BF16)
    al = (a - ah.astype(F32)).astype(BF16)
    bh = b.astype(BF16)
    bl = (b - bh.astype(F32)).astype(BF16)
    return jnp.dot(jnp.concatenate([ah, ah, al], axis=1), jnp.concatenate([bh, bl, bh], axis=0),
                   preferred_element_type=F32)


def _same_block(i, j, size):
    return (i ^ j) < size


def _ada_norm(x, g, scale, shift):
    y = x * lax.rsqrt(jnp.mean(x * x, axis=-1, keepdims=True) + EPS) * g
    return y * (1.0 + scale) + shift


def _ada_kernel(c_ref, w_ref, b_ref, o_ref):
    o_ref[...] = _mm(_silu(c_ref[...]), w_ref[...]) + b_ref[...]


def _ada_call(cvec, ada_w, ada_b):
    depth = ada_w.shape[0]
    ncol = ada_w.shape[2] // D_MODEL
    out = pl.pallas_call(
        _ada_kernel,
        grid=(depth, ncol),
        in_specs=[pl.BlockSpec((8, D_MODEL), lambda l, j: (0, 0)),
                  pl.BlockSpec((None, D_MODEL, D_MODEL), lambda l, j: (l, 0, j)),
                  pl.BlockSpec((None, 1, D_MODEL), lambda l, j: (l, 0, j))],
        out_specs=pl.BlockSpec((None, None, 8, D_MODEL), lambda l, j: (l, j, 0, 0)),
        out_shape=jax.ShapeDtypeStruct((depth, ncol, 8, D_MODEL), F32),
        compiler_params=_params("arbitrary", "arbitrary"),
        name="ada_mod",
    )(cvec, ada_w, ada_b.reshape(depth, 1, -1))
    return out.reshape(depth * ncol * 8, 1, D_MODEL)


def _mod_spec(layer, sub, kind, ctx):
    base = (layer * 9 + sub * 3 + kind) * 8
    if ctx:
        return pl.BlockSpec((None, 1, D_MODEL), lambda b, *_: (base + 2, 0, 0))
    return pl.BlockSpec((None, 1, D_MODEL), lambda b, *_: (base + b, 0, 0))


def _head_norm(o, gate, g):
    outs = []
    for hd in range(o.shape[1] // 128):
        x = o[:, hd * 128:(hd + 1) * 128]
        x = x * lax.rsqrt(jnp.mean(x * x, axis=-1, keepdims=True) + EPS) * g
        outs.append(x * _silu(gate[:, hd * 128:(hd + 1) * 128]))
    return jnp.concatenate(outs, axis=1)


def _mix_ab(of_ref, ob_ref, z_ref, y5a_ref, y5b_ref, y5c_ref, y5d_ref, u_ref, ng_ref, dsk_ref, glw_ref, glb_ref,
            wout_ref):
    a = _head_norm(of_ref[...] + ob_ref[...], z_ref[...], ng_ref[...])
    y5 = jnp.concatenate([y5a_ref[...], y5b_ref[...], y5c_ref[...], y5d_ref[...]], axis=1)
    y = y5 + dsk_ref[...] * u_ref[...]
    y = 0.5 * y * (1.0 + jnp.tanh(math.sqrt(2.0 / math.pi) * (y + 0.044715 * (y * y * y))))
    bmix = y * jax.nn.sigmoid(_mm(y, glw_ref[...]) + glb_ref[...])
    return _mm(jnp.concatenate([a, bmix], axis=1), wout_ref[...])


def _mix_cd(c_ref, of_ref, ob_ref, g_ref, ng_ref, wout_ref):
    dmix = _head_norm(of_ref[...] + ob_ref[...], g_ref[...], ng_ref[...])
    return _mm(jnp.concatenate([c_ref[...], dmix], axis=1), wout_ref[...])


_MIXERS = {"ab": (_mix_ab, 13), "cd": (_mix_cd, 6)}


def _ffn_kernel(tf, final, mixer, h_ref, *refs):
    x = h_ref[...]
    if mixer is not None:
        mix_fn, n_mix = _MIXERS[mixer]
        x = x + refs[0][...] * mix_fn(*refs[1:1 + n_mix])
        refs = refs[1 + n_mix:]
    sh_ref, sc_ref, gt_ref, g_ref, wup_ref, wd_ref = refs[:6]
    if final:
        fg_ref, o_ref = refs[6:]
    else:
        (o_ref,) = refs[6:]
    xn = _ada_norm(x, g_ref[...], sc_ref[...], sh_ref[...]).astype(BF16)
    acc = None
    for j in range(FFN_DIM // tf):
        gate = jnp.dot(xn, wup_ref[:, j * tf:(j + 1) * tf], preferred_element_type=F32)
        up = jnp.dot(xn, wup_ref[:, FFN_DIM + j * tf:FFN_DIM + (j + 1) * tf], preferred_element_type=F32)
        act = (_silu(gate) * up).astype(BF16)
        part = jnp.dot(act, wd_ref[j * tf:(j + 1) * tf, :], preferred_element_type=F32)
        acc = part if acc is None else acc + part
    y = x + 0.5 * gt_ref[...] * acc
    if final:
        y = y * lax.rsqrt(jnp.mean(y * y, axis=-1, keepdims=True) + EPS) * fg_ref[...]
    o_ref[...] = y


def _ffn_call(h, mod, layer, sub, ctx, norm_g_row, w_up, w_down, which, final_g=None, mixer=None):
    b, seq, _ = h.shape
    tm = min(512, seq)
    tf = 256
    final = final_g is not None
    resident = dict(pipeline_mode=pl.Buffered(1))
    in_specs = [pl.BlockSpec((None, tm, D_MODEL), lambda b, i: (b, i, 0))]
    args = [h]
    if mixer is not None:
        kind, toks, consts = mixer
        in_specs.append(_mod_spec(layer, 1, 2, ctx))
        in_specs += [pl.BlockSpec((None, tm, t.shape[2]), lambda b, i: (b, i, 0)) for t in toks]
        in_specs += [pl.BlockSpec(cst.shape, lambda b, i: (0, 0), **resident) for cst in consts]
        args += [mod, *toks, *consts]
        assert 1 + len(toks) + len(consts) == 1 + _MIXERS[kind][1]
    in_specs += [
        _mod_spec(layer, sub, 0, ctx), _mod_spec(layer, sub, 1, ctx), _mod_spec(layer, sub, 2, ctx),
        pl.BlockSpec((1, D_MODEL), lambda b, i: (0, 0)),
        pl.BlockSpec((None, None, D_MODEL, 2 * FFN_DIM), lambda b, i: which + (0, 0), **resident),
        pl.BlockSpec((None, None, FFN_DIM, D_MODEL), lambda b, i: which + (0, 0), **resident),
    ]
    args += [mod, mod, mod, norm_g_row, w_up, w_down]
    if final:
        in_specs.append(pl.BlockSpec((1, D_MODEL), lambda b, i: (0, 0)))
        args.append(final_g)
    return pl.pallas_call(
        functools.partial(_ffn_kernel, tf, final, None if mixer is None else mixer[0]),
        grid=(b, seq // tm),
        in_specs=in_specs,
        out_specs=pl.BlockSpec((None, tm, D_MODEL), lambda b, i: (b, i, 0)),
        out_shape=jax.ShapeDtypeStruct(h.shape, F32),
        compiler_params=_params("parallel", "parallel"),
        name="ffn",
    )(*args)


def _inproj_kernel(nw, glu, h_ref, sh_ref, sc_ref, g_ref, *refs):
    w_refs, o_refs = refs[:nw], refs[nw:]
    xn = _ada_norm(h_ref[...], g_ref[...], sc_ref[...], sh_ref[...]).astype(BF16)
    outs = [jnp.dot(xn, w[...], preferred_element_type=F32) for w in w_refs]
    if glu:
        outs = [outs[0] * jax.nn.sigmoid(outs[1])] + outs[2:]
    for o_ref, val in zip(o_refs, outs):
        o_ref[...] = val


def _inproj_call(h, mod, layer, ctx, norm_g_row, w, pieces, glu):
    b, seq, _ = h.shape
    tm = min(512, seq)
    widths = [wd for _, wd in pieces]
    out_widths = widths[1:] if glu else widths
    in_specs = [pl.BlockSpec((None, tm, D_MODEL), lambda b, i: (b, i, 0)),
                _mod_spec(layer, 1, 0, ctx), _mod_spec(layer, 1, 1, ctx),
                pl.BlockSpec((1, D_MODEL), lambda b, i: (0, 0))]
    for off, wd in pieces:
        assert off % wd == 0
        in_specs.append(pl.BlockSpec((D_MODEL, wd), functools.partial(lambda b, i, blk: (0, blk), blk=off // wd),
                                     pipeline_mode=pl.Buffered(1)))
    return pl.pallas_call(
        functools.partial(_inproj_kernel, len(pieces), glu),
        grid=(b, seq // tm),
        in_specs=in_specs,
        out_specs=[pl.BlockSpec((None, tm, wd), lambda b, i: (b, i, 0)) for wd in out_widths],
        out_shape=[jax.ShapeDtypeStruct((b, seq, wd), F32) for wd in out_widths],
        compiler_params=_params("parallel", "parallel"),
        name="inproj",
    )(h, mod, mod, norm_g_row, *([w] * len(pieces)))


def _inproj_ab_kernel(tm, nt, h_ref, hp_ref, hn_ref, sh_ref, sc_ref, g_ref, wqkv_ref, wz_ref, wu_ref, wg_ref, cw_ref,
                      q_ref, k_ref, v_ref, z_ref, u_ref, gates_ref, pad_scr):
    i = pl.program_id(1)
    norm = lambda x: _ada_norm(x, g_ref[...], sc_ref[...], sh_ref[...]).astype(BF16)
    dot = functools.partial(jnp.dot, preferred_element_type=F32)
    xn = norm(h_ref[...])
    z_ref[...] = dot(xn, wz_ref[...])
    u_ref[...] = dot(xn, wu_ref[...])
    gates_ref[...] = dot(xn, wg_ref[...])
    halo = dot(norm(jnp.concatenate([hp_ref[...], hn_ref[...]], axis=0)), wqkv_ref[...])
    pad_scr[pl.ds(0, 8), :] = jnp.where(i == 0, 0.0, halo[:8])
    pad_scr[pl.ds(8, tm), :] = dot(xn, wqkv_ref[...])
    pad_scr[pl.ds(8 + tm, 8), :] = jnp.where(i == nt - 1, 0.0, halo[8:])
    w = cw_ref[...]
    acc = None
    for tap in range(GDN_CONV):
        term = pad_scr[pl.ds(8 - GDN_CONV // 2 + tap, tm), :] * w[tap:tap + 1, :]
        acc = term if acc is None else acc + term
    y = _silu(acc)

    def l2n_heads(x, scale):
        outs = []
        for hd in range(GDN_HEADS):
            xh = x[:, hd * GDN_DIM:(hd + 1) * GDN_DIM]
            outs.append(xh * (lax.rsqrt(jnp.sum(xh * xh, axis=-1, keepdims=True) + EPS) * scale))
        return jnp.concatenate(outs, axis=1)

    q_ref[...] = l2n_heads(y[:, :GDN_WIDTH], GDN_DIM ** -0.5)
    k_ref[...] = l2n_heads(y[:, GDN_WIDTH:2 * GDN_WIDTH], 1.0)
    v_ref[...] = y[:, 2 * GDN_WIDTH:]


def _inproj_ab_call(h, mod, layer, ctx, norm_g_row, w, conv_w):
    b, seq, _ = h.shape
    tm = min(512, seq)
    nt = seq // tm
    r8 = tm // 8
    gw = GDN_WIDTH
    resident = dict(pipeline_mode=pl.Buffered(1))
    wspec = lambda wd, blk: pl.BlockSpec((D_MODEL, wd), lambda b, i: (0, blk), **resident)
    tok = lambda wd: pl.BlockSpec((None, tm, wd), lambda b, i: (b, i, 0))
    in_specs = [tok(D_MODEL),
                pl.BlockSpec((None, 8, D_MODEL), lambda b, i: (b, jnp.maximum(i * r8 - 1, 0), 0)),
                pl.BlockSpec((None, 8, D_MODEL), lambda b, i: (b, jnp.minimum((i + 1) * r8, seq // 8 - 1), 0)),
                _mod_spec(layer, 1, 0, ctx), _mod_spec(layer, 1, 1, ctx),
                pl.BlockSpec((1, D_MODEL), lambda b, i: (0, 0)),
                wspec(3 * gw, 0), wspec(gw, 3), wspec(S5_WIDTH, 4), wspec(128, (4 * gw + S5_WIDTH) // 128),
                pl.BlockSpec((GDN_CONV, 3 * gw), lambda b, i: (0, 0))]
    widths = [gw, gw, gw, gw, S5_WIDTH, 128]
    return pl.pallas_call(
        functools.partial(_inproj_ab_kernel, tm, nt),
        grid=(b, nt),
        in_specs=in_specs,
        out_specs=[tok(wd) for wd in widths],
        out_shape=[jax.ShapeDtypeStruct((b, seq, wd), F32) for wd in widths],
        scratch_shapes=[pltpu.VMEM((tm + 16, 3 * gw), F32)],
        compiler_params=_params("parallel", "parallel"),
        name="inproj_ab",
    )(h, h, h, mod, mod, norm_g_row, w, w, w, w, conv_w)


def _gdn_kernel(rev, t_blk, nb, q_scr, k_scr, v_scr, gates_ref, alog_ref, dtb_ref, s0_ref,
                o_ref, sfin_ref,
                s_scr, sel_scr, ut_scr, w_scr, qd_scr, kd_scr, qk_scr, cd_scr, gt_scr):
    i = pl.program_id(1)
    c = GDN_CHUNK
    nch = t_blk // c
    hh = GDN_HEADS
    width = hh * GDN_DIM

    @pl.when(i == 0)
    def _():
        s_scr[...] = s0_ref[...]

    gates = gates_ref[...]
    lane = lax.broadcasted_iota(jnp.int32, gates.shape, 1)
    log_a = -jnp.exp(alog_ref[...]) * jax.nn.softplus(gates + dtb_ref[...])
    trow = lax.broadcasted_iota(jnp.int32, (t_blk, t_blk), 0)
    tcol = lax.broadcasted_iota(jnp.int32, (t_blk, t_blk), 1)
    cum = _same_block(trow, tcol, c) & ((trow <= tcol) if rev else (trow >= tcol))
    g_all = _mm_exact_lhs(cum.astype(BF16), log_a)
    for m in range(nch):
        gt_scr[m] = g_all[m * c:(m + 1) * c, :].T
    mixed = jnp.where(lane < 2 * hh, g_all, jax.nn.sigmoid(gates))
    d_off = hh if rev else 0
    src = lax.broadcasted_iota(jnp.int32, (128, 2 * width), 0)
    dst = lax.broadcasted_iota(jnp.int32, (128, 2 * width), 1)
    want = d_off + (dst >> 8) + jnp.where((dst & 128) != 0, 2 * hh, 0)
    sel_scr[...] = _mm_exact_rhs(mixed, (src == want).astype(BF16))

    row = lax.broadcasted_iota(jnp.int32, (c, c), 0)
    col = lax.broadcasted_iota(jnp.int32, (c, c), 1)
    incl = (row <= col) if rev else (row >= col)
    strict = (row < col) if rev else (row > col)
    last = 0 if rev else c - 1
    diag16 = _same_block(row, col, 16)
    levels = []
    d = 16
    while d < c:
        levels.append(_same_block(row, col, 2 * d) & jnp.logical_not(_same_block(row, col, d)))
        d *= 2

    def lanes_c(x):
        return x[:, :c] if c <= 128 else jnp.concatenate([x] * (c // 128), axis=1)

    heads = range(hh)

    def inverse_minus_identity(a_mats):
        idx = range(len(a_mats))
        p = [jnp.where(diag16, -a, 0.0) for a in a_mats]
        n = list(p)
        p = [_mm(x, x) for x in p]
        for it in range(3):
            if it < 2:
                both = [_mm(jnp.concatenate([n[i], p[i]], axis=0), p[i]) for i in idx]
                n = [n[i] + p[i] + both[i][:c] for i in idx]
                p = [both[i][c:] for i in idx]
            else:
                n = [n[i] + p[i] + _mm(n[i], p[i]) for i in idx]
        for mask in levels:
            off = [jnp.where(mask, a, 0.0) for a in a_mats]
            x = [off[i] + _mm(n[i], off[i]) for i in idx]
            n = [n[i] - (x[i] + _mm(x[i], n[i])) for i in idx]
        return n

    def prep(m, carry):
        rows = pl.ds(pl.multiple_of(m * c, c), c)
        lanes = [pl.ds(hd * GDN_DIM, GDN_DIM) for hd in heads]
        q = [q_scr[rows, lanes[hd]] for hd in heads]
        k = [k_scr[rows, lanes[hd]] for hd in heads]
        g = [sel_scr[rows, pl.ds(hd * 256, 128)] for hd in heads]
        beta = [sel_scr[rows, pl.ds(hd * 256 + 128, 128)] for hd in heads]
        decay = []
        for hd in heads:
            diff = lanes_c(g[hd]) - gt_scr[m, pl.ds(d_off + hd, 1), :]
            decay.append(jnp.where(incl, jnp.exp(jnp.where(incl, diff, 0.0)), 0.0))
        kq = [_mm_nt(jnp.concatenate([q[hd], k[hd]], axis=0), k[hd]) for hd in heads]
        a_mats = [jnp.where(strict, kq[hd][c:] * decay[hd], 0.0) * lanes_c(beta[hd]) for hd in heads]
        n_mats = inverse_minus_identity(a_mats)
        for hd in heads:
            eg = jnp.exp(g[hd])
            rhs = jnp.concatenate([v_scr[rows, lanes[hd]] * beta[hd], k[hd] * (beta[hd] * eg)], axis=1)
            sol = rhs + _mm_x3(n_mats[hd], rhs)
            g_last = g[hd][last:last + 1, :]
            ut_scr[rows, lanes[hd]] = sol[:, :GDN_DIM]
            w_scr[rows, lanes[hd]] = sol[:, GDN_DIM:]
            qd_scr[rows, lanes[hd]] = q[hd] * eg
            kd_scr[rows, lanes[hd]] = k[hd] * jnp.exp(g_last - g[hd])
            qk_scr[rows, pl.ds(hd * c, c)] = jnp.where(incl, kq[hd][:c] * decay[hd], 0.0)
            cd_scr[m, hd] = jnp.broadcast_to(jnp.exp(g_last), (8, GDN_DIM))
        return carry

    lax.fori_loop(0, nch, prep, 0, unroll=True)


    def step(ci, carry):
        cidx = (nch - 1 - ci) if rev else ci
        rows = pl.ds(pl.multiple_of(cidx * c, c), c)
        lanes = [pl.ds(hd * GDN_DIM, GDN_DIM) for hd in heads]
        s = [s_scr[hd] for hd in heads]
        ws = [_mm(jnp.concatenate([w_scr[rows, lanes[hd]], qd_scr[rows, lanes[hd]]], axis=0), s[hd]) for hd in heads]
        u = [ut_scr[rows, lanes[hd]] - ws[hd][:c] for hd in heads]
        intra = [_mm(qk_scr[rows, pl.ds(hd * c, c)], u[hd]) for hd in heads]
        outer = [_mm_tn(kd_scr[rows, lanes[hd]], u[hd]) for hd in heads]
        for hd in heads:
            o_ref[rows, lanes[hd]] = ws[hd][c:] + intra[hd]
            s_scr[hd] = s[hd] * cd_scr[cidx, hd][0:1, :] + outer[hd]
        return carry

    lax.fori_loop(0, nch, step, 0)

    @pl.when(i == nb - 1)
    def _():
        sfin_ref[...] = s_scr[...]


def _gdn_call(q, k, v, gates, alog_row, dtb_row, s0, rev):
    b, seq, _ = q.shape
    t_blk = min(512, seq)
    nb = seq // t_blk
    hh = GDN_HEADS
    width = GDN_WIDTH

    def blk(i):
        return (nb - 1 - i) if rev else i

    tile = pl.BlockSpec((None, t_blk, width), lambda b, i: (b, blk(i), 0))
    state_spec = pl.BlockSpec((None, hh, GDN_DIM, GDN_DIM), lambda b, i: (b, 0, 0, 0))
    in_specs = [tile, tile, tile,
                pl.BlockSpec((None, t_blk, 128), lambda b, i: (b, blk(i), 0)),
                pl.BlockSpec((1, 128), lambda b, i: (0, 0)),
                pl.BlockSpec((1, 128), lambda b, i: (0, 0)),
                state_spec]
    args = [q, k, v, gates, alog_row, dtb_row, s0]
    tok = lambda w: pltpu.VMEM((t_blk, w), F32)
    return pl.pallas_call(
        functools.partial(_gdn_kernel, rev, t_blk, nb),
        grid=(b, nb),
        in_specs=in_specs,
        out_specs=[tile, state_spec],
        out_shape=[jax.ShapeDtypeStruct((b, seq, width), F32),
                   jax.ShapeDtypeStruct((b, hh, GDN_DIM, GDN_DIM), F32)],
        scratch_shapes=[pltpu.VMEM((hh, GDN_DIM, GDN_DIM), F32),
                        tok(2 * width),
                        tok(width), tok(width), tok(width), tok(width), tok(hh * GDN_CHUNK),
                        pltpu.VMEM((t_blk // GDN_CHUNK, hh, 8, GDN_DIM), F32),
                        pltpu.VMEM((t_blk // GDN_CHUNK, 128, GDN_CHUNK), F32)],
        compiler_params=_params("parallel", "arbitrary"),
        name="gdn_bwd" if rev else "gdn_fwd",
    )(*args)


def _s5_ops_kernel(lr_ref, li_ref, ls_ref, btr_ref, bti_ref, cr_ref, ci_ref,
                   m_ref, winr_ref, wini_ref, woutr_ref, wouti_ref, a16r_ref, a16i_ref):
    d = pl.program_id(0)
    lr = lr_ref[...]
    li = li_ref[...]
    dt = jnp.exp(ls_ref[...])

    def apow(kk):
        mag = jnp.exp(lr * dt * kk)
        ang = li * dt * kk
        return mag * jnp.cos(ang), mag * jnp.sin(ang)

    def cmul(xr, xi, yr, yi):
        return xr * yr - xi * yi, xr * yi + xi * yr

    ar, ai = apow(1.0)
    den = lr * lr + li * li
    nr, ni = ar - 1.0, ai
    zr = (nr * lr + ni * li) / den
    zi = (ni * lr - nr * li) / den
    bbr, bbi = cmul(zr, zi, btr_ref[...], bti_ref[...])
    cr, ci = cr_ref[...], ci_ref[...]

    t16 = lax.broadcasted_iota(jnp.int32, (S5_CHUNK, 1), 0)
    tv16 = jnp.where(d == 0, t16, S5_CHUNK - 1 - t16).astype(F32)
    t_row = lax.broadcasted_iota(jnp.int32, (S5_ROW, 1), 0) // S5_GROUP
    t_col = lax.broadcasted_iota(jnp.int32, (1, S5_ROW), 1) // S5_GROUP
    tv_row = jnp.where(d == 0, t_row, S5_CHUNK - 1 - t_row)
    tv_col = jnp.where(d == 0, t_col, S5_CHUNK - 1 - t_col)
    tile = lambda x: jnp.concatenate([x] * S5_CHUNK, axis=0)
    spread = (lax.broadcasted_iota(jnp.int32, (S5_ROW, S5_CHUNK), 0) // S5_GROUP
              == lax.broadcasted_iota(jnp.int32, (S5_ROW, S5_CHUNK), 1)).astype(BF16)
    rep = lambda x: _mm_exact_lhs(spread, x)
    pos_r, pos_i = apow(tv16)
    neg_r, neg_i = apow(-tv16)
    xr, xi = cmul(tile(bbr), tile(bbi), rep(neg_r), rep(neg_i))
    yr, yi = cmul(tile(cr), tile(ci), rep(pos_r), rep(pos_i))
    nt = lambda p, q: lax.dot_general(p, q, (((1,), (1,)), ((), ())), precision=HIGHEST,
                                      preferred_element_type=F32)
    m = nt(xr, yr) - nt(xi, yi)
    m_ref[...] = jnp.where(tv_col >= tv_row, m, 0.0)

    p15r, p15i = apow(S5_CHUNK - 1.0)
    winr_ref[...], wini_ref[...] = cmul(xr, xi, p15r, p15i)
    wr, wi = cmul(yr, yi, ar, ai)
    woutr_ref[...] = wr
    wouti_ref[...] = -wi
    a16r, a16i = apow(float(S5_CHUNK))
    a16r_ref[...] = a16r
    a16i_ref[...] = a16i


def _s5_ops_call(lam_re, lam_im, log_step, bt_re, bt_im, c_re, c_im):
    g, p, cg = S5_GROUPS, S5_STATE, S5_GROUP
    ls = jnp.broadcast_to(log_step[:, :, None, None], (2, g, 1, p))
    lam_spec = pl.BlockSpec((None, None, 1, p), lambda d, gi: (d, gi, 0, 0))
    par_spec = pl.BlockSpec((None, cg, p), lambda d, gi: (gi, 0, 0))
    out = lambda *shape: pl.BlockSpec((None, None) + shape, lambda d, gi: (d, gi, 0, 0))
    return pl.pallas_call(
        _s5_ops_kernel,
        grid=(2, g),
        in_specs=[lam_spec, lam_spec, lam_spec, par_spec, par_spec, par_spec, par_spec],
        out_specs=[out(S5_ROW, S5_ROW), out(S5_ROW, p), out(S5_ROW, p), out(S5_ROW, p), out(S5_ROW, p),
                   out(1, p), out(1, p)],
        out_shape=[jax.ShapeDtypeStruct((2, g, S5_ROW, S5_ROW), F32)]
        + [jax.ShapeDtypeStruct((2, g, S5_ROW, p), F32)] * 4
        + [jax.ShapeDtypeStruct((2, g, 1, p), F32)] * 2,
        compiler_params=_params("parallel", "parallel"),
        name="s5_ops",
    )(lam_re.reshape(2, g, 1, p), lam_im.reshape(2, g, 1, p), ls, bt_re, bt_im, c_re, c_im)


def _regroup_kernel(to_groups, rows, *refs):
    units = 128 // S5_GROUP
    cols_per_t = S5_WIDTH // 128
    cols_per_g = S5_ROW // 128
    tok_refs = refs[:cols_per_t] if to_groups else refs[1:]
    row_ref = refs[cols_per_t] if to_groups else refs[0]
    unit = lax.broadcasted_iota(jnp.int32, (rows, 128), 1) // S5_GROUP

    def token_slab(t, col):
        return tok_refs[col].at[pl.ds(t, rows, stride=S5_CHUNK), :]

    def gather(pieces, src_unit):
        acc = None
        for p in range(units):
            shift = ((p - src_unit[p]) * S5_GROUP) % 128
            piece = pltpu.roll(pieces[p], shift, axis=1) if shift else pieces[p]
            acc = piece if acc is None else jnp.where(unit == p, piece, acc)
        return acc

    if to_groups:
        for gh in range(cols_per_t):
            for t_hi in range(cols_per_g):
                slabs = [token_slab(units * t_hi + p, gh)[...] for p in range(units)]
                for q in range(units):
                    j = (units * gh + q) * cols_per_g + t_hi
                    row_ref[:, j * 128:(j + 1) * 128] = gather(slabs, [q] * units).astype(row_ref.dtype)
    else:
        for j in range(S5_CHUNK * cols_per_t):
            t, gh = j // cols_per_t, j % cols_per_t
            cols = [(units * gh + p) * cols_per_g + t // units for p in range(units)]
            pieces = [row_ref[:, col * 128:(col + 1) * 128] for col in cols]
            token_slab(t, gh)[...] = gather(pieces, [t % units] * units)


def _regroup_call(x, to_groups, dtype):
    b = x.shape[0]
    nc = x.shape[1] // S5_CHUNK if to_groups else x.shape[1]
    rows = min(64, nc)
    ncol = S5_WIDTH // 128
    row_spec = pl.BlockSpec((None, rows, S5_CHUNK * S5_WIDTH), lambda b, i: (b, i, 0))
    row_shape = jax.ShapeDtypeStruct((b, nc, S5_CHUNK * S5_WIDTH), dtype)
    if to_groups:
        in_specs = [pl.BlockSpec((None, rows * S5_CHUNK, 128), functools.partial(lambda b, i, col: (b, i, col), col=col))
                    for col in range(ncol)]
        args, out_specs, out_shape = [x] * ncol, row_spec, row_shape
    else:
        in_specs, args = [row_spec], [x]
        out_specs = [pl.BlockSpec((None, rows * S5_CHUNK, 128), lambda b, i: (b, i, 0))] * ncol
        out_shape = [jax.ShapeDtypeStruct((b, nc * S5_CHUNK, 128), dtype)] * ncol
    return pl.pallas_call(
        functools.partial(_regroup_kernel, to_groups, rows),
        grid=(b, nc // rows),
        in_specs=in_specs,
        out_specs=out_specs,
        out_shape=out_shape,
        compiler_params=_params("parallel", "parallel"),
        name="s5_to_groups" if to_groups else "s5_to_tokens",
    )(*args)


def _s5_in_kernel(u_ref, wr_ref, wi_ref, sr_ref, si_ref):
    dot = functools.partial(jnp.dot, preferred_element_type=F32)
    u0, u1 = u_ref[:, :S5_ROW], u_ref[:, S5_ROW:]
    sr_ref[...] = dot(u0, wr_ref[0]) + dot(u1, wr_ref[1])
    si_ref[...] = dot(u0, wi_ref[0]) + dot(u1, wi_ref[1])


def _s5_in_call(ug, win_r, win_i):
    b, nc, _ = ug.shape
    g = S5_GROUPS
    w_spec = pl.BlockSpec((None, 2, S5_ROW, 128), lambda d, b, gp: (d, gp, 0, 0))
    o_spec = pl.BlockSpec((None, None, nc, 128), lambda d, b, gp: (d, b, 0, gp))
    return pl.pallas_call(
        _s5_in_kernel,
        grid=(2, b, g // 2),
        in_specs=[pl.BlockSpec((None, nc, 2 * S5_ROW), lambda d, b, gp: (b, 0, gp)), w_spec, w_spec],
        out_specs=[o_spec, o_spec],
        out_shape=[jax.ShapeDtypeStruct((2, b, nc, S5_LANES), F32)] * 2,
        compiler_params=_params("parallel", "parallel", "parallel"),
        name="s5_in",
    )(ug, win_r, win_i)


def _s5_scan_kernel(nc, sr_ref, si_ref, ar_ref, ai_ref, h0r_ref, h0i_ref, hr_ref, hi_ref, fr_ref, fi_ref):
    coef = [(ar_ref[d], ai_ref[d]) for d in range(2)]

    def body(i, carry):
        out = []
        for d in range(2):
            hr, hi = carry[d]
            ar, ai = coef[d]
            n = i if d == 0 else nc - 1 - i
            hr_ref[d, pl.ds(n, 1), :] = hr
            hi_ref[d, pl.ds(n, 1), :] = hi
            sr = sr_ref[d, pl.ds(n, 1), :]
            si = si_ref[d, pl.ds(n, 1), :]
            out.append((ar * hr - ai * hi + sr, ar * hi + ai * hr + si))
        return tuple(out)

    fin = lax.fori_loop(0, nc, body, tuple((h0r_ref[d], h0i_ref[d]) for d in range(2)), unroll=4)
    for d in range(2):
        fr_ref[d] = fin[d][0]
        fi_ref[d] = fin[d][1]


def _s5_scan_call(s_re, s_im, a16_re, a16_im, h0_re, h0_im):
    _, b, nc, _ = s_re.shape
    tl = 512
    big = pl.BlockSpec((2, None, nc, tl), lambda b, j: (0, b, 0, j))
    a_spec = pl.BlockSpec((2, 1, tl), lambda b, j: (0, 0, j))
    st_spec = pl.BlockSpec((2, None, 1, tl), lambda b, j: (0, b, 0, j))
    return pl.pallas_call(
        functools.partial(_s5_scan_kernel, nc),
        grid=(b, S5_LANES // tl),
        in_specs=[big, big, a_spec, a_spec, st_spec, st_spec],
        out_specs=[big, big, st_spec, st_spec],
        out_shape=[jax.ShapeDtypeStruct(s_re.shape, F32)] * 2
        + [jax.ShapeDtypeStruct((2, b, 1, S5_LANES), F32)] * 2,
        compiler_params=_params("parallel", "parallel"),
        name="s5_scan",
    )(s_re, s_im, a16_re, a16_im, h0_re, h0_im)


def _s5_out_kernel(u_ref, m_ref, hr_ref, hi_ref, wr_ref, wi_ref, y_ref):
    u = u_ref[...]
    acc = None
    for d in range(2):
        y = (jnp.dot(u, m_ref[d], preferred_element_type=F32)
             + _mm(hr_ref[d], wr_ref[d]) + _mm(hi_ref[d], wi_ref[d]))
        acc = y if acc is None else acc + y
    y_ref[...] = acc


def _s5_out_call(ug, m, h_re, h_im, wout_r, wout_i):
    b, nc, _ = ug.shape
    g = S5_GROUPS
    h_spec = pl.BlockSpec((2, None, nc, 128), lambda b, gi: (0, b, 0, gi // 2))
    w_spec = pl.BlockSpec((2, None, 128, S5_ROW), lambda b, gi: (0, gi, 0, 0))
    u_spec = pl.BlockSpec((None, nc, S5_ROW), lambda b, gi: (b, 0, gi))
    return pl.pallas_call(
        _s5_out_kernel,
        grid=(b, g),
        in_specs=[u_spec,
                  pl.BlockSpec((2, None, S5_ROW, S5_ROW), lambda b, gi: (0, gi, 0, 0)),
                  h_spec, h_spec, w_spec, w_spec],
        out_specs=u_spec,
        out_shape=jax.ShapeDtypeStruct(ug.shape, F32),
        compiler_params=_params("parallel", "parallel"),
        name="s5_out",
    )(ug, m, h_re, h_im, wout_r, wout_i)


def _s5_prepare_ops(lam_re, lam_im, log_step, b_re, b_im, c_re, c_im):
    m, win_r, win_i, wout_r, wout_i, a16_r, a16_i = _s5_ops_call(
        lam_re, lam_im, log_step, jnp.swapaxes(b_re, 1, 2), jnp.swapaxes(b_im, 1, 2), c_re, c_im)
    p = S5_STATE
    odd = (jnp.arange(S5_GROUPS) % 2 == 1)[None, :, None, None]

    def pad_in(w):
        z = jnp.zeros_like(w)
        return jnp.where(odd, jnp.concatenate([z, w], -1), jnp.concatenate([w, z], -1)).astype(BF16)

    def pad_out(w):
        wt = jnp.swapaxes(w, 2, 3)
        z = jnp.zeros_like(wt)
        return jnp.where(odd, jnp.concatenate([z, wt], 2), jnp.concatenate([wt, z], 2)).astype(BF16)

    return dict(m=m.astype(BF16), win_r=pad_in(win_r), win_i=pad_in(win_i),
                wout_r=pad_out(wout_r), wout_i=pad_out(wout_i),
                a16_r=a16_r.reshape(2, 1, S5_LANES), a16_i=a16_i.reshape(2, 1, S5_LANES))


def _s5_mix(u, ops, h0_re, h0_im):
    b, seq, _ = u.shape
    nc = seq // S5_CHUNK
    ug = _regroup_call(u, True, BF16)
    s_re, s_im = _s5_in_call(ug, ops["win_r"], ops["win_i"])
    h_re, h_im, f_re, f_im = _s5_scan_call(s_re, s_im, ops["a16_r"], ops["a16_i"], h0_re, h0_im)
    y = _s5_out_call(ug, ops["m"], h_re, h_im, ops["wout_r"], ops["wout_i"])
    y = _regroup_call(y, False, F32)
    return y, f_re, f_im


def _conv_kernel(rows, width, n_hor, has_ver, nt, *refs):
    if has_ver:
        (xh_ref, vp_ref, vc_ref, vn_ref, wh_ref, wv_ref, b_ref, g_ref, be_ref,
         o_ref, hs_scr, vs_scr) = refs
    else:
        xh_ref, wh_ref, b_ref, g_ref, be_ref, o_ref, hs_scr = refs
    i = pl.program_id(1)
    tokens = rows * width
    slot = width + 2 * CONV_PAD
    half = CONV_K // 2

    hs_scr[...] = jnp.zeros_like(hs_scr)
    for r in range(rows):
        hs_scr[r, pl.ds(CONV_PAD, width), :] = xh_ref[pl.ds(r * width, width), :]
    wh = wh_ref[...]
    acc_h = jnp.zeros((tokens, n_hor), F32)
    for tap in range(CONV_K):
        win = hs_scr[:, pl.ds(CONV_PAD - half + tap, width), :].reshape(tokens, n_hor)
        acc_h = acc_h + win * wh[tap:tap + 1, :]

    if has_ver:
        vs_scr[pl.ds(0, tokens), :] = jnp.where(i == 0, 0.0, vp_ref[...])
        vs_scr[pl.ds(tokens, tokens), :] = vc_ref[...]
        vs_scr[pl.ds(2 * tokens, tokens), :] = jnp.where(i == nt - 1, 0.0, vn_ref[...])
        wv = wv_ref[...]
        acc_v = jnp.zeros((tokens, CONV_WIDTH - n_hor), F32)
        for tap in range(CONV_K):
            acc_v = acc_v + vs_scr[pl.ds(tokens + (tap - half) * width, tokens), :] * wv[tap:tap + 1, :]
        x = jnp.concatenate([acc_h, acc_v], axis=1)
    else:
        x = acc_h
    x = x + b_ref[...]
    xc = x - jnp.mean(x, axis=-1, keepdims=True)
    y = xc * lax.rsqrt(jnp.mean(xc * xc, axis=-1, keepdims=True) + EPS) * g_ref[...] + be_ref[...]
    o_ref[...] = _silu(y)


def _conv_call(xc, dw_w, dw_b, ln_g, ln_b, grid_rows):
    b, seq, ch = xc.shape
    vec = pl.BlockSpec((1, ch), lambda b, i: (0, 0))
    if grid_rows > 0:
        width, rows, n_hor = GRID_W, 16, ch // 2
        tokens = rows * width
        nt = seq // tokens
        half_spec = lambda f: pl.BlockSpec((None, tokens, n_hor), f)
        in_specs = [half_spec(lambda b, i: (b, i, 0)),
                    half_spec(lambda b, i: (b, jnp.maximum(i - 1, 0), 1)),
                    half_spec(lambda b, i: (b, i, 1)),
                    half_spec(lambda b, i: (b, jnp.minimum(i + 1, nt - 1), 1)),
                    pl.BlockSpec((CONV_K, n_hor), lambda b, i: (0, 0)),
                    pl.BlockSpec((CONV_K, n_hor), lambda b, i: (0, 1)),
                    vec, vec, vec]
        args = [xc, xc, xc, xc, dw_w, dw_w, dw_b, ln_g, ln_b]
        scratch = [pltpu.VMEM((rows, width + 2 * CONV_PAD, n_hor), F32),
                   pltpu.VMEM((3 * tokens, ch - n_hor), F32)]
        has_ver = True
    else:
        width, rows, n_hor = seq, 1, ch
        tokens = seq
        nt = 1
        in_specs = [pl.BlockSpec((None, tokens, ch), lambda b, i: (b, 0, 0)),
                    pl.BlockSpec((CONV_K, ch), lambda b, i: (0, 0)),
                    vec, vec, vec]
        args = [xc, dw_w, dw_b, ln_g, ln_b]
        scratch = [pltpu.VMEM((rows, width + 2 * CONV_PAD, n_hor), F32)]
        has_ver = False
    return pl.pallas_call(
        functools.partial(_conv_kernel, rows, width, n_hor, has_ver, nt),
        grid=(b, nt),
        in_specs=in_specs,
        out_specs=pl.BlockSpec((None, tokens, ch), lambda b, i: (b, i, 0)),
        out_shape=jax.ShapeDtypeStruct((b, seq, ch), F32),
        scratch_shapes=scratch,
        compiler_params=_params("parallel", "parallel"),
        name="conformer_conv",
    )(*args)


def _lb_kernel(x_ref, o_ref):
    x = x_ref[...]
    n = x.shape[0]
    rows = [x[r:r + 1, :] for r in range(n)]
    mx = functools.reduce(jnp.maximum, rows)
    ex = [jnp.exp(r - mx) for r in rows]
    tot = functools.reduce(lambda p, q: p + q, ex)
    run = None
    for r in range(n):
        run = ex[r] / tot if run is None else run + ex[r] / tot
        o_ref[pl.ds(r, 1), :] = run


def _lb_call(logits):
    return pl.pallas_call(_lb_kernel, out_shape=jax.ShapeDtypeStruct(logits.shape, F32), name="hgrn_lb")(logits)


def _hgrn_kernel(rev, t_blk, nb, q_ref, f_ref, v_ref, lb_ref, s0_ref, o_ref, sfin_ref,
                 st_scr, kv_scr, sall_scr):
    i = pl.program_id(1)
    c = HGRN_CHUNK
    nch = t_blk // c
    hh = HGRN_HEADS

    @pl.when(i == 0)
    def _():
        st_scr[...] = s0_ref[...]

    lb = lb_ref[...]
    f = lb + (1.0 - lb) * jax.nn.sigmoid(f_ref[...])
    k = 1.0 - f
    log_f = jnp.log(f)
    row = lax.broadcasted_iota(jnp.int32, (t_blk, t_blk), 0)
    col = lax.broadcasted_iota(jnp.int32, (t_blk, t_blk), 1)
    same = _same_block(row, col, c)
    incl = same & ((row <= col) if rev else (row >= col))
    bc = _mm_exact_lhs(incl.astype(BF16), log_f)
    last = 0 if rev else c - 1
    bc3 = bc.reshape(nch, c, bc.shape[1])
    tot = jnp.broadcast_to(bc3[:, last:last + 1, :], bc3.shape).reshape(bc.shape)
    q_in = q_ref[...] * jnp.exp(bc)
    k_in = k * jnp.exp(-bc)
    k_out = k * jnp.exp(tot - bc)
    cd = jnp.exp(tot)
    v = v_ref[...]
    order = range(nch - 1, -1, -1) if rev else range(nch)
    heads = range(hh)
    ls = [slice(hd * HGRN_DIM, (hd + 1) * HGRN_DIM) for hd in heads]
    attn = [jnp.where(incl, _mm_nt(q_in[:, ls[hd]], k_in[:, ls[hd]]), 0.0) for hd in heads]
    o_intra = [_mm(attn[hd], v[:, ls[hd]]) for hd in heads]
    for n in range(nch):
        rows = slice(n * c, (n + 1) * c)
        for hd in heads:
            kv_scr[hd, n] = _mm_tn(v[rows, ls[hd]], k_out[rows, ls[hd]])
    st = [st_scr[hd] for hd in heads]
    for n in order:
        for hd in heads:
            sall_scr[hd, n] = st[hd]
            st[hd] = st[hd] * cd[n * c:n * c + 1, ls[hd]] + kv_scr[hd, n]
    for hd in heads:
        st_scr[hd] = st[hd]
    for n in range(nch):
        rows = slice(n * c, (n + 1) * c)
        for hd in heads:
            o_ref[pl.ds(n * c, c), pl.ds(hd * HGRN_DIM, HGRN_DIM)] = (
                _mm_nt(q_in[rows, ls[hd]], sall_scr[hd, n]) + o_intra[hd][rows])

    @pl.when(i == nb - 1)
    def _():
        sfin_ref[...] = st_scr[...]


def _hgrn_call(q, fgate, v, lb_row, s0, rev):
    b, seq, _ = q.shape
    t_blk = min(256, seq)
    nb = seq // t_blk
    hh = HGRN_HEADS

    def blk(i):
        return (nb - 1 - i) if rev else i

    tok = pl.BlockSpec((None, t_blk, HGRN_WIDTH), lambda b, i: (b, blk(i), 0))
    state_spec = pl.BlockSpec((None, hh, HGRN_DIM, HGRN_DIM), lambda b, i: (b, 0, 0, 0))
    chunk_states = pltpu.VMEM((hh, t_blk // HGRN_CHUNK, HGRN_DIM, HGRN_DIM), F32)
    return pl.pallas_call(
        functools.partial(_hgrn_kernel, rev, t_blk, nb),
        grid=(b, nb),
        in_specs=[tok, tok, tok, pl.BlockSpec((1, HGRN_WIDTH), lambda b, i: (0, 0)), state_spec],
        out_specs=[tok, state_spec],
        out_shape=[jax.ShapeDtypeStruct((b, seq, HGRN_WIDTH), F32),
                   jax.ShapeDtypeStruct((b, hh, HGRN_DIM, HGRN_DIM), F32)],
        scratch_shapes=[pltpu.VMEM((hh, HGRN_DIM, HGRN_DIM), F32), chunk_states, chunk_states],
        compiler_params=_params("parallel", "arbitrary"),
        name="hgrn_bwd" if rev else "hgrn_fwd",
    )(q, fgate, v, lb_row, s0)


def _mixer_ab(h, hc, mod, layer, need_ctx, prm):
    ng = prm["norm_g1"]
    outs = {}
    gdn_state = [jnp.zeros((h.shape[0], GDN_HEADS, GDN_DIM, GDN_DIM), F32)] * 2
    s5_state = [jnp.zeros((2, h.shape[0], 1, S5_LANES), F32)] * 2
    for ctx, x in ((True, hc), (False, h)):
        q, k, v, z, u, gates = _inproj_ab_call(x, mod, layer, ctx, ng, prm["w_in"], prm["conv_w"])
        o_f, sf = _gdn_call(q, k, v, gates, prm["alog"], prm["dtb"], gdn_state[0], rev=False)
        o_b, sb = _gdn_call(q, k, v, gates, prm["alog"], prm["dtb"], gdn_state[1], rev=True)
        y5, f_re, f_im = _s5_mix(u, prm["s5"], s5_state[0], s5_state[1])
        gdn_state = [sf, sb]
        s5_state = [f_re, f_im]
        if ctx and not need_ctx:
            continue
        outs[ctx] = ("ab", [o_f, o_b, z, *y5, u],
                     [prm["gdn_g"], prm["s5_d"], prm["glu_w"], prm["glu_b"], prm["w_out"]])
    return outs[False], outs.get(True)


def _mixer_cd(h, hc, mod, layer, need_ctx, prm, grid_rows):
    ng = prm["norm_g1"]
    outs = {}
    state = [jnp.zeros((h.shape[0], HGRN_HEADS, HGRN_DIM, HGRN_DIM), F32)] * 2
    for ctx, x in ((True, hc), (False, h)):
        xc, q, f_f, f_b, iv, g = _inproj_call(x, mod, layer, ctx, ng, prm["w_in"], prm["pieces"], glu=True)
        o_f, sf = _hgrn_call(q, f_f, iv, prm["lb"], state[0], rev=False)
        o_b, sb = _hgrn_call(q, f_b, iv, prm["lb"], state[1], rev=True)
        state = [sf, sb]
        if ctx and not need_ctx:
            continue
        cmix = _conv_call(xc, prm["dw_w"], prm["dw_b"], prm["ln_g"], prm["ln_b"], 0 if ctx else grid_rows)
        outs[ctx] = ("cd", [cmix, o_f, o_b, g], [prm["hgrn_g"], prm["w_out"]])
    return outs[False], outs.get(True)


def kernel(x, c, ctx, c_ctx, ada_w, ada_b, norm_g, ffn_w_up, ffn_w_down, ab_w_in, ab_w_out, gdn_conv_w, gdn_a_log, gdn_dt_bias, gdn_norm_g, s5_lambda_re, s5_lambda_im, s5_log_step, s5_b_re, s5_b_im, s5_c_re, s5_c_im, s5_d, s5_glu_w, s5_glu_b, cd_w_in, cd_w_out, conv_dw_w, conv_dw_b, conv_ln_g, conv_ln_b, hgrn_lb_logits, hgrn_norm_g, final_norm_g):
    depth = ada_w.shape[0]
    batch = x.shape[0]
    grid_rows = x.shape[1] // GRID_W
    assert batch <= 2 and x.shape[1] % 1024 == 0 and ctx.shape[1] % 256 == 0

    cvec = jnp.zeros((8, D_MODEL), F32).at[:batch].set(c).at[2].set(c_ctx)
    mod = _ada_call(cvec, ada_w, ada_b)
    lb_all = _lb_call(hgrn_lb_logits)
    row = lambda v: v.reshape(1, -1)
    gw, kw = GDN_WIDTH, HGRN_WIDTH

    w_up = ffn_w_up.astype(BF16)
    w_down = ffn_w_down.astype(BF16)
    h, hc = x, ctx
    for l in range(depth):
        last = l == depth - 1
        h = _ffn_call(h, mod, l, 0, False, row(norm_g[l, 0]), w_up, w_down, (l, 0))
        hc = _ffn_call(hc, mod, l, 0, True, row(norm_g[l, 0]), w_up, w_down, (l, 0))
        if l % 2 == 0:
            e = l // 2
            w = ab_w_in[e]
            ng4 = 4 * GDN_HEADS
            w = jnp.concatenate([w[:, :4 * gw], w[:, 4 * gw + ng4:], w[:, 4 * gw:4 * gw + ng4],
                                 jnp.zeros((D_MODEL, 128 - ng4), F32)], axis=1).astype(BF16)
            pad8 = lambda v: jnp.zeros((1, 128), F32).at[0, :2 * GDN_HEADS].set(v.reshape(-1))
            prm = dict(
                norm_g1=row(norm_g[l, 1]),
                w_in=w, conv_w=gdn_conv_w[e], alog=pad8(gdn_a_log[e]), dtb=pad8(gdn_dt_bias[e]),
                gdn_g=row(gdn_norm_g[e]),
                s5=_s5_prepare_ops(s5_lambda_re[e], s5_lambda_im[e], s5_log_step[e], s5_b_re[e], s5_b_im[e],
                                   s5_c_re[e], s5_c_im[e]),
                s5_d=row(s5_d[e]), glu_w=s5_glu_w[e].astype(BF16), glu_b=row(s5_glu_b[e]),
                w_out=ab_w_out[e].astype(BF16))
            mix, mix_ctx = _mixer_ab(h, hc, mod, l, not last, prm)
        else:
            o = l // 2
            cw = CONV_WIDTH
            prm = dict(
                norm_g1=row(norm_g[l, 1]),
                w_in=cd_w_in[o].astype(BF16),
                pieces=[(k * cw, cw) for k in range(2)] + [(2 * cw + k * kw, kw) for k in range(5)],
                lb=lb_all[o:o + 1], hgrn_g=row(hgrn_norm_g[o]),
                dw_w=conv_dw_w[o], dw_b=row(conv_dw_b[o]), ln_g=row(conv_ln_g[o]), ln_b=row(conv_ln_b[o]),
                w_out=cd_w_out[o].astype(BF16))
            mix, mix_ctx = _mixer_cd(h, hc, mod, l, not last, prm, grid_rows)
        h = _ffn_call(h, mod, l, 2, False, row(norm_g[l, 2]), w_up, w_down, (l, 1),
                      final_g=row(final_norm_g) if last else None, mixer=mix)
        if not last:
            hc = _ffn_call(hc, mod, l, 2, True, row(norm_g[l, 2]), w_up, w_down, (l, 1), mixer=mix_ctx)
    return h
```

```python
import functools
import math

import jax
import jax.numpy as jnp
from jax import lax
from jax.experimental import pallas as pl
from jax.experimental.pallas import tpu as pltpu

F32 = jnp.float32
BF16 = jnp.bfloat16
HIGHEST = lax.Precision.HIGHEST
EPS = 1e-6

D_MODEL = 1024
GRID_W = 64
FFN_DIM = 2816

GDN_HEADS = 4
GDN_DIM = 128
GDN_WIDTH = GDN_HEADS * GDN_DIM
GDN_CONV = 5
GDN_CHUNK = 256

S5_WIDTH = 512
S5_GROUP = 16
S5_GROUPS = 32
S5_STATE = 64
S5_CHUNK = 16
S5_ROW = S5_CHUNK * S5_GROUP
S5_LANES = S5_GROUPS * S5_STATE

CONV_WIDTH = 512
CONV_K = 31
CONV_PAD = 16

HGRN_HEADS = 4
HGRN_DIM = 128
HGRN_WIDTH = HGRN_HEADS * HGRN_DIM
HGRN_CHUNK = 16

V7X_VMEM_LIMIT = 48 * 1024 * 1024


def _params(*sem):
    return pltpu.CompilerParams(dimension_semantics=sem, vmem_limit_bytes=V7X_VMEM_LIMIT)


def _silu(x):
    return x * jax.nn.sigmoid(x)


def _mm(a, b):
    return jnp.dot(a.astype(BF16), b.astype(BF16), preferred_element_type=F32)


def _mm_nt(a, b):
    return lax.dot_general(a.astype(BF16), b.astype(BF16), (((1,), (1,)), ((), ())),
                           preferred_element_type=F32)


def _mm_tn(a, b):
    return lax.dot_general(a.astype(BF16), b.astype(BF16), (((0,), (0,)), ((), ())),
                           preferred_element_type=F32)


def _split3(x):
    hi = x.astype(BF16)
    r1 = x - hi.astype(F32)
    mid = r1.astype(BF16)
    lo = (r1 - mid.astype(F32)).astype(BF16)
    return hi, mid, lo


def _mm_exact_lhs(a_bf16, x):
    return jnp.dot(jnp.concatenate([a_bf16] * 3, axis=1), jnp.concatenate(_split3(x), axis=0),
                   preferred_element_type=F32)


def _mm_exact_rhs(x, b_bf16):
    return jnp.dot(jnp.concatenate(_split3(x), axis=1), jnp.concatenate([b_bf16] * 3, axis=0),
                   preferred_element_type=F32)


def _mm_x3(a, b):
    ah = a.astype(BF16)
    al = (a - ah.astype(F32)).astype(BF16)
    bh = b.astype(BF16)
    bl = (b - bh.astype(F32)).astype(BF16)
    return jnp.dot(jnp.concatenate([ah, ah, al], axis=1), jnp.concatenate([bh, bl, bh], axis=0),
                   preferred_element_type=F32)


def _same_block(i, j, size):
    return (i ^ j) < size


def _ada_norm(x, g, scale, shift):
    y = x * lax.rsqrt(jnp.mean(x * x, axis=-1, keepdims=True) + EPS) * g
    return y * (1.0 + scale) + shift


def _ada_kernel(c_ref, w_ref, b_ref, o_ref):
    o_ref[...] = _mm(_silu(c_ref[...]), w_ref[...]) + b_ref[...]


def _ada_call(cvec, ada_w, ada_b):
    depth = ada_w.shape[0]
    ncol = ada_w.shape[2] // D_MODEL
    out = pl.pallas_call(
        _ada_kernel,
        grid=(depth, ncol),
        in_specs=[pl.BlockSpec((8, D_MODEL), lambda l, j: (0, 0)),
                  pl.BlockSpec((None, D_MODEL, D_MODEL), lambda l, j: (l, 0, j)),
                  pl.BlockSpec((None, 1, D_MODEL), lambda l, j: (l, 0, j))],
        out_specs=pl.BlockSpec((None, None, 8, D_MODEL), lambda l, j: (l, j, 0, 0)),
        out_shape=jax.ShapeDtypeStruct((depth, ncol, 8, D_MODEL), F32),
        compiler_params=_params("arbitrary", "arbitrary"),
        name="ada_mod",
    )(cvec, ada_w, ada_b.reshape(depth, 1, -1))
    return out.reshape(depth * ncol * 8, 1, D_MODEL)


def _mod_spec(layer, sub, kind, ctx):
    base = (layer * 9 + sub * 3 + kind) * 8
    if ctx:
        return pl.BlockSpec((None, 1, D_MODEL), lambda b, *_: (base + 2, 0, 0))
    return pl.BlockSpec((None, 1, D_MODEL), lambda b, *_: (base + b, 0, 0))


def _head_norm(o, gate, g):
    outs = []
    for hd in range(o.shape[1] // 128):
        x = o[:, hd * 128:(hd + 1) * 128]
        x = x * lax.rsqrt(jnp.mean(x * x, axis=-1, keepdims=True) + EPS) * g
        outs.append(x * _silu(gate[:, hd * 128:(hd + 1) * 128]))
    return jnp.concatenate(outs, axis=1)


def _mix_ab(of_ref, ob_ref, z_ref, y5a_ref, y5b_ref, y5c_ref, y5d_ref, u_ref, ng_ref, dsk_ref, glw_ref, glb_ref,
            wout_ref):
    a = _head_norm(of_ref[...] + ob_ref[...], z_ref[...], ng_ref[...])
    y5 = jnp.concatenate([y5a_ref[...], y5b_ref[...], y5c_ref[...], y5d_ref[...]], axis=1)
    y = y5 + dsk_ref[...] * u_ref[...]
    y = 0.5 * y * (1.0 + jnp.tanh(math.sqrt(2.0 / math.pi) * (y + 0.044715 * (y * y * y))))
    bmix = y * jax.nn.sigmoid(_mm(y, glw_ref[...]) + glb_ref[...])
    return _mm(jnp.concatenate([a, bmix], axis=1), wout_ref[...])


def _mix_cd(c_ref, of_ref, ob_ref, g_ref, ng_ref, wout_ref):
    dmix = _head_norm(of_ref[...] + ob_ref[...], g_ref[...], ng_ref[...])
    return _mm(jnp.concatenate([c_ref[...], dmix], axis=1), wout_ref[...])


_MIXERS = {"ab": (_mix_ab, 13), "cd": (_mix_cd, 6)}


def _ffn_kernel(tf, final, mixer, h_ref, *refs):
    x = h_ref[...]
    if mixer is not None:
        mix_fn, n_mix = _MIXERS[mixer]
        x = x + refs[0][...] * mix_fn(*refs[1:1 + n_mix])
        refs = refs[1 + n_mix:]
    sh_ref, sc_ref, gt_ref, g_ref, wup_ref, wd_ref = refs[:6]
    if final:
        fg_ref, o_ref = refs[6:]
    else:
        (o_ref,) = refs[6:]
    xn = _ada_norm(x, g_ref[...], sc_ref[...], sh_ref[...]).astype(BF16)
    acc = None
    for j in range(FFN_DIM // tf):
        gate = jnp.dot(xn, wup_ref[:, j * tf:(j + 1) * tf], preferred_element_type=F32)
        up = jnp.dot(xn, wup_ref[:, FFN_DIM + j * tf:FFN_DIM + (j + 1) * tf], preferred_element_type=F32)
        act = (_silu(gate) * up).astype(BF16)
        part = jnp.dot(act, wd_ref[j * tf:(j + 1) * tf, :], preferred_element_type=F32)
        acc = part if acc is None else acc + part
    y = x + 0.5 * gt_ref[...] * acc
    if final:
        y = y * lax.rsqrt(jnp.mean(y * y, axis=-1, keepdims=True) + EPS) * fg_ref[...]
    o_ref[...] = y


def _ffn_call(h, mod, layer, sub, ctx, norm_g_row, w_up, w_down, which, final_g=None, mixer=None):
    b, seq, _ = h.shape
    tm = min(512, seq)
    tf = 256
    final = final_g is not None
    resident = dict(pipeline_mode=pl.Buffered(1))
    in_specs = [pl.BlockSpec((None, tm, D_MODEL), lambda b, i: (b, i, 0))]
    args = [h]
    if mixer is not None:
        kind, toks, consts = mixer
        in_specs.append(_mod_spec(layer, 1, 2, ctx))
        in_specs += [pl.BlockSpec((None, tm, t.shape[2]), lambda b, i: (b, i, 0)) for t in toks]
        in_specs += [pl.BlockSpec(cst.shape, lambda b, i: (0, 0), **resident) for cst in consts]
        args += [mod, *toks, *consts]
        assert 1 + len(toks) + len(consts) == 1 + _MIXERS[kind][1]
    in_specs += [
        _mod_spec(layer, sub, 0, ctx), _mod_spec(layer, sub, 1, ctx), _mod_spec(layer, sub, 2, ctx),
        pl.BlockSpec((1, D_MODEL), lambda b, i: (0, 0)),
        pl.BlockSpec((None, None, D_MODEL, 2 * FFN_DIM), lambda b, i: which + (0, 0), **resident),
        pl.BlockSpec((None, None, FFN_DIM, D_MODEL), lambda b, i: which + (0, 0), **resident),
    ]
    args += [mod, mod, mod, norm_g_row, w_up, w_down]
    if final:
        in_specs.append(pl.BlockSpec((1, D_MODEL), lambda b, i: (0, 0)))
        args.append(final_g)
    return pl.pallas_call(
        functools.partial(_ffn_kernel, tf, final, None if mixer is None else mixer[0]),
        grid=(b, seq // tm),
        in_specs=in_specs,
        out_specs=pl.BlockSpec((None, tm, D_MODEL), lambda b, i: (b, i, 0)),
        out_shape=jax.ShapeDtypeStruct(h.shape, F32),
        compiler_params=_params("parallel", "parallel"),
        name="ffn",
    )(*args)


def _inproj_kernel(nw, glu, h_ref, sh_ref, sc_ref, g_ref, *refs):
    w_refs, o_refs = refs[:nw], refs[nw:]
    xn = _ada_norm(h_ref[...], g_ref[...], sc_ref[...], sh_ref[...]).astype(BF16)
    outs = [jnp.dot(xn, w[...], preferred_element_type=F32) for w in w_refs]
    if glu:
        outs = [outs[0] * jax.nn.sigmoid(outs[1])] + outs[2:]
    for o_ref, val in zip(o_refs, outs):
        o_ref[...] = val


def _inproj_call(h, mod, layer, ctx, norm_g_row, w, pieces, glu):
    b, seq, _ = h.shape
    tm = min(512, seq)
    widths = [wd for _, wd in pieces]
    out_widths = widths[1:] if glu else widths
    in_specs = [pl.BlockSpec((None, tm, D_MODEL), lambda b, i: (b, i, 0)),
                _mod_spec(layer, 1, 0, ctx), _mod_spec(layer, 1, 1, ctx),
                pl.BlockSpec((1, D_MODEL), lambda b, i: (0, 0))]
    for off, wd in pieces:
        assert off % wd == 0
        in_specs.append(pl.BlockSpec((D_MODEL, wd), functools.partial(lambda b, i, blk: (0, blk), blk=off // wd),
                                     pipeline_mode=pl.Buffered(1)))
    return pl.pallas_call(
        functools.partial(_inproj_kernel, len(pieces), glu),
        grid=(b, seq // tm),
        in_specs=in_specs,
        out_specs=[pl.BlockSpec((None, tm, wd), lambda b, i: (b, i, 0)) for wd in out_widths],
        out_shape=[jax.ShapeDtypeStruct((b, seq, wd), F32) for wd in out_widths],
        compiler_params=_params("parallel", "parallel"),
        name="inproj",
    )(h, mod, mod, norm_g_row, *([w] * len(pieces)))


def _inproj_ab_kernel(tm, nt, h_ref, hp_ref, hn_ref, sh_ref, sc_ref, g_ref, wqkv_ref, wz_ref, wu_ref, wg_ref, cw_ref,
                      q_ref, k_ref, v_ref, z_ref, u_ref, gates_ref, pad_scr):
    i = pl.program_id(1)
    norm = lambda x: _ada_norm(x, g_ref[...], sc_ref[...], sh_ref[...]).astype(BF16)
    dot = functools.partial(jnp.dot, preferred_element_type=F32)
    xn = norm(h_ref[...])
    xh = norm(jnp.concatenate([hp_ref[...], hn_ref[...]], axis=0))
    w = cw_ref[...]
    gw = GDN_WIDTH

    def project(part):
        cols = pl.ds(part * gw, gw)
        wp = wqkv_ref[:, part * gw:(part + 1) * gw]
        halo = dot(xh, wp)
        pad_scr[pl.ds(0, 8), cols] = jnp.where(i == 0, 0.0, halo[:8])
        pad_scr[pl.ds(8, tm), cols] = dot(xn, wp)
        pad_scr[pl.ds(8 + tm, 8), cols] = jnp.where(i == nt - 1, 0.0, halo[8:])

    def conv(part):
        acc = None
        for tap in range(GDN_CONV):
            term = (pad_scr[pl.ds(8 - GDN_CONV // 2 + tap, tm), pl.ds(part * gw, gw)]
                    * w[tap:tap + 1, part * gw:(part + 1) * gw])
            acc = term if acc is None else acc + term
        return _silu(acc)

    def l2n_heads(x, scale):
        outs = []
        for hd in range(GDN_HEADS):
            xs = x[:, hd * GDN_DIM:(hd + 1) * GDN_DIM]
            outs.append(xs * (lax.rsqrt(jnp.sum(xs * xs, axis=-1, keepdims=True) + EPS) * scale))
        return jnp.concatenate(outs, axis=1)

    project(0)
    project(1)
    q_ref[...] = l2n_heads(conv(0), GDN_DIM ** -0.5)
    project(2)
    k_ref[...] = l2n_heads(conv(1), 1.0)
    z_ref[...] = dot(xn, wz_ref[...])
    v_ref[...] = conv(2)
    u_ref[...] = dot(xn, wu_ref[...])
    gates_ref[...] = dot(xn, wg_ref[...])


def _inproj_ab_call(h, mod, layer, ctx, norm_g_row, w, conv_w):
    b, seq, _ = h.shape
    tm = min(512, seq)
    nt = seq // tm
    r8 = tm // 8
    gw = GDN_WIDTH
    resident = dict(pipeline_mode=pl.Buffered(1))
    wspec = lambda wd, blk: pl.BlockSpec((D_MODEL, wd), lambda b, i: (0, blk), **resident)
    tok = lambda wd: pl.BlockSpec((None, tm, wd), lambda b, i: (b, i, 0))
    in_specs = [tok(D_MODEL),
                pl.BlockSpec((None, 8, D_MODEL), lambda b, i: (b, jnp.maximum(i * r8 - 1, 0), 0)),
                pl.BlockSpec((None, 8, D_MODEL), lambda b, i: (b, jnp.minimum((i + 1) * r8, seq // 8 - 1), 0)),
                _mod_spec(layer, 1, 0, ctx), _mod_spec(layer, 1, 1, ctx),
                pl.BlockSpec((1, D_MODEL), lambda b, i: (0, 0)),
                wspec(3 * gw, 0), wspec(gw, 3), wspec(S5_WIDTH, 4), wspec(128, (4 * gw + S5_WIDTH) // 128),
                pl.BlockSpec((GDN_CONV, 3 * gw), lambda b, i: (0, 0))]
    widths = [gw, gw, gw, gw, S5_WIDTH, 128]
    return pl.pallas_call(
        functools.partial(_inproj_ab_kernel, tm, nt),
        grid=(b, nt),
        in_specs=in_specs,
        out_specs=[tok(wd) for wd in widths],
        out_shape=[jax.ShapeDtypeStruct((b, seq, wd), F32) for wd in widths],
        scratch_shapes=[pltpu.VMEM((tm + 16, 3 * gw), F32)],
        compiler_params=_params("parallel", "parallel"),
        name="inproj_ab",
    )(h, h, h, mod, mod, norm_g_row, w, w, w, w, conv_w)


def _gdn_kernel(rev, t_blk, nb, q_scr, k_scr, v_scr, gates_ref, alog_ref, dtb_ref, s0_ref,
                o_ref, sfin_ref,
                s_scr, sel_scr, ut_scr, w_scr, qd_scr, kd_scr, qk_scr, cd_scr, gt_scr):
    i = pl.program_id(1)
    c = GDN_CHUNK
    nch = t_blk // c
    hh = GDN_HEADS
    width = hh * GDN_DIM

    @pl.when(i == 0)
    def _():
        s_scr[...] = s0_ref[...]

    gates = gates_ref[...]
    log_a =-jnp.exp(alog_ref[...]) * jax.nn.softplus(gates + dtb_ref[...])
    row = lax.broadcasted_iota(jnp.int32, (c, c), 0)
    col = lax.broadcasted_iota(jnp.int32, (c, c), 1)
    incl = (row <= col) if rev else (row >= col)
    strict = (row < col) if rev else (row > col)
    last = 0 if rev else c - 1
    g_all = jnp.concatenate([_mm_exact_lhs(incl.astype(BF16), log_a[m * c:(m + 1) * c, :]) for m in range(nch)],
                            axis=0)
    for m in range(nch):
        gt_scr[m] = g_all[m * c:(m + 1) * c, :].T
    beta_all = jax.nn.sigmoid(gates)
    d_off = hh if rev else 0
    for hd in range(hh):
        sel_scr[:, pl.ds(hd * 256, 128)] = jnp.broadcast_to(g_all[:, d_off + hd:d_off + hd + 1], (t_blk, 128))
        sel_scr[:, pl.ds(hd * 256 + 128, 128)] = jnp.broadcast_to(
            beta_all[:, 2 * hh + d_off + hd:2 * hh + d_off + hd + 1], (t_blk, 128))

    diag16 = _same_block(row, col, 16)
    levels = []
    d = 16
    while d < c:
        levels.append(_same_block(row, col, 2 * d) & jnp.logical_not(_same_block(row, col, d)))
        d *= 2

    def lanes_c(x):
        return x[:, :c] if c <= 128 else jnp.concatenate([x] * (c // 128), axis=1)

    heads = range(hh)

    def inverse_minus_identity(a_mats):
        idx = range(len(a_mats))
        p = [jnp.where(diag16, -a, 0.0) for a in a_mats]
        n = list(p)
        p = [_mm(x, x) for x in p]
        for it in range(3):
            if it < 2:
                both = [_mm(jnp.concatenate([n[i], p[i]], axis=0), p[i]) for i in idx]
                n = [n[i] + p[i] + both[i][:c] for i in idx]
                p = [both[i][c:] for i in idx]
            else:
                n = [n[i] + p[i] + _mm(n[i], p[i]) for i in idx]
        for mask in levels:
            off = [jnp.where(mask, a, 0.0) for a in a_mats]
            x = [off[i] + _mm(n[i], off[i]) for i in idx]
            n = [n[i] - (x[i] + _mm(x[i], n[i])) for i in idx]
        return n

    def prep(m, carry):
        rows = pl.ds(pl.multiple_of(m * c, c), c)
        lanes = [pl.ds(hd * GDN_DIM, GDN_DIM) for hd in heads]
        q = [q_scr[rows, lanes[hd]] for hd in heads]
        k = [k_scr[rows, lanes[hd]] for hd in heads]
        g = [sel_scr[rows, pl.ds(hd * 256, 128)] for hd in heads]
        beta = [sel_scr[rows, pl.ds(hd * 256 + 128, 128)] for hd in heads]
        decay = []
        for hd in heads:
            diff = lanes_c(g[hd]) - gt_scr[m, pl.ds(d_off + hd, 1), :]
            decay.append(jnp.where(incl, jnp.exp(jnp.where(incl, diff, 0.0)), 0.0))
        kq = [_mm_nt(jnp.concatenate([q[hd], k[hd]], axis=0), k[hd]) for hd in heads]
        a_mats = [jnp.where(strict, kq[hd][c:] * decay[hd], 0.0) * lanes_c(beta[hd]) for hd in heads]
        n_mats = inverse_minus_identity(a_mats)
        for hd in heads:
            eg = jnp.exp(g[hd])
            rhs = jnp.concatenate([v_scr[rows, lanes[hd]] * beta[hd], k[hd] * (beta[hd] * eg)], axis=1)
            sol = rhs + _mm_x3(n_mats[hd], rhs)
            g_last = g[hd][last:last + 1, :]
            ut_scr[rows, lanes[hd]] = sol[:, :GDN_DIM]
            w_scr[rows, lanes[hd]] = sol[:, GDN_DIM:]
            qd_scr[rows, lanes[hd]] = q[hd] * eg
            kd_scr[rows, lanes[hd]] = k[hd] * jnp.exp(g_last - g[hd])
            qk_scr[rows, pl.ds(hd * c, c)] = jnp.where(incl, kq[hd][:c] * decay[hd], 0.0)
            cd_scr[m, hd] = jnp.broadcast_to(jnp.exp(g_last), (8, GDN_DIM))
        return carry

    lax.fori_loop(0, nch, prep, 0, unroll=True)


    def step(ci, carry):
        cidx = (nch - 1 - ci) if rev else ci
        rows = pl.ds(pl.multiple_of(cidx * c, c), c)
        lanes = [pl.ds(hd * GDN_DIM, GDN_DIM) for hd in heads]
        s = [s_scr[hd] for hd in heads]
        ws = [_mm(jnp.concatenate([w_scr[rows, lanes[hd]], qd_scr[rows, lanes[hd]]], axis=0), s[hd]) for hd in heads]
        u = [ut_scr[rows, lanes[hd]] - ws[hd][:c] for hd in heads]
        intra = [_mm(qk_scr[rows, pl.ds(hd * c, c)], u[hd]) for hd in heads]
        outer = [_mm_tn(kd_scr[rows, lanes[hd]], u[hd]) for hd in heads]
        for hd in heads:
            o_ref[rows, lanes[hd]] = ws[hd][c:] + intra[hd]
            s_scr[hd] = s[hd] * cd_scr[cidx, hd][0:1, :] + outer[hd]
        return carry

    lax.fori_loop(0, nch, step, 0)

    @pl.when(i == nb - 1)
    def _():
        sfin_ref[...] = s_scr[...]


def _gdn_call(q, k, v, gates, alog_row, dtb_row, s0, rev):
    b, seq, _ = q.shape
    t_blk = min(512, seq)
    nb = seq // t_blk
    hh = GDN_HEADS
    width = GDN_WIDTH

    def blk(i):
        return (nb - 1 - i) if rev else i

    tile = pl.BlockSpec((None, t_blk, width), lambda b, i: (b, blk(i), 0))
    state_spec = pl.BlockSpec((None, hh, GDN_DIM, GDN_DIM), lambda b, i: (b, 0, 0, 0))
    in_specs = [tile, tile, tile,
                pl.BlockSpec((None, t_blk, 128), lambda b, i: (b, blk(i), 0)),
                pl.BlockSpec((1, 128), lambda b, i: (0, 0)),
                pl.BlockSpec((1, 128), lambda b, i: (0, 0)),
                state_spec]
    args = [q, k, v, gates, alog_row, dtb_row, s0]
    tok = lambda w: pltpu.VMEM((t_blk, w), F32)
    return pl.pallas_call(
        functools.partial(_gdn_kernel, rev, t_blk, nb),
        grid=(b, nb),
        in_specs=in_specs,
        out_specs=[tile, state_spec],
        out_shape=[jax.ShapeDtypeStruct((b, seq, width), F32),
                   jax.ShapeDtypeStruct((b, hh, GDN_DIM, GDN_DIM), F32)],
        scratch_shapes=[pltpu.VMEM((hh, GDN_DIM, GDN_DIM), F32),
                        tok(2 * width),
                        tok(width), tok(width), tok(width), tok(width), tok(hh * GDN_CHUNK),
                        pltpu.VMEM((t_blk // GDN_CHUNK, hh, 8, GDN_DIM), F32),
                        pltpu.VMEM((t_blk // GDN_CHUNK, 128, GDN_CHUNK), F32)],
        compiler_params=_params("parallel", "arbitrary"),
        name="gdn_bwd" if rev else "gdn_fwd",
    )(*args)


def _s5_ops_kernel(lr_ref, li_ref, ls_ref, btr_ref, bti_ref, cr_ref, ci_ref,
                   m_ref, winr_ref, wini_ref, woutr_ref, wouti_ref, a16r_ref, a16i_ref):
    d = pl.program_id(0)
    lr = lr_ref[...]
    li = li_ref[...]
    dt = jnp.exp(ls_ref[...])

    def apow(kk):
        mag = jnp.exp(lr * dt * kk)
        ang = li * dt * kk
        return mag * jnp.cos(ang), mag * jnp.sin(ang)

    def cmul(xr, xi, yr, yi):
        return xr * yr - xi * yi, xr * yi + xi * yr

    ar, ai = apow(1.0)
    den = lr * lr + li * li
    nr, ni = ar - 1.0, ai
    zr = (nr * lr + ni * li) / den
    zi = (ni * lr - nr * li) / den
    bbr, bbi = cmul(zr, zi, btr_ref[...], bti_ref[...])
    cr, ci = cr_ref[...], ci_ref[...]

    t16 = lax.broadcasted_iota(jnp.int32, (S5_CHUNK, 1), 0)
    tv16 = jnp.where(d == 0, t16, S5_CHUNK - 1 - t16).astype(F32)
    t_row = lax.broadcasted_iota(jnp.int32, (S5_ROW, 1), 0) // S5_GROUP
    t_col = lax.broadcasted_iota(jnp.int32, (1, S5_ROW), 1) // S5_GROUP
    tv_row = jnp.where(d == 0, t_row, S5_CHUNK - 1 - t_row)
    tv_col = jnp.where(d == 0, t_col, S5_CHUNK - 1 - t_col)
    tile = lambda x: jnp.concatenate([x] * S5_CHUNK, axis=0)
    spread = (lax.broadcasted_iota(jnp.int32, (S5_ROW, S5_CHUNK), 0) // S5_GROUP
              == lax.broadcasted_iota(jnp.int32, (S5_ROW, S5_CHUNK), 1)).astype(BF16)
    rep = lambda x: _mm_exact_lhs(spread, x)
    pos_r, pos_i = apow(tv16)
    neg_r, neg_i = apow(-tv16)
    xr, xi = cmul(tile(bbr), tile(bbi), rep(neg_r), rep(neg_i))
    yr, yi = cmul(tile(cr), tile(ci), rep(pos_r), rep(pos_i))
    nt = lambda p, q: lax.dot_general(p, q, (((1,), (1,)), ((), ())), precision=HIGHEST,
                                      preferred_element_type=F32)
    m = nt(xr, yr) - nt(xi, yi)
    m_ref[...] = jnp.where(tv_col >= tv_row, m, 0.0)

    p15r, p15i = apow(S5_CHUNK - 1.0)
    winr_ref[...], wini_ref[...] = cmul(xr, xi, p15r, p15i)
    wr, wi = cmul(yr, yi, ar, ai)
    woutr_ref[...] = wr
    wouti_ref[...] = -wi
    a16r, a16i = apow(float(S5_CHUNK))
    a16r_ref[...] = a16r
    a16i_ref[...] = a16i


def _s5_ops_call(lam_re, lam_im, log_step, bt_re, bt_im, c_re, c_im):
    g, p, cg = S5_GROUPS, S5_STATE, S5_GROUP
    ls = jnp.broadcast_to(log_step[:, :, None, None], (2, g, 1, p))
    lam_spec = pl.BlockSpec((None, None, 1, p), lambda d, gi: (d, gi, 0, 0))
    par_spec = pl.BlockSpec((None, cg, p), lambda d, gi: (gi, 0, 0))
    out = lambda *shape: pl.BlockSpec((None, None) + shape, lambda d, gi: (d, gi, 0, 0))
    return pl.pallas_call(
        _s5_ops_kernel,
        grid=(2, g),
        in_specs=[lam_spec, lam_spec, lam_spec, par_spec, par_spec, par_spec, par_spec],
        out_specs=[out(S5_ROW, S5_ROW), out(S5_ROW, p), out(S5_ROW, p), out(S5_ROW, p), out(S5_ROW, p),
                   out(1, p), out(1, p)],
        out_shape=[jax.ShapeDtypeStruct((2, g, S5_ROW, S5_ROW), F32)]
        + [jax.ShapeDtypeStruct((2, g, S5_ROW, p), F32)] * 4
        + [jax.ShapeDtypeStruct((2, g, 1, p), F32)] * 2,
        compiler_params=_params("parallel", "parallel"),
        name="s5_ops",
    )(lam_re.reshape(2, g, 1, p), lam_im.reshape(2, g, 1, p), ls, bt_re, bt_im, c_re, c_im)


def _regroup_kernel(to_groups, rows, *refs):
    units = 128 // S5_GROUP
    cols_per_t = S5_WIDTH // 128
    cols_per_g = S5_ROW // 128
    tok_refs = refs[:cols_per_t] if to_groups else refs[1:]
    row_ref = refs[cols_per_t] if to_groups else refs[0]
    unit = lax.broadcasted_iota(jnp.int32, (rows, 128), 1) // S5_GROUP

    def token_slab(t, col):
        return tok_refs[col].at[pl.ds(t, rows, stride=S5_CHUNK), :]

    def gather(pieces, src_unit):
        acc = None
        for p in range(units):
            shift = ((p - src_unit[p]) * S5_GROUP) % 128
            piece = pltpu.roll(pieces[p], shift, axis=1) if shift else pieces[p]
            acc = piece if acc is None else jnp.where(unit == p, piece, acc)
        return acc

    if to_groups:
        for gh in range(cols_per_t):
            for t_hi in range(cols_per_g):
                slabs = [token_slab(units * t_hi + p, gh)[...] for p in range(units)]
                for q in range(units):
                    j = (units * gh + q) * cols_per_g + t_hi
                    row_ref[:, j * 128:(j + 1) * 128] = gather(slabs, [q] * units).astype(row_ref.dtype)
    else:
        for j in range(S5_CHUNK * cols_per_t):
            t, gh = j // cols_per_t, j % cols_per_t
            cols = [(units * gh + p) * cols_per_g + t // units for p in range(units)]
            pieces = [row_ref[:, col * 128:(col + 1) * 128] for col in cols]
            token_slab(t, gh)[...] = gather(pieces, [t % units] * units)


def _regroup_call(x, to_groups, dtype):
    b = x.shape[0]
    nc = x.shape[1] // S5_CHUNK if to_groups else x.shape[1]
    rows = min(64, nc)
    ncol = S5_WIDTH // 128
    row_spec = pl.BlockSpec((None, rows, S5_CHUNK * S5_WIDTH), lambda b, i: (b, i, 0))
    row_shape = jax.ShapeDtypeStruct((b, nc, S5_CHUNK * S5_WIDTH), dtype)
    if to_groups:
        in_specs = [pl.BlockSpec((None, rows * S5_CHUNK, 128), functools.partial(lambda b, i, col: (b, i, col), col=col))
                    for col in range(ncol)]
        args, out_specs, out_shape = [x] * ncol, row_spec, row_shape
    else:
        in_specs, args = [row_spec], [x]
        out_specs = [pl.BlockSpec((None, rows * S5_CHUNK, 128), lambda b, i: (b, i, 0))] * ncol
        out_shape = [jax.ShapeDtypeStruct((b, nc * S5_CHUNK, 128), dtype)] * ncol
    return pl.pallas_call(
        functools.partial(_regroup_kernel, to_groups, rows),
        grid=(b, nc // rows),
        in_specs=in_specs,
        out_specs=out_specs,
        out_shape=out_shape,
        compiler_params=_params("parallel", "parallel"),
        name="s5_to_groups" if to_groups else "s5_to_tokens",
    )(*args)


def _s5_in_kernel(u_ref, wr_ref, wi_ref, sr_ref, si_ref):
    dot = functools.partial(jnp.dot, preferred_element_type=F32)
    u0, u1 = u_ref[:, :S5_ROW], u_ref[:, S5_ROW:]
    sr_ref[...] = dot(u0, wr_ref[0]) + dot(u1, wr_ref[1])
    si_ref[...] = dot(u0, wi_ref[0]) + dot(u1, wi_ref[1])


def _s5_in_call(ug, win_r, win_i):
    b, nc, _ = ug.shape
    g = S5_GROUPS
    w_spec = pl.BlockSpec((None, 2, S5_ROW, 128), lambda d, b, gp: (d, gp, 0, 0))
    o_spec = pl.BlockSpec((None, None, nc, 128), lambda d, b, gp: (d, b, 0, gp))
    return pl.pallas_call(
        _s5_in_kernel,
        grid=(2, b, g // 2),
        in_specs=[pl.BlockSpec((None, nc, 2 * S5_ROW), lambda d, b, gp: (b, 0, gp)), w_spec, w_spec],
        out_specs=[o_spec, o_spec],
        out_shape=[jax.ShapeDtypeStruct((2, b, nc, S5_LANES), F32)] * 2,
        compiler_params=_params("parallel", "parallel", "parallel"),
        name="s5_in",
    )(ug, win_r, win_i)


def _s5_scan_kernel(nc, sr_ref, si_ref, ar_ref, ai_ref, h0r_ref, h0i_ref, hr_ref, hi_ref, fr_ref, fi_ref):
    coef = [(ar_ref[d], ai_ref[d]) for d in range(2)]

    def body(i, carry):
        out = []
        for d in range(2):
            hr, hi = carry[d]
            ar, ai = coef[d]
            n = i if d == 0 else nc - 1 - i
            hr_ref[d, pl.ds(n, 1), :] = hr
            hi_ref[d, pl.ds(n, 1), :] = hi
            sr = sr_ref[d, pl.ds(n, 1), :]
            si = si_ref[d, pl.ds(n, 1), :]
            out.append((ar * hr - ai * hi + sr, ar * hi + ai * hr + si))
        return tuple(out)

    fin = lax.fori_loop(0, nc, body, tuple((h0r_ref[d], h0i_ref[d]) for d in range(2)), unroll=4)
    for d in range(2):
        fr_ref[d] = fin[d][0]
        fi_ref[d] = fin[d][1]


def _s5_scan_call(s_re, s_im, a16_re, a16_im, h0_re, h0_im):
    _, b, nc, _ = s_re.shape
    tl = 512
    big = pl.BlockSpec((2, None, nc, tl), lambda b, j: (0, b, 0, j))
    a_spec = pl.BlockSpec((2, 1, tl), lambda b, j: (0, 0, j))
    st_spec = pl.BlockSpec((2, None, 1, tl), lambda b, j: (0, b, 0, j))
    return pl.pallas_call(
        functools.partial(_s5_scan_kernel, nc),
        grid=(b, S5_LANES // tl),
        in_specs=[big, big, a_spec, a_spec, st_spec, st_spec],
        out_specs=[big, big, st_spec, st_spec],
        out_shape=[jax.ShapeDtypeStruct(s_re.shape, F32)] * 2
        + [jax.ShapeDtypeStruct((2, b, 1, S5_LANES), F32)] * 2,
        compiler_params=_params("parallel", "parallel"),
        name="s5_scan",
    )(s_re, s_im, a16_re, a16_im, h0_re, h0_im)


def _s5_out_kernel(u_ref, m_ref, hr_ref, hi_ref, wr_ref, wi_ref, y_ref):
    u = u_ref[...]
    acc = None
    for d in range(2):
        y = (jnp.dot(u, m_ref[d], preferred_element_type=F32)
             + _mm(hr_ref[d], wr_ref[d]) + _mm(hi_ref[d], wi_ref[d]))
        acc = y if acc is None else acc + y
    y_ref[...] = acc


def _s5_out_call(ug, m, h_re, h_im, wout_r, wout_i):
    b, nc, _ = ug.shape
    g = S5_GROUPS
    h_spec = pl.BlockSpec((2, None, nc, 128), lambda b, gi: (0, b, 0, gi // 2))
    w_spec = pl.BlockSpec((2, None, 128, S5_ROW), lambda b, gi: (0, gi, 0, 0))
    u_spec = pl.BlockSpec((None, nc, S5_ROW), lambda b, gi: (b, 0, gi))
    return pl.pallas_call(
        _s5_out_kernel,
        grid=(b, g),
        in_specs=[u_spec,
                  pl.BlockSpec((2, None, S5_ROW, S5_ROW), lambda b, gi: (0, gi, 0, 0)),
                  h_spec, h_spec, w_spec, w_spec],
        out_specs=u_spec,
        out_shape=jax.ShapeDtypeStruct(ug.shape, F32),
        compiler_params=_params("parallel", "parallel"),
        name="s5_out",
    )(ug, m, h_re, h_im, wout_r, wout_i)


def _s5_prepare_ops(lam_re, lam_im, log_step, b_re, b_im, c_re, c_im):
    m, win_r, win_i, wout_r, wout_i, a16_r, a16_i = _s5_ops_call(
        lam_re, lam_im, log_step, jnp.swapaxes(b_re, 1, 2), jnp.swapaxes(b_im, 1, 2), c_re, c_im)
    p = S5_STATE
    odd = (jnp.arange(S5_GROUPS) % 2 == 1)[None, :, None, None]

    def pad_in(w):
        z = jnp.zeros_like(w)
        return jnp.where(odd, jnp.concatenate([z, w], -1), jnp.concatenate([w, z], -1)).astype(BF16)

    def pad_out(w):
        wt = jnp.swapaxes(w, 2, 3)
        z = jnp.zeros_like(wt)
        return jnp.where(odd, jnp.concatenate([z, wt], 2), jnp.concatenate([wt, z], 2)).astype(BF16)

    return dict(m=m.astype(BF16), win_r=pad_in(win_r), win_i=pad_in(win_i),
                wout_r=pad_out(wout_r), wout_i=pad_out(wout_i),
                a16_r=a16_r.reshape(2, 1, S5_LANES), a16_i=a16_i.reshape(2, 1, S5_LANES))


def _s5_mix(u, ops, h0_re, h0_im):
    b, seq, _ = u.shape
    nc = seq // S5_CHUNK
    ug = _regroup_call(u, True, BF16)
    s_re, s_im = _s5_in_call(ug, ops["win_r"], ops["win_i"])
    h_re, h_im, f_re, f_im = _s5_scan_call(s_re, s_im, ops["a16_r"], ops["a16_i"], h0_re, h0_im)
    y = _s5_out_call(ug, ops["m"], h_re, h_im, ops["wout_r"], ops["wout_i"])
    y = _regroup_call(y, False, F32)
    return y, f_re, f_im


def _conv_kernel(rows, width, n_hor, has_ver, nt, *refs):
    if has_ver:
        (xh_ref, vp_ref, vc_ref, vn_ref, wh_ref, wv_ref, b_ref, g_ref, be_ref,
         o_ref, hs_scr, vs_scr) = refs
    else:
        xh_ref, wh_ref, b_ref, g_ref, be_ref, o_ref, hs_scr = refs
    i = pl.program_id(1)
    tokens = rows * width
    slot = width + 2 * CONV_PAD
    half = CONV_K // 2

    hs_scr[...] = jnp.zeros_like(hs_scr)
    for r in range(rows):
        hs_scr[r, pl.ds(CONV_PAD, width), :] = xh_ref[pl.ds(r * width, width), :]
    wh = wh_ref[...]
    acc_h = jnp.zeros((tokens, n_hor), F32)
    for tap in range(CONV_K):
        win = hs_scr[:, pl.ds(CONV_PAD - half + tap, width), :].reshape(tokens, n_hor)
        acc_h = acc_h + win * wh[tap:tap + 1, :]

    if has_ver:
        vs_scr[pl.ds(0, tokens), :] = jnp.where(i == 0, 0.0, vp_ref[...])
        vs_scr[pl.ds(tokens, tokens), :] = vc_ref[...]
        vs_scr[pl.ds(2 * tokens, tokens), :] = jnp.where(i == nt - 1, 0.0, vn_ref[...])
        wv = wv_ref[...]
        acc_v = jnp.zeros((tokens, CONV_WIDTH - n_hor), F32)
        for tap in range(CONV_K):
            acc_v = acc_v + vs_scr[pl.ds(tokens + (tap - half) * width, tokens), :] * wv[tap:tap + 1, :]
        x = jnp.concatenate([acc_h, acc_v], axis=1)
    else:
        x = acc_h
    x = x + b_ref[...]
    xc = x - jnp.mean(x, axis=-1, keepdims=True)
    y = xc * lax.rsqrt(jnp.mean(xc * xc, axis=-1, keepdims=True) + EPS) * g_ref[...] + be_ref[...]
    o_ref[...] = _silu(y)


def _conv_call(xc, dw_w, dw_b, ln_g, ln_b, grid_rows):
    b, seq, ch = xc.shape
    vec = pl.BlockSpec((1, ch), lambda b, i: (0, 0))
    if grid_rows > 0:
        width, rows, n_hor = GRID_W, 16, ch // 2
        tokens = rows * width
        nt = seq // tokens
        half_spec = lambda f: pl.BlockSpec((None, tokens, n_hor), f)
        in_specs = [half_spec(lambda b, i: (b, i, 0)),
                    half_spec(lambda b, i: (b, jnp.maximum(i - 1, 0), 1)),
                    half_spec(lambda b, i: (b, i, 1)),
                    half_spec(lambda b, i: (b, jnp.minimum(i + 1, nt - 1), 1)),
                    pl.BlockSpec((CONV_K, n_hor), lambda b, i: (0, 0)),
                    pl.BlockSpec((CONV_K, n_hor), lambda b, i: (0, 1)),
                    vec, vec, vec]
        args = [xc, xc, xc, xc, dw_w, dw_w, dw_b, ln_g, ln_b]
        scratch = [pltpu.VMEM((rows, width + 2 * CONV_PAD, n_hor), F32),
                   pltpu.VMEM((3 * tokens, ch - n_hor), F32)]
        has_ver = True
    else:
        width, rows, n_hor = seq, 1, ch
        tokens = seq
        nt = 1
        in_specs = [pl.BlockSpec((None, tokens, ch), lambda b, i: (b, 0, 0)),
                    pl.BlockSpec((CONV_K, ch), lambda b, i: (0, 0)),
                    vec, vec, vec]
        args = [xc, dw_w, dw_b, ln_g, ln_b]
        scratch = [pltpu.VMEM((rows, width + 2 * CONV_PAD, n_hor), F32)]
        has_ver = False
    return pl.pallas_call(
        functools.partial(_conv_kernel, rows, width, n_hor, has_ver, nt),
        grid=(b, nt),
        in_specs=in_specs,
        out_specs=pl.BlockSpec((None, tokens, ch), lambda b, i: (b, i, 0)),
        out_shape=jax.ShapeDtypeStruct((b, seq, ch), F32),
        scratch_shapes=scratch,
        compiler_params=_params("parallel", "parallel"),
        name="conformer_conv",
    )(*args)


def _lb_kernel(x_ref, o_ref):
    x = x_ref[...]
    n = x.shape[0]
    rows = [x[r:r + 1, :] for r in range(n)]
    mx = functools.reduce(jnp.maximum, rows)
    ex = [jnp.exp(r - mx) for r in rows]
    tot = functools.reduce(lambda p, q: p + q, ex)
    run = None
    for r in range(n):
        run = ex[r] / tot if run is None else run + ex[r] / tot
        o_ref[pl.ds(r, 1), :] = run


def _lb_call(logits):
    return pl.pallas_call(_lb_kernel, out_shape=jax.ShapeDtypeStruct(logits.shape, F32), name="hgrn_lb")(logits)


def _hgrn_kernel(rev, t_blk, nb, q_ref, f_ref, v_ref, lb_ref, s0_ref, o_ref, sfin_ref,
                 st_scr, kv_scr, sall_scr):
    i = pl.program_id(1)
    c = HGRN_CHUNK
    nch = t_blk // c
    hh = HGRN_HEADS

    @pl.when(i == 0)
    def _():
        st_scr[...] = s0_ref[...]

    lb = lb_ref[...]
    f = lb + (1.0 - lb) * jax.nn.sigmoid(f_ref[...])
    k = 1.0 - f
    log_f = jnp.log(f)
    row = lax.broadcasted_iota(jnp.int32, (t_blk, t_blk), 0)
    col = lax.broadcasted_iota(jnp.int32, (t_blk, t_blk), 1)
    same = _same_block(row, col, c)
    incl = same & ((row <= col) if rev else (row >= col))
    bc = _mm_exact_lhs(incl.astype(BF16), log_f)
    last = 0 if rev else c - 1
    bc3 = bc.reshape(nch, c, bc.shape[1])
    tot = jnp.broadcast_to(bc3[:, last:last + 1, :], bc3.shape).reshape(bc.shape)
    q_in = q_ref[...] * jnp.exp(bc)
    k_in = k * jnp.exp(-bc)
    k_out = k * jnp.exp(tot - bc)
    cd = jnp.exp(tot)
    v = v_ref[...]
    order = range(nch - 1, -1, -1) if rev else range(nch)
    heads = range(hh)
    ls = [slice(hd * HGRN_DIM, (hd + 1) * HGRN_DIM) for hd in heads]
    attn = [jnp.where(incl, _mm_nt(q_in[:, ls[hd]], k_in[:, ls[hd]]), 0.0) for hd in heads]
    o_intra = [_mm(attn[hd], v[:, ls[hd]]) for hd in heads]
    for n in range(nch):
        rows = slice(n * c, (n + 1) * c)
        for hd in heads:
            kv_scr[hd, n] = _mm_tn(v[rows, ls[hd]], k_out[rows, ls[hd]])
    st = [st_scr[hd] for hd in heads]
    for n in order:
        for hd in heads:
            sall_scr[hd, n] = st[hd]
            st[hd] = st[hd] * cd[n * c:n * c + 1, ls[hd]] + kv_scr[hd, n]
    for hd in heads:
        st_scr[hd] = st[hd]
    grp = 4
    blk = lax.broadcasted_iota(jnp.int32, (grp * c, HGRN_DIM), 0) // c
    for n in range(0, nch, grp):
        rows = slice(n * c, (n + grp) * c)
        for hd in heads:
            qg = q_in[rows, ls[hd]]
            q_bd = jnp.concatenate([jnp.where(blk == j, qg, 0.0) for j in range(grp)], axis=1)
            s_cat = jnp.concatenate([sall_scr[hd, n + j] for j in range(grp)], axis=1)
            o_ref[pl.ds(n * c, grp * c), pl.ds(hd * HGRN_DIM, HGRN_DIM)] = _mm_nt(q_bd, s_cat) + o_intra[hd][rows]

    @pl.when(i == nb - 1)
    def _():
        sfin_ref[...] = st_scr[...]


def _hgrn_call(q, fgate, v, lb_row, s0, rev):
    b, seq, _ = q.shape
    t_blk = min(256, seq)
    nb = seq // t_blk
    hh = HGRN_HEADS

    def blk(i):
        return (nb - 1 - i) if rev else i

    tok = pl.BlockSpec((None, t_blk, HGRN_WIDTH), lambda b, i: (b, blk(i), 0))
    state_spec = pl.BlockSpec((None, hh, HGRN_DIM, HGRN_DIM), lambda b, i: (b, 0, 0, 0))
    chunk_states = pltpu.VMEM((hh, t_blk // HGRN_CHUNK, HGRN_DIM, HGRN_DIM), F32)
    return pl.pallas_call(
        functools.partial(_hgrn_kernel, rev, t_blk, nb),
        grid=(b, nb),
        in_specs=[tok, tok, tok, pl.BlockSpec((1, HGRN_WIDTH), lambda b, i: (0, 0)), state_spec],
        out_specs=[tok, state_spec],
        out_shape=[jax.ShapeDtypeStruct((b, seq, HGRN_WIDTH), F32),
                   jax.ShapeDtypeStruct((b, hh, HGRN_DIM, HGRN_DIM), F32)],
        scratch_shapes=[pltpu.VMEM((hh, HGRN_DIM, HGRN_DIM), F32), chunk_states, chunk_states],
        compiler_params=_params("parallel", "arbitrary"),
        name="hgrn_bwd" if rev else "hgrn_fwd",
    )(q, fgate, v, lb_row, s0)


def _mixer_ab(h, hc, mod, layer, need_ctx, prm):
    ng = prm["norm_g1"]
    outs = {}
    gdn_state = [jnp.zeros((h.shape[0], GDN_HEADS, GDN_DIM, GDN_DIM), F32)] * 2
    s5_state = [jnp.zeros((2, h.shape[0], 1, S5_LANES), F32)] * 2
    for ctx, x in ((True, hc), (False, h)):
        q, k, v, z, u, gates = _inproj_ab_call(x, mod, layer, ctx, ng, prm["w_in"], prm["conv_w"])
        o_f, sf = _gdn_call(q, k, v, gates, prm["alog"], prm["dtb"], gdn_state[0], rev=False)
        o_b, sb = _gdn_call(q, k, v, gates, prm["alog"], prm["dtb"], gdn_state[1], rev=True)
        y5, f_re, f_im = _s5_mix(u, prm["s5"], s5_state[0], s5_state[1])
        gdn_state = [sf, sb]
        s5_state = [f_re, f_im]
        if ctx and not need_ctx:
            continue
        outs[ctx] = ("ab", [o_f, o_b, z, *y5, u],
                     [prm["gdn_g"], prm["s5_d"], prm["glu_w"], prm["glu_b"], prm["w_out"]])
    return outs[False], outs.get(True)


def _mixer_cd(h, hc, mod, layer, need_ctx, prm, grid_rows):
    ng = prm["norm_g1"]
    outs = {}
    state = [jnp.zeros((h.shape[0], HGRN_HEADS, HGRN_DIM, HGRN_DIM), F32)] * 2
    for ctx, x in ((True, hc), (False, h)):
        xc, q, f_f, f_b, iv, g = _inproj_call(x, mod, layer, ctx, ng, prm["w_in"], prm["pieces"], glu=True)
        o_f, sf = _hgrn_call(q, f_f, iv, prm["lb"], state[0], rev=False)
        o_b, sb = _hgrn_call(q, f_b, iv, prm["lb"], state[1], rev=True)
        state = [sf, sb]
        if ctx and not need_ctx:
            continue
        cmix = _conv_call(xc, prm["dw_w"], prm["dw_b"], prm["ln_g"], prm["ln_b"], 0 if ctx else grid_rows)
        outs[ctx] = ("cd", [cmix, o_f, o_b, g], [prm["hgrn_g"], prm["w_out"]])
    return outs[False], outs.get(True)


def kernel(x, c, ctx, c_ctx, ada_w, ada_b, norm_g, ffn_w_up, ffn_w_down, ab_w_in, ab_w_out, gdn_conv_w, gdn_a_log, gdn_dt_bias, gdn_norm_g, s5_lambda_re, s5_lambda_im, s5_log_step, s5_b_re, s5_b_im, s5_c_re, s5_c_im, s5_d, s5_glu_w, s5_glu_b, cd_w_in, cd_w_out, conv_dw_w, conv_dw_b, conv_ln_g, conv_ln_b, hgrn_lb_logits, hgrn_norm_g, final_norm_g):
    depth = ada_w.shape[0]
    batch = x.shape[0]
    grid_rows = x.shape[1] // GRID_W
    assert batch <= 2 and x.shape[1] % 1024 == 0 and ctx.shape[1] % 256 == 0

    cvec = jnp.zeros((8, D_MODEL), F32).at[:batch].set(c).at[2].set(c_ctx)
    mod = _ada_call(cvec, ada_w, ada_b)
    lb_all = _lb_call(hgrn_lb_logits)
    row = lambda v: v.reshape(1, -1)
    gw, kw = GDN_WIDTH, HGRN_WIDTH

    w_up = ffn_w_up.astype(BF16)
    w_down = ffn_w_down.astype(BF16)
    h, hc = x, ctx
    for l in range(depth):
        last = l == depth - 1
        h = _ffn_call(h, mod, l, 0, False, row(norm_g[l, 0]), w_up, w_down, (l, 0))
        hc = _ffn_call(hc, mod, l, 0, True, row(norm_g[l, 0]), w_up, w_down, (l, 0))
        if l % 2 == 0:
            e = l // 2
            w = ab_w_in[e]
            ng4 = 4 * GDN_HEADS
            w = jnp.concatenate([w[:, :4 * gw], w[:, 4 * gw + ng4:], w[:, 4 * gw:4 * gw + ng4],
                                 jnp.zeros((D_MODEL, 128 - ng4), F32)], axis=1).astype(BF16)
            pad8 = lambda v: jnp.zeros((1, 128), F32).at[0, :2 * GDN_HEADS].set(v.reshape(-1))
            prm = dict(
                norm_g1=row(norm_g[l, 1]),
                w_in=w, conv_w=gdn_conv_w[e], alog=pad8(gdn_a_log[e]), dtb=pad8(gdn_dt_bias[e]),
                gdn_g=row(gdn_norm_g[e]),
                s5=_s5_prepare_ops(s5_lambda_re[e], s5_lambda_im[e], s5_log_step[e], s5_b_re[e], s5_b_im[e],
                                   s5_c_re[e], s5_c_im[e]),
                s5_d=row(s5_d[e]), glu_w=s5_glu_w[e].astype(BF16), glu_b=row(s5_glu_b[e]),
                w_out=ab_w_out[e].astype(BF16))
            mix, mix_ctx = _mixer_ab(h, hc, mod, l, not last, prm)
        else:
            o = l // 2
            cw = CONV_WIDTH
            prm = dict(
                norm_g1=row(norm_g[l, 1]),
                w_in=cd_w_in[o].astype(BF16),
                pieces=[(k * cw, cw) for k in range(2)] + [(2 * cw + k * kw, kw) for k in range(5)],
                lb=lb_all[o:o + 1], hgrn_g=row(hgrn_norm_g[o]),
                dw_w=conv_dw_w[o], dw_b=row(conv_dw_b[o]), ln_g=row(conv_ln_g[o]), ln_b=row(conv_ln_b[o]),
                w_out=cd_w_out[o].astype(BF16))
            mix, mix_ctx = _mixer_cd(h, hc, mod, l, not last, prm, grid_rows)
        h = _ffn_call(h, mod, l, 2, False, row(norm_g[l, 2]), w_up, w_down, (l, 1),
                      final_g=row(final_norm_g) if last else None, mixer=mix)
        if not last:
            hc = _ffn_call(hc, mod, l, 2, True, row(norm_g[l, 2]), w_up, w_down, (l, 1), mixer=mix_ctx)
    return h
```

```python
import functools
import math

import jax
import jax.numpy as jnp
from jax import lax
from jax.experimental import pallas as pl
from jax.experimental.pallas import tpu as pltpu

F32 = jnp.float32
BF16 = jnp.bfloat16
HIGHEST = lax.Precision.HIGHEST
EPS = 1e-6

D_MODEL = 1024
GRID_W = 64
FFN_DIM = 2816

GDN_HEADS = 4
GDN_DIM = 128
GDN_WIDTH = GDN_HEADS * GDN_DIM
GDN_CONV = 5
GDN_CHUNK = 256

S5_WIDTH = 512
S5_GROUP = 16
S5_GROUPS = 32
S5_STATE = 64
S5_CHUNK = 16
S5_ROW = S5_CHUNK * S5_GROUP
S5_LANES = S5_GROUPS * S5_STATE

CONV_WIDTH = 512
CONV_K = 31
CONV_PAD = 16

HGRN_HEADS = 4
HGRN_DIM = 128
HGRN_WIDTH = HGRN_HEADS * HGRN_DIM
HGRN_CHUNK = 16

V7X_VMEM_LIMIT = 48 * 1024 * 1024


def _params(*sem):
    return pltpu.CompilerParams(dimension_semantics=sem, vmem_limit_bytes=V7X_VMEM_LIMIT)


def _silu(x):
    return x * jax.nn.sigmoid(x)


def _mm(a, b):
    return jnp.dot(a.astype(BF16), b.astype(BF16), preferred_element_type=F32)


def _mm_nt(a, b):
    return lax.dot_general(a.astype(BF16), b.astype(BF16), (((1,), (1,)), ((), ())),
                           preferred_element_type=F32)


def _mm_tn(a, b):
    return lax.dot_general(a.astype(BF16), b.astype(BF16), (((0,), (0,)), ((), ())),
                           preferred_element_type=F32)


def _split3(x):
    hi = x.astype(BF16)
    r1 = x - hi.astype(F32)
    mid = r1.astype(BF16)
    lo = (r1 - mid.astype(F32)).astype(BF16)
    return hi, mid, lo


def _mm_exact_lhs(a_bf16, x):
    return jnp.dot(jnp.concatenate([a_bf16] * 3, axis=1), jnp.concatenate(_split3(x), axis=0),
                   preferred_element_type=F32)


def _mm_exact_rhs(x, b_bf16):
    return jnp.dot(jnp.concatenate(_split3(x), axis=1), jnp.concatenate([b_bf16] * 3, axis=0),
                   preferred_element_type=F32)


def _mm_x3(a, b):
    ah = a.astype(BF16)
    al = (a - ah.astype(F32)).astype(BF16)
    bh = b.astype(BF16)
    bl = (b - bh.astype(F32)).astype(BF16)
    return jnp.dot(jnp.concatenate([ah, ah, al], axis=1), jnp.concatenate([bh, bl, bh], axis=0),
                   preferred_element_type=F32)


def _same_block(i, j, size):
    return (i ^ j) < size


def _ada_norm(x, g, scale, shift):
    y = x * lax.rsqrt(jnp.mean(x * x, axis=-1, keepdims=True) + EPS) * g
    return y * (1.0 + scale) + shift


def _ada_kernel(c_ref, w_ref, b_ref, o_ref):
    o_ref[...] = _mm(_silu(c_ref[...]), w_ref[...]) + b_ref[...]


def _ada_call(cvec, ada_w, ada_b):
    depth = ada_w.shape[0]
    ncol = ada_w.shape[2] // D_MODEL
    out = pl.pallas_call(
        _ada_kernel,
        grid=(depth, ncol),
        in_specs=[pl.BlockSpec((8, D_MODEL), lambda l, j: (0, 0)),
                  pl.BlockSpec((None, D_MODEL, D_MODEL), lambda l, j: (l, 0, j)),
                  pl.BlockSpec((None, 1, D_MODEL), lambda l, j: (l, 0, j))],
        out_specs=pl.BlockSpec((None, None, 8, D_MODEL), lambda l, j: (l, j, 0, 0)),
        out_shape=jax.ShapeDtypeStruct((depth, ncol, 8, D_MODEL), F32),
        compiler_params=_params("arbitrary", "arbitrary"),
        name="ada_mod",
    )(cvec, ada_w, ada_b.reshape(depth, 1, -1))
    return out.reshape(depth * ncol * 8, 1, D_MODEL)


def _mod_spec(layer, sub, kind, ctx):
    base = (layer * 9 + sub * 3 + kind) * 8
    if ctx:
        return pl.BlockSpec((None, 1, D_MODEL), lambda b, *_: (base + 2, 0, 0))
    return pl.BlockSpec((None, 1, D_MODEL), lambda b, *_: (base + b, 0, 0))


def _head_norm(o, gate, g):
    outs = []
    for hd in range(o.shape[1] // 128):
        x = o[:, hd * 128:(hd + 1) * 128]
        x = x * lax.rsqrt(jnp.mean(x * x, axis=-1, keepdims=True) + EPS) * g
        outs.append(x * _silu(gate[:, hd * 128:(hd + 1) * 128]))
    return jnp.concatenate(outs, axis=1)


def _mix_ab(of_ref, ob_ref, z_ref, y5a_ref, y5b_ref, y5c_ref, y5d_ref, u_ref, ng_ref, dsk_ref, glw_ref, glb_ref,
            wout_ref):
    a = _head_norm(of_ref[...] + ob_ref[...], z_ref[...], ng_ref[...])
    y5 = jnp.concatenate([y5a_ref[...], y5b_ref[...], y5c_ref[...], y5d_ref[...]], axis=1)
    y = y5 + dsk_ref[...] * u_ref[...]
    y = 0.5 * y * (1.0 + jnp.tanh(math.sqrt(2.0 / math.pi) * (y + 0.044715 * (y * y * y))))
    bmix = y * jax.nn.sigmoid(_mm(y, glw_ref[...]) + glb_ref[...])
    return _mm(jnp.concatenate([a, bmix], axis=1), wout_ref[...])


def _mix_cd(c_ref, of_ref, ob_ref, g_ref, ng_ref, wout_ref):
    dmix = _head_norm(of_ref[...] + ob_ref[...], g_ref[...], ng_ref[...])
    return _mm(jnp.concatenate([c_ref[...], dmix], axis=1), wout_ref[...])


_MIXERS = {"ab": (_mix_ab, 13), "cd": (_mix_cd, 6)}


def _ffn_kernel(tf, final, mixer, h_ref, *refs):
    x = h_ref[...]
    if mixer is not None:
        mix_fn, n_mix = _MIXERS[mixer]
        x = x + refs[0][...] * mix_fn(*refs[1:1 + n_mix])
        refs = refs[1 + n_mix:]
    sh_ref, sc_ref, gt_ref, g_ref, wup_ref, wd_ref = refs[:6]
    if final:
        fg_ref, o_ref = refs[6:]
    else:
        (o_ref,) = refs[6:]
    xn = _ada_norm(x, g_ref[...], sc_ref[...], sh_ref[...]).astype(BF16)
    acc = None
    for j in range(FFN_DIM // tf):
        gate = jnp.dot(xn, wup_ref[:, j * tf:(j + 1) * tf], preferred_element_type=F32)
        up = jnp.dot(xn, wup_ref[:, FFN_DIM + j * tf:FFN_DIM + (j + 1) * tf], preferred_element_type=F32)
        act = (_silu(gate) * up).astype(BF16)
        part = jnp.dot(act, wd_ref[j * tf:(j + 1) * tf, :], preferred_element_type=F32)
        acc = part if acc is None else acc + part
    y = x + 0.5 * gt_ref[...] * acc
    if final:
        y = y * lax.rsqrt(jnp.mean(y * y, axis=-1, keepdims=True) + EPS) * fg_ref[...]
    o_ref[...] = y


def _ffn_call(h, mod, layer, sub, ctx, norm_g_row, w_up, w_down, which, final_g=None, mixer=None):
    b, seq, _ = h.shape
    tm = min(512, seq)
    tf = 256
    final = final_g is not None
    resident = dict(pipeline_mode=pl.Buffered(1))
    in_specs = [pl.BlockSpec((None, tm, D_MODEL), lambda b, i: (b, i, 0))]
    args = [h]
    if mixer is not None:
        kind, toks, consts = mixer
        in_specs.append(_mod_spec(layer, 1, 2, ctx))
        in_specs += [pl.BlockSpec((None, tm, t.shape[2]), lambda b, i: (b, i, 0)) for t in toks]
        in_specs += [pl.BlockSpec(cst.shape, lambda b, i: (0, 0), **resident) for cst in consts]
        args += [mod, *toks, *consts]
        assert 1 + len(toks) + len(consts) == 1 + _MIXERS[kind][1]
    in_specs += [
        _mod_spec(layer, sub, 0, ctx), _mod_spec(layer, sub, 1, ctx), _mod_spec(layer, sub, 2, ctx),
        pl.BlockSpec((1, D_MODEL), lambda b, i: (0, 0)),
        pl.BlockSpec((None, None, D_MODEL, 2 * FFN_DIM), lambda b, i: which + (0, 0), **resident),
        pl.BlockSpec((None, None, FFN_DIM, D_MODEL), lambda b, i: which + (0, 0), **resident),
    ]
    args += [mod, mod, mod, norm_g_row, w_up, w_down]
    if final:
        in_specs.append(pl.BlockSpec((1, D_MODEL), lambda b, i: (0, 0)))
        args.append(final_g)
    return pl.pallas_call(
        functools.partial(_ffn_kernel, tf, final, None if mixer is None else mixer[0]),
        grid=(b, seq // tm),
        in_specs=in_specs,
        out_specs=pl.BlockSpec((None, tm, D_MODEL), lambda b, i: (b, i, 0)),
        out_shape=jax.ShapeDtypeStruct(h.shape, F32),
        compiler_params=_params("parallel", "parallel"),
        name="ffn",
    )(*args)


def _inproj_kernel(nw, glu, h_ref, sh_ref, sc_ref, g_ref, *refs):
    w_refs, o_refs = refs[:nw], refs[nw:]
    xn = _ada_norm(h_ref[...], g_ref[...], sc_ref[...], sh_ref[...]).astype(BF16)
    outs = [jnp.dot(xn, w[...], preferred_element_type=F32) for w in w_refs]
    if glu:
        outs = [outs[0] * jax.nn.sigmoid(outs[1])] + outs[2:]
    for o_ref, val in zip(o_refs, outs):
        o_ref[...] = val


def _inproj_call(h, mod, layer, ctx, norm_g_row, w, pieces, glu):
    b, seq, _ = h.shape
    tm = min(512, seq)
    widths = [wd for _, wd in pieces]
    out_widths = widths[1:] if glu else widths
    in_specs = [pl.BlockSpec((None, tm, D_MODEL), lambda b, i: (b, i, 0)),
                _mod_spec(layer, 1, 0, ctx), _mod_spec(layer, 1, 1, ctx),
                pl.BlockSpec((1, D_MODEL), lambda b, i: (0, 0))]
    for off, wd in pieces:
        assert off % wd == 0
        in_specs.append(pl.BlockSpec((D_MODEL, wd), functools.partial(lambda b, i, blk: (0, blk), blk=off // wd),
                                     pipeline_mode=pl.Buffered(1)))
    return pl.pallas_call(
        functools.partial(_inproj_kernel, len(pieces), glu),
        grid=(b, seq // tm),
        in_specs=in_specs,
        out_specs=[pl.BlockSpec((None, tm, wd), lambda b, i: (b, i, 0)) for wd in out_widths],
        out_shape=[jax.ShapeDtypeStruct((b, seq, wd), F32) for wd in out_widths],
        compiler_params=_params("parallel", "parallel"),
        name="inproj",
    )(h, mod, mod, norm_g_row, *([w] * len(pieces)))


def _inproj_ab_kernel(tm, nt, h_ref, hp_ref, hn_ref, sh_ref, sc_ref, g_ref, wqkv_ref, wz_ref, wu_ref, wg_ref, cw_ref,
                      q_ref, k_ref, v_ref, z_ref, u_ref, gates_ref, pad_scr):
    i = pl.program_id(1)
    norm = lambda x: _ada_norm(x, g_ref[...], sc_ref[...], sh_ref[...]).astype(BF16)
    dot = functools.partial(jnp.dot, preferred_element_type=F32)
    xn = norm(h_ref[...])
    xh = norm(jnp.concatenate([hp_ref[...], hn_ref[...]], axis=0))
    w = cw_ref[...]
    gw = GDN_WIDTH

    def project(part):
        cols = pl.ds(part * gw, gw)
        wp = wqkv_ref[:, part * gw:(part + 1) * gw]
        halo = dot(xh, wp)
        pad_scr[pl.ds(0, 8), cols] = jnp.where(i == 0, 0.0, halo[:8])
        pad_scr[pl.ds(8, tm), cols] = dot(xn, wp)
        pad_scr[pl.ds(8 + tm, 8), cols] = jnp.where(i == nt - 1, 0.0, halo[8:])

    def conv(part):
        acc = None
        for tap in range(GDN_CONV):
            term = (pad_scr[pl.ds(8 - GDN_CONV // 2 + tap, tm), pl.ds(part * gw, gw)]
                    * w[tap:tap + 1, part * gw:(part + 1) * gw])
            acc = term if acc is None else acc + term
        return _silu(acc)

    def l2n_heads(x, scale):
        outs = []
        for hd in range(GDN_HEADS):
            xs = x[:, hd * GDN_DIM:(hd + 1) * GDN_DIM]
            outs.append(xs * (lax.rsqrt(jnp.sum(xs * xs, axis=-1, keepdims=True) + EPS) * scale))
        return jnp.concatenate(outs, axis=1)

    project(0)
    project(1)
    q_ref[...] = l2n_heads(conv(0), GDN_DIM ** -0.5)
    project(2)
    k_ref[...] = l2n_heads(conv(1), 1.0)
    z_ref[...] = dot(xn, wz_ref[...])
    v_ref[...] = conv(2)
    u_ref[...] = dot(xn, wu_ref[...])
    gates_ref[...] = dot(xn, wg_ref[...])


def _inproj_ab_call(h, mod, layer, ctx, norm_g_row, w, conv_w):
    b, seq, _ = h.shape
    tm = min(512, seq)
    nt = seq // tm
    r8 = tm // 8
    gw = GDN_WIDTH
    resident = dict(pipeline_mode=pl.Buffered(1))
    wspec = lambda wd, blk: pl.BlockSpec((D_MODEL, wd), lambda b, i: (0, blk), **resident)
    tok = lambda wd: pl.BlockSpec((None, tm, wd), lambda b, i: (b, i, 0))
    in_specs = [tok(D_MODEL),
                pl.BlockSpec((None, 8, D_MODEL), lambda b, i: (b, jnp.maximum(i * r8 - 1, 0), 0)),
                pl.BlockSpec((None, 8, D_MODEL), lambda b, i: (b, jnp.minimum((i + 1) * r8, seq // 8 - 1), 0)),
                _mod_spec(layer, 1, 0, ctx), _mod_spec(layer, 1, 1, ctx),
                pl.BlockSpec((1, D_MODEL), lambda b, i: (0, 0)),
                wspec(3 * gw, 0), wspec(gw, 3), wspec(S5_WIDTH, 4), wspec(128, (4 * gw + S5_WIDTH) // 128),
                pl.BlockSpec((GDN_CONV, 3 * gw), lambda b, i: (0, 0))]
    widths = [gw, gw, gw, gw, S5_WIDTH, 128]
    return pl.pallas_call(
        functools.partial(_inproj_ab_kernel, tm, nt),
        grid=(b, nt),
        in_specs=in_specs,
        out_specs=[tok(wd) for wd in widths],
        out_shape=[jax.ShapeDtypeStruct((b, seq, wd), F32) for wd in widths],
        scratch_shapes=[pltpu.VMEM((tm + 16, 3 * gw), F32)],
        compiler_params=_params("parallel", "parallel"),
        name="inproj_ab",
    )(h, h, h, mod, mod, norm_g_row, w, w, w, w, conv_w)


def _gdn_kernel(rev, t_blk, nb, q_scr, k_scr, v_scr, gates_ref, alog_ref, dtb_ref, s0_ref,
                o_ref, sfin_ref,
                s_scr, sel_scr, ut_scr, w_scr, qd_scr, kd_scr, qk_scr, cd_scr, gt_scr):
    i = pl.program_id(1)
    c = GDN_CHUNK
    nch = t_blk // c
    hh = GDN_HEADS
    width = hh * GDN_DIM

    @pl.when(i == 0)
    def _():
        s_scr[...] = s0_ref[...]

    gates = gates_ref[...]
    log_a =-jnp.exp(alog_ref[...]) * jax.nn.softplus(gates + dtb_ref[...])
    row = lax.broadcasted_iota(jnp.int32, (c, c), 0)
    col = lax.broadcasted_iota(jnp.int32, (c, c), 1)
    incl = (row <= col) if rev else (row >= col)
    strict = (row < col) if rev else (row > col)
    last = 0 if rev else c - 1
    g_all = jnp.concatenate([_mm_exact_lhs(incl.astype(BF16), log_a[m * c:(m + 1) * c, :]) for m in range(nch)],
                            axis=0)
    for m in range(nch):
        gt_scr[m] = g_all[m * c:(m + 1) * c, :].T
    beta_all = jax.nn.sigmoid(gates)
    d_off = hh if rev else 0
    for hd in range(hh):
        sel_scr[:, pl.ds(hd * 256, 128)] = jnp.broadcast_to(g_all[:, d_off + hd:d_off + hd + 1], (t_blk, 128))
        sel_scr[:, pl.ds(hd * 256 + 128, 128)] = jnp.broadcast_to(
            beta_all[:, 2 * hh + d_off + hd:2 * hh + d_off + hd + 1], (t_blk, 128))

    diag16 = _same_block(row, col, 16)
    levels = []
    d = 16
    while d < c:
        levels.append(_same_block(row, col, 2 * d) & jnp.logical_not(_same_block(row, col, d)))
        d *= 2

    def lanes_c(x):
        return x[:, :c] if c <= 128 else jnp.concatenate([x] * (c // 128), axis=1)

    heads = range(hh)

    def inverse_minus_identity(a_mats):
        idx = range(len(a_mats))
        p = [jnp.where(diag16, -a, 0.0) for a in a_mats]
        n = list(p)
        p = [_mm(x, x) for x in p]
        for it in range(3):
            if it < 2:
                both = [_mm(jnp.concatenate([n[i], p[i]], axis=0), p[i]) for i in idx]
                n = [n[i] + p[i] + both[i][:c] for i in idx]
                p = [both[i][c:] for i in idx]
            else:
                n = [n[i] + p[i] + _mm(n[i], p[i]) for i in idx]
        for mask in levels:
            off = [jnp.where(mask, a, 0.0) for a in a_mats]
            x = [off[i] + _mm(n[i], off[i]) for i in idx]
            n = [n[i] - (x[i] + _mm(x[i], n[i])) for i in idx]
        return n

    def prep(m, carry):
        rows = pl.ds(pl.multiple_of(m * c, c), c)
        lanes = [pl.ds(hd * GDN_DIM, GDN_DIM) for hd in heads]
        q = [q_scr[rows, lanes[hd]] for hd in heads]
        k = [k_scr[rows, lanes[hd]] for hd in heads]
        g = [sel_scr[rows, pl.ds(hd * 256, 128)] for hd in heads]
        beta = [sel_scr[rows, pl.ds(hd * 256 + 128, 128)] for hd in heads]
        decay = []
        for hd in heads:
            diff = lanes_c(g[hd]) - gt_scr[m, pl.ds(d_off + hd, 1), :]
            decay.append(jnp.where(incl, jnp.exp(jnp.where(incl, diff, 0.0)), 0.0))
        kq = [_mm_nt(jnp.concatenate([q[hd], k[hd]], axis=0), k[hd]) for hd in heads]
        a_mats = [jnp.where(strict, kq[hd][c:] * decay[hd], 0.0) * lanes_c(beta[hd]) for hd in heads]
        n_mats = inverse_minus_identity(a_mats)
        for hd in heads:
            eg = jnp.exp(g[hd])
            rhs = jnp.concatenate([v_scr[rows, lanes[hd]] * beta[hd], k[hd] * (beta[hd] * eg)], axis=1)
            sol = rhs + _mm_x3(n_mats[hd], rhs)
            g_last = g[hd][last:last + 1, :]
            ut_scr[rows, lanes[hd]] = sol[:, :GDN_DIM]
            w_scr[rows, lanes[hd]] = sol[:, GDN_DIM:]
            qd_scr[rows, lanes[hd]] = q[hd] * eg
            kd_scr[rows, lanes[hd]] = k[hd] * jnp.exp(g_last - g[hd])
            qk_scr[rows, pl.ds(hd * c, c)] = jnp.where(incl, kq[hd][:c] * decay[hd], 0.0)
            cd_scr[m, hd] = jnp.broadcast_to(jnp.exp(g_last), (8, GDN_DIM))
        return carry

    lax.fori_loop(0, nch, prep, 0, unroll=True)


    def step(ci, carry):
        cidx = (nch - 1 - ci) if rev else ci
        rows = pl.ds(pl.multiple_of(cidx * c, c), c)
        lanes = [pl.ds(hd * GDN_DIM, GDN_DIM) for hd in heads]
        s = [s_scr[hd] for hd in heads]
        ws = [_mm(jnp.concatenate([w_scr[rows, lanes[hd]], qd_scr[rows, lanes[hd]]], axis=0), s[hd]) for hd in heads]
        u = [ut_scr[rows, lanes[hd]] - ws[hd][:c] for hd in heads]
        intra = [_mm(qk_scr[rows, pl.ds(hd * c, c)], u[hd]) for hd in heads]
        outer = [_mm_tn(kd_scr[rows, lanes[hd]], u[hd]) for hd in heads]
        for hd in heads:
            o_ref[rows, lanes[hd]] = ws[hd][c:] + intra[hd]
            s_scr[hd] = s[hd] * cd_scr[cidx, hd][0:1, :] + outer[hd]
        return carry

    lax.fori_loop(0, nch, step, 0)

    @pl.when(i == nb - 1)
    def _():
        sfin_ref[...] = s_scr[...]


def _gdn_call(q, k, v, gates, alog_row, dtb_row, s0, rev):
    b, seq, _ = q.shape
    t_blk = min(512, seq)
    nb = seq // t_blk
    hh = GDN_HEADS
    width = GDN_WIDTH

    def blk(i):
        return (nb - 1 - i) if rev else i

    tile = pl.BlockSpec((None, t_blk, width), lambda b, i: (b, blk(i), 0))
    state_spec = pl.BlockSpec((None, hh, GDN_DIM, GDN_DIM), lambda b, i: (b, 0, 0, 0))
    in_specs = [tile, tile, tile,
                pl.BlockSpec((None, t_blk, 128), lambda b, i: (b, blk(i), 0)),
                pl.BlockSpec((1, 128), lambda b, i: (0, 0)),
                pl.BlockSpec((1, 128), lambda b, i: (0, 0)),
                state_spec]
    args = [q, k, v, gates, alog_row, dtb_row, s0]
    tok = lambda w: pltpu.VMEM((t_blk, w), F32)
    return pl.pallas_call(
        functools.partial(_gdn_kernel, rev, t_blk, nb),
        grid=(b, nb),
        in_specs=in_specs,
        out_specs=[tile, state_spec],
        out_shape=[jax.ShapeDtypeStruct((b, seq, width), F32),
                   jax.ShapeDtypeStruct((b, hh, GDN_DIM, GDN_DIM), F32)],
        scratch_shapes=[pltpu.VMEM((hh, GDN_DIM, GDN_DIM), F32),
                        tok(2 * width),
                        tok(width), tok(width), tok(width), tok(width), tok(hh * GDN_CHUNK),
                        pltpu.VMEM((t_blk // GDN_CHUNK, hh, 8, GDN_DIM), F32),
                        pltpu.VMEM((t_blk // GDN_CHUNK, 128, GDN_CHUNK), F32)],
        compiler_params=_params("parallel", "arbitrary"),
        name="gdn_bwd" if rev else "gdn_fwd",
    )(*args)


def _s5_ops_kernel(lr_ref, li_ref, ls_ref, btr_ref, bti_ref, cr_ref, ci_ref,
                   m_ref, winr_ref, wini_ref, woutr_ref, wouti_ref, a16r_ref, a16i_ref, tab_scr):
    d = pl.program_id(0)
    lr = lr_ref[...]
    li = li_ref[...]
    dt = jnp.exp(ls_ref[...])

    def apow(kk):
        mag = jnp.exp(lr * dt * kk)
        ang = li * dt * kk
        return mag * jnp.cos(ang), mag * jnp.sin(ang)

    def cmul(xr, xi, yr, yi):
        return xr * yr - xi * yi, xr * yi + xi * yr

    ar, ai = apow(1.0)
    den = lr * lr + li * li
    nr, ni = ar - 1.0, ai
    zr = (nr * lr + ni * li) / den
    zi = (ni * lr - nr * li) / den
    bbr, bbi = cmul(zr, zi, btr_ref[...], bti_ref[...])
    cr, ci = cr_ref[...], ci_ref[...]

    t16 = lax.broadcasted_iota(jnp.int32, (S5_CHUNK, 1), 0)
    tv16 = jnp.where(d == 0, t16, S5_CHUNK - 1 - t16).astype(F32)
    tile = lambda x: jnp.concatenate([x] * S5_CHUNK, axis=0)
    spread = (lax.broadcasted_iota(jnp.int32, (S5_ROW, S5_CHUNK), 0) // S5_GROUP
              == lax.broadcasted_iota(jnp.int32, (S5_ROW, S5_CHUNK), 1)).astype(BF16)
    rep = lambda x: _mm_exact_lhs(spread, x)
    pr, pi = apow(S5_CHUNK - 1.0 - tv16)
    xr, xi = cmul(tile(bbr), tile(bbi), rep(pr), rep(pi))
    winr_ref[...] = xr
    wini_ref[...] = xi
    pr, pi = apow(tv16 + 1.0)
    yr, yi = cmul(tile(cr), tile(ci), rep(pr), rep(pi))
    woutr_ref[...] = yr
    wouti_ref[...] = -yi

    nt = lambda p, q: lax.dot_general(p, q, (((1,), (1,)), ((), ())), precision=HIGHEST,
                                      preferred_element_type=F32)
    table = nt(xr, cr) - nt(xi, ci)
    expand = (lax.broadcasted_iota(jnp.int32, (S5_GROUP, S5_ROW), 0)
              == lax.broadcasted_iota(jnp.int32, (S5_GROUP, S5_ROW), 1) % S5_GROUP).astype(BF16)
    tab_scr[...] = jnp.zeros_like(tab_scr)
    tab_scr[pl.ds(S5_ROW, S5_ROW), :] = _mm_exact_rhs(table, expand)
    col_t = lax.broadcasted_iota(jnp.int32, (S5_ROW, S5_ROW), 1) // S5_GROUP
    m = jnp.zeros((S5_ROW, S5_ROW), F32)
    for t in range(S5_CHUNK):
        start = S5_ROW + jnp.where(d == 0, S5_CHUNK - 1 - t, -t) * S5_GROUP
        win = tab_scr[pl.ds(pl.multiple_of(start, S5_GROUP), S5_ROW), :]
        m = jnp.where(col_t == t, win, m)
    m_ref[...] = m
    a16r, a16i = apow(float(S5_CHUNK))
    a16r_ref[...] = a16r
    a16i_ref[...] = a16i


def _s5_ops_call(lam_re, lam_im, log_step, bt_re, bt_im, c_re, c_im):
    g, p, cg = S5_GROUPS, S5_STATE, S5_GROUP
    ls = jnp.broadcast_to(log_step[:, :, None, None], (2, g, 1, p))
    lam_spec = pl.BlockSpec((None, None, 1, p), lambda d, gi: (d, gi, 0, 0))
    par_spec = pl.BlockSpec((None, cg, p), lambda d, gi: (gi, 0, 0))
    out = lambda *shape: pl.BlockSpec((None, None) + shape, lambda d, gi: (d, gi, 0, 0))
    return pl.pallas_call(
        _s5_ops_kernel,
        grid=(2, g),
        in_specs=[lam_spec, lam_spec, lam_spec, par_spec, par_spec, par_spec, par_spec],
        out_specs=[out(S5_ROW, S5_ROW), out(S5_ROW, p), out(S5_ROW, p), out(S5_ROW, p), out(S5_ROW, p),
                   out(1, p), out(1, p)],
        out_shape=[jax.ShapeDtypeStruct((2, g, S5_ROW, S5_ROW), F32)]
        + [jax.ShapeDtypeStruct((2, g, S5_ROW, p), F32)] * 4
        + [jax.ShapeDtypeStruct((2, g, 1, p), F32)] * 2,
        scratch_shapes=[pltpu.VMEM((3 * S5_ROW, S5_ROW), F32)],
        compiler_params=_params("parallel", "parallel"),
        name="s5_ops",
    )(lam_re.reshape(2, g, 1, p), lam_im.reshape(2, g, 1, p), ls, bt_re, bt_im, c_re, c_im)


def _regroup_kernel(to_groups, rows, *refs):
    units = 128 // S5_GROUP
    cols_per_t = S5_WIDTH // 128
    cols_per_g = S5_ROW // 128
    tok_refs = refs[:cols_per_t] if to_groups else refs[1:]
    row_ref = refs[cols_per_t] if to_groups else refs[0]
    unit = lax.broadcasted_iota(jnp.int32, (rows, 128), 1) // S5_GROUP

    def token_slab(t, col):
        return tok_refs[col].at[pl.ds(t, rows, stride=S5_CHUNK), :]

    def gather(pieces, src_unit):
        acc = None
        for p in range(units):
            shift = ((p - src_unit[p]) * S5_GROUP) % 128
            piece = pltpu.roll(pieces[p], shift, axis=1) if shift else pieces[p]
            acc = piece if acc is None else jnp.where(unit == p, piece, acc)
        return acc

    if to_groups:
        for gh in range(cols_per_t):
            for t_hi in range(cols_per_g):
                slabs = [token_slab(units * t_hi + p, gh)[...] for p in range(units)]
                for q in range(units):
                    j = (units * gh + q) * cols_per_g + t_hi
                    row_ref[:, j * 128:(j + 1) * 128] = gather(slabs, [q] * units).astype(row_ref.dtype)
    else:
        for j in range(S5_CHUNK * cols_per_t):
            t, gh = j // cols_per_t, j % cols_per_t
            cols = [(units * gh + p) * cols_per_g + t // units for p in range(units)]
            pieces = [row_ref[:, col * 128:(col + 1) * 128] for col in cols]
            token_slab(t, gh)[...] = gather(pieces, [t % units] * units)


def _regroup_call(x, to_groups, dtype):
    b = x.shape[0]
    nc = x.shape[1] // S5_CHUNK if to_groups else x.shape[1]
    rows = min(64, nc)
    ncol = S5_WIDTH // 128
    row_spec = pl.BlockSpec((None, rows, S5_CHUNK * S5_WIDTH), lambda b, i: (b, i, 0))
    row_shape = jax.ShapeDtypeStruct((b, nc, S5_CHUNK * S5_WIDTH), dtype)
    if to_groups:
        in_specs = [pl.BlockSpec((None, rows * S5_CHUNK, 128), functools.partial(lambda b, i, col: (b, i, col), col=col))
                    for col in range(ncol)]
        args, out_specs, out_shape = [x] * ncol, row_spec, row_shape
    else:
        in_specs, args = [row_spec], [x]
        out_specs = [pl.BlockSpec((None, rows * S5_CHUNK, 128), lambda b, i: (b, i, 0))] * ncol
        out_shape = [jax.ShapeDtypeStruct((b, nc * S5_CHUNK, 128), dtype)] * ncol
    return pl.pallas_call(
        functools.partial(_regroup_kernel, to_groups, rows),
        grid=(b, nc // rows),
        in_specs=in_specs,
        out_specs=out_specs,
        out_shape=out_shape,
        compiler_params=_params("parallel", "parallel"),
        name="s5_to_groups" if to_groups else "s5_to_tokens",
    )(*args)


def _s5_in_kernel(u_ref, wr_ref, wi_ref, sr_ref, si_ref):
    dot = functools.partial(jnp.dot, preferred_element_type=F32)
    u0, u1 = u_ref[:, :S5_ROW], u_ref[:, S5_ROW:]
    sr_ref[...] = dot(u0, wr_ref[0]) + dot(u1, wr_ref[1])
    si_ref[...] = dot(u0, wi_ref[0]) + dot(u1, wi_ref[1])


def _s5_in_call(ug, win_r, win_i):
    b, nc, _ = ug.shape
    g = S5_GROUPS
    w_spec = pl.BlockSpec((None, 2, S5_ROW, 128), lambda d, b, gp: (d, gp, 0, 0))
    o_spec = pl.BlockSpec((None, None, nc, 128), lambda d, b, gp: (d, b, 0, gp))
    return pl.pallas_call(
        _s5_in_kernel,
        grid=(2, b, g // 2),
        in_specs=[pl.BlockSpec((None, nc, 2 * S5_ROW), lambda d, b, gp: (b, 0, gp)), w_spec, w_spec],
        out_specs=[o_spec, o_spec],
        out_shape=[jax.ShapeDtypeStruct((2, b, nc, S5_LANES), F32)] * 2,
        compiler_params=_params("parallel", "parallel", "parallel"),
        name="s5_in",
    )(ug, win_r, win_i)


def _s5_scan_kernel(nc, sr_ref, si_ref, ar_ref, ai_ref, h0r_ref, h0i_ref, hr_ref, hi_ref, fr_ref, fi_ref):
    coef = [(ar_ref[d], ai_ref[d]) for d in range(2)]

    def body(i, carry):
        out = []
        for d in range(2):
            hr, hi = carry[d]
            ar, ai = coef[d]
            n = i if d == 0 else nc - 1 - i
            hr_ref[d, pl.ds(n, 1), :] = hr
            hi_ref[d, pl.ds(n, 1), :] = hi
            sr = sr_ref[d, pl.ds(n, 1), :]
            si = si_ref[d, pl.ds(n, 1), :]
            out.append((ar * hr - ai * hi + sr, ar * hi + ai * hr + si))
        return tuple(out)

    fin = lax.fori_loop(0, nc, body, tuple((h0r_ref[d], h0i_ref[d]) for d in range(2)), unroll=4)
    for d in range(2):
        fr_ref[d] = fin[d][0]
        fi_ref[d] = fin[d][1]


def _s5_scan_call(s_re, s_im, a16_re, a16_im, h0_re, h0_im):
    _, b, nc, _ = s_re.shape
    tl = 512
    big = pl.BlockSpec((2, None, nc, tl), lambda b, j: (0, b, 0, j))
    a_spec = pl.BlockSpec((2, 1, tl), lambda b, j: (0, 0, j))
    st_spec = pl.BlockSpec((2, None, 1, tl), lambda b, j: (0, b, 0, j))
    return pl.pallas_call(
        functools.partial(_s5_scan_kernel, nc),
        grid=(b, S5_LANES // tl),
        in_specs=[big, big, a_spec, a_spec, st_spec, st_spec],
        out_specs=[big, big, st_spec, st_spec],
        out_shape=[jax.ShapeDtypeStruct(s_re.shape, F32)] * 2
        + [jax.ShapeDtypeStruct((2, b, 1, S5_LANES), F32)] * 2,
        compiler_params=_params("parallel", "parallel"),
        name="s5_scan",
    )(s_re, s_im, a16_re, a16_im, h0_re, h0_im)


def _s5_out_kernel(u_ref, m_ref, hr_ref, hi_ref, wr_ref, wi_ref, y_ref):
    u = u_ref[...]
    acc = None
    for d in range(2):
        y = (jnp.dot(u, m_ref[d], preferred_element_type=F32)
             + _mm(hr_ref[d], wr_ref[d]) + _mm(hi_ref[d], wi_ref[d]))
        acc = y if acc is None else acc + y
    y_ref[...] = acc


def _s5_out_call(ug, m, h_re, h_im, wout_r, wout_i):
    b, nc, _ = ug.shape
    g = S5_GROUPS
    h_spec = pl.BlockSpec((2, None, nc, 128), lambda b, gi: (0, b, 0, gi // 2))
    w_spec = pl.BlockSpec((2, None, 128, S5_ROW), lambda b, gi: (0, gi, 0, 0))
    u_spec = pl.BlockSpec((None, nc, S5_ROW), lambda b, gi: (b, 0, gi))
    return pl.pallas_call(
        _s5_out_kernel,
        grid=(b, g),
        in_specs=[u_spec,
                  pl.BlockSpec((2, None, S5_ROW, S5_ROW), lambda b, gi: (0, gi, 0, 0)),
                  h_spec, h_spec, w_spec, w_spec],
        out_specs=u_spec,
        out_shape=jax.ShapeDtypeStruct(ug.shape, F32),
        compiler_params=_params("parallel", "parallel"),
        name="s5_out",
    )(ug, m, h_re, h_im, wout_r, wout_i)


def _s5_prepare_ops(lam_re, lam_im, log_step, b_re, b_im, c_re, c_im):
    m, win_r, win_i, wout_r, wout_i, a16_r, a16_i = _s5_ops_call(
        lam_re, lam_im, log_step, jnp.swapaxes(b_re, 1, 2), jnp.swapaxes(b_im, 1, 2), c_re, c_im)
    p = S5_STATE
    odd = (jnp.arange(S5_GROUPS) % 2 == 1)[None, :, None, None]

    def pad_in(w):
        z = jnp.zeros_like(w)
        return jnp.where(odd, jnp.concatenate([z, w], -1), jnp.concatenate([w, z], -1)).astype(BF16)

    def pad_out(w):
        wt = jnp.swapaxes(w, 2, 3)
        z = jnp.zeros_like(wt)
        return jnp.where(odd, jnp.concatenate([z, wt], 2), jnp.concatenate([wt, z], 2)).astype(BF16)

    return dict(m=m.astype(BF16), win_r=pad_in(win_r), win_i=pad_in(win_i),
                wout_r=pad_out(wout_r), wout_i=pad_out(wout_i),
                a16_r=a16_r.reshape(2, 1, S5_LANES), a16_i=a16_i.reshape(2, 1, S5_LANES))


def _s5_mix(u, ops, h0_re, h0_im):
    b, seq, _ = u.shape
    nc = seq // S5_CHUNK
    ug = _regroup_call(u, True, BF16)
    s_re, s_im = _s5_in_call(ug, ops["win_r"], ops["win_i"])
    h_re, h_im, f_re, f_im = _s5_scan_call(s_re, s_im, ops["a16_r"], ops["a16_i"], h0_re, h0_im)
    y = _s5_out_call(ug, ops["m"], h_re, h_im, ops["wout_r"], ops["wout_i"])
    y = _regroup_call(y, False, F32)
    return y, f_re, f_im


def _conv_kernel(rows, width, n_hor, has_ver, nt, *refs):
    if has_ver:
        (xh_ref, vp_ref, vc_ref, vn_ref, wh_ref, wv_ref, b_ref, g_ref, be_ref,
         o_ref, hs_scr, vs_scr) = refs
    else:
        xh_ref, wh_ref, b_ref, g_ref, be_ref, o_ref, hs_scr = refs
    i = pl.program_id(1)
    tokens = rows * width
    slot = width + 2 * CONV_PAD
    half = CONV_K // 2

    hs_scr[...] = jnp.zeros_like(hs_scr)
    for r in range(rows):
        hs_scr[r, pl.ds(CONV_PAD, width), :] = xh_ref[pl.ds(r * width, width), :]
    wh = wh_ref[...]
    acc_h = jnp.zeros((tokens, n_hor), F32)
    for tap in range(CONV_K):
        win = hs_scr[:, pl.ds(CONV_PAD - half + tap, width), :].reshape(tokens, n_hor)
        acc_h = acc_h + win * wh[tap:tap + 1, :]

    if has_ver:
        vs_scr[pl.ds(0, tokens), :] = jnp.where(i == 0, 0.0, vp_ref[...])
        vs_scr[pl.ds(tokens, tokens), :] = vc_ref[...]
        vs_scr[pl.ds(2 * tokens, tokens), :] = jnp.where(i == nt - 1, 0.0, vn_ref[...])
        wv = wv_ref[...]
        acc_v = jnp.zeros((tokens, CONV_WIDTH - n_hor), F32)
        for tap in range(CONV_K):
            acc_v = acc_v + vs_scr[pl.ds(tokens + (tap - half) * width, tokens), :] * wv[tap:tap + 1, :]
        x = jnp.concatenate([acc_h, acc_v], axis=1)
    else:
        x = acc_h
    x = x + b_ref[...]
    xc = x - jnp.mean(x, axis=-1, keepdims=True)
    y = xc * lax.rsqrt(jnp.mean(xc * xc, axis=-1, keepdims=True) + EPS) * g_ref[...] + be_ref[...]
    o_ref[...] = _silu(y)


def _conv_call(xc, dw_w, dw_b, ln_g, ln_b, grid_rows):
    b, seq, ch = xc.shape
    vec = pl.BlockSpec((1, ch), lambda b, i: (0, 0))
    if grid_rows > 0:
        width, rows, n_hor = GRID_W, 16, ch // 2
        tokens = rows * width
        nt = seq // tokens
        half_spec = lambda f: pl.BlockSpec((None, tokens, n_hor), f)
        in_specs = [half_spec(lambda b, i: (b, i, 0)),
                    half_spec(lambda b, i: (b, jnp.maximum(i - 1, 0), 1)),
                    half_spec(lambda b, i: (b, i, 1)),
                    half_spec(lambda b, i: (b, jnp.minimum(i + 1, nt - 1), 1)),
                    pl.BlockSpec((CONV_K, n_hor), lambda b, i: (0, 0)),
                    pl.BlockSpec((CONV_K, n_hor), lambda b, i: (0, 1)),
                    vec, vec, vec]
        args = [xc, xc, xc, xc, dw_w, dw_w, dw_b, ln_g, ln_b]
        scratch = [pltpu.VMEM((rows, width + 2 * CONV_PAD, n_hor), F32),
                   pltpu.VMEM((3 * tokens, ch - n_hor), F32)]
        has_ver = True
    else:
        width, rows, n_hor = seq, 1, ch
        tokens = seq
        nt = 1
        in_specs = [pl.BlockSpec((None, tokens, ch), lambda b, i: (b, 0, 0)),
                    pl.BlockSpec((CONV_K, ch), lambda b, i: (0, 0)),
                    vec, vec, vec]
        args = [xc, dw_w, dw_b, ln_g, ln_b]
        scratch = [pltpu.VMEM((rows, width + 2 * CONV_PAD, n_hor), F32)]
        has_ver = False
    return pl.pallas_call(
        functools.partial(_conv_kernel, rows, width, n_hor, has_ver, nt),
        grid=(b, nt),
        in_specs=in_specs,
        out_specs=pl.BlockSpec((None, tokens, ch), lambda b, i: (b, i, 0)),
        out_shape=jax.ShapeDtypeStruct((b, seq, ch), F32),
        scratch_shapes=scratch,
        compiler_params=_params("parallel", "parallel"),
        name="conformer_conv",
    )(*args)


def _lb_kernel(x_ref, o_ref):
    x = x_ref[...]
    n = x.shape[0]
    rows = [x[r:r + 1, :] for r in range(n)]
    mx = functools.reduce(jnp.maximum, rows)
    ex = [jnp.exp(r - mx) for r in rows]
    tot = functools.reduce(lambda p, q: p + q, ex)
    run = None
    for r in range(n):
        run = ex[r] / tot if run is None else run + ex[r] / tot
        o_ref[pl.ds(r, 1), :] = run


def _lb_call(logits):
    return pl.pallas_call(_lb_kernel, out_shape=jax.ShapeDtypeStruct(logits.shape, F32), name="hgrn_lb")(logits)


def _hgrn_kernel(rev, t_blk, nb, q_ref, f_ref, v_ref, lb_ref, s0_ref, o_ref, sfin_ref,
                 st_scr, kv_scr, sall_scr):
    i = pl.program_id(1)
    c = HGRN_CHUNK
    nch = t_blk // c
    hh = HGRN_HEADS

    @pl.when(i == 0)
    def _():
        st_scr[...] = s0_ref[...]

    lb = lb_ref[...]
    f = lb + (1.0 - lb) * jax.nn.sigmoid(f_ref[...])
    k = 1.0 - f
    log_f = jnp.log(f)
    row = lax.broadcasted_iota(jnp.int32, (t_blk, t_blk), 0)
    col = lax.broadcasted_iota(jnp.int32, (t_blk, t_blk), 1)
    same = _same_block(row, col, c)
    incl = same & ((row <= col) if rev else (row >= col))
    bc = _mm_exact_lhs(incl.astype(BF16), log_f)
    last = 0 if rev else c - 1
    bc3 = bc.reshape(nch, c, bc.shape[1])
    tot = jnp.broadcast_to(bc3[:, last:last + 1, :], bc3.shape).reshape(bc.shape)
    q_in = q_ref[...] * jnp.exp(bc)
    k_in = k * jnp.exp(-bc)
    k_out = k * jnp.exp(tot - bc)
    cd = jnp.exp(tot)
    v = v_ref[...]
    order = range(nch - 1, -1, -1) if rev else range(nch)
    heads = range(hh)
    ls = [slice(hd * HGRN_DIM, (hd + 1) * HGRN_DIM) for hd in heads]
    attn = [jnp.where(incl, _mm_nt(q_in[:, ls[hd]], k_in[:, ls[hd]]), 0.0) for hd in heads]
    o_intra = [_mm(attn[hd], v[:, ls[hd]]) for hd in heads]
    for n in range(nch):
        rows = slice(n * c, (n + 1) * c)
        for hd in heads:
            kv_scr[hd, n] = _mm_tn(v[rows, ls[hd]], k_out[rows, ls[hd]])
    st = [st_scr[hd] for hd in heads]
    for n in order:
        for hd in heads:
            sall_scr[hd, n] = st[hd]
            st[hd] = st[hd] * cd[n * c:n * c + 1, ls[hd]] + kv_scr[hd, n]
    for hd in heads:
        st_scr[hd] = st[hd]
    grp = 4
    blk = lax.broadcasted_iota(jnp.int32, (grp * c, HGRN_DIM), 0) // c
    for n in range(0, nch, grp):
        rows = slice(n * c, (n + grp) * c)
        for hd in heads:
            qg = q_in[rows, ls[hd]]
            q_bd = jnp.concatenate([jnp.where(blk == j, qg, 0.0) for j in range(grp)], axis=1)
            s_cat = jnp.concatenate([sall_scr[hd, n + j] for j in range(grp)], axis=1)
            o_ref[pl.ds(n * c, grp * c), pl.ds(hd * HGRN_DIM, HGRN_DIM)] = _mm_nt(q_bd, s_cat) + o_intra[hd][rows]

    @pl.when(i == nb - 1)
    def _():
        sfin_ref[...] = st_scr[...]


def _hgrn_call(q, fgate, v, lb_row, s0, rev):
    b, seq, _ = q.shape
    t_blk = min(256, seq)
    nb = seq // t_blk
    hh = HGRN_HEADS

    def blk(i):
        return (nb - 1 - i) if rev else i

    tok = pl.BlockSpec((None, t_blk, HGRN_WIDTH), lambda b, i: (b, blk(i), 0))
    state_spec = pl.BlockSpec((None, hh, HGRN_DIM, HGRN_DIM), lambda b, i: (b, 0, 0, 0))
    chunk_states = pltpu.VMEM((hh, t_blk // HGRN_CHUNK, HGRN_DIM, HGRN_DIM), F32)
    return pl.pallas_call(
        functools.partial(_hgrn_kernel, rev, t_blk, nb),
        grid=(b, nb),
        in_specs=[tok, tok, tok, pl.BlockSpec((1, HGRN_WIDTH), lambda b, i: (0, 0)), state_spec],
        out_specs=[tok, state_spec],
        out_shape=[jax.ShapeDtypeStruct((b, seq, HGRN_WIDTH), F32),
                   jax.ShapeDtypeStruct((b, hh, HGRN_DIM, HGRN_DIM), F32)],
        scratch_shapes=[pltpu.VMEM((hh, HGRN_DIM, HGRN_DIM), F32), chunk_states, chunk_states],
        compiler_params=_params("parallel", "arbitrary"),
        name="hgrn_bwd" if rev else "hgrn_fwd",
    )(q, fgate, v, lb_row, s0)


def _mixer_ab(h, hc, mod, layer, need_ctx, prm):
    ng = prm["norm_g1"]
    outs = {}
    gdn_state = [jnp.zeros((h.shape[0], GDN_HEADS, GDN_DIM, GDN_DIM), F32)] * 2
    s5_state = [jnp.zeros((2, h.shape[0], 1, S5_LANES), F32)] * 2
    for ctx, x in ((True, hc), (False, h)):
        q, k, v, z, u, gates = _inproj_ab_call(x, mod, layer, ctx, ng, prm["w_in"], prm["conv_w"])
        o_f, sf = _gdn_call(q, k, v, gates, prm["alog"], prm["dtb"], gdn_state[0], rev=False)
        o_b, sb = _gdn_call(q, k, v, gates, prm["alog"], prm["dtb"], gdn_state[1], rev=True)
        y5, f_re, f_im = _s5_mix(u, prm["s5"], s5_state[0], s5_state[1])
        gdn_state = [sf, sb]
        s5_state = [f_re, f_im]
        if ctx and not need_ctx:
            continue
        outs[ctx] = ("ab", [o_f, o_b, z, *y5, u],
                     [prm["gdn_g"], prm["s5_d"], prm["glu_w"], prm["glu_b"], prm["w_out"]])
    return outs[False], outs.get(True)


def _mixer_cd(h, hc, mod, layer, need_ctx, prm, grid_rows):
    ng = prm["norm_g1"]
    outs = {}
    state = [jnp.zeros((h.shape[0], HGRN_HEADS, HGRN_DIM, HGRN_DIM), F32)] * 2
    for ctx, x in ((True, hc), (False, h)):
        xc, q, f_f, f_b, iv, g = _inproj_call(x, mod, layer, ctx, ng, prm["w_in"], prm["pieces"], glu=True)
        o_f, sf = _hgrn_call(q, f_f, iv, prm["lb"], state[0], rev=False)
        o_b, sb = _hgrn_call(q, f_b, iv, prm["lb"], state[1], rev=True)
        state = [sf, sb]
        if ctx and not need_ctx:
            continue
        cmix = _conv_call(xc, prm["dw_w"], prm["dw_b"], prm["ln_g"], prm["ln_b"], 0 if ctx else grid_rows)
        outs[ctx] = ("cd", [cmix, o_f, o_b, g], [prm["hgrn_g"], prm["w_out"]])
    return outs[False], outs.get(True)


def kernel(x, c, ctx, c_ctx, ada_w, ada_b, norm_g, ffn_w_up, ffn_w_down, ab_w_in, ab_w_out, gdn_conv_w, gdn_a_log, gdn_dt_bias, gdn_norm_g, s5_lambda_re, s5_lambda_im, s5_log_step, s5_b_re, s5_b_im, s5_c_re, s5_c_im, s5_d, s5_glu_w, s5_glu_b, cd_w_in, cd_w_out, conv_dw_w, conv_dw_b, conv_ln_g, conv_ln_b, hgrn_lb_logits, hgrn_norm_g, final_norm_g):
    depth = ada_w.shape[0]
    batch = x.shape[0]
    grid_rows = x.shape[1] // GRID_W
    assert batch <= 2 and x.shape[1] % 1024 == 0 and ctx.shape[1] % 256 == 0

    cvec = jnp.zeros((8, D_MODEL), F32).at[:batch].set(c).at[2].set(c_ctx)
    mod = _ada_call(cvec, ada_w, ada_b)
    lb_all = _lb_call(hgrn_lb_logits)
    row = lambda v: v.reshape(1, -1)
    gw, kw = GDN_WIDTH, HGRN_WIDTH

    w_up = ffn_w_up.astype(BF16)
    w_down = ffn_w_down.astype(BF16)
    h, hc = x, ctx
    for l in range(depth):
        last = l == depth - 1
        h = _ffn_call(h, mod, l, 0, False, row(norm_g[l, 0]), w_up, w_down, (l, 0))
        hc = _ffn_call(hc, mod, l, 0, True, row(norm_g[l, 0]), w_up, w_down, (l, 0))
        if l % 2 == 0:
            e = l // 2
            w = ab_w_in[e]
            ng4 = 4 * GDN_HEADS
            w = jnp.concatenate([w[:, :4 * gw], w[:, 4 * gw + ng4:], w[:, 4 * gw:4 * gw + ng4],
                                 jnp.zeros((D_MODEL, 128 - ng4), F32)], axis=1).astype(BF16)
            pad8 = lambda v: jnp.zeros((1, 128), F32).at[0, :2 * GDN_HEADS].set(v.reshape(-1))
            prm = dict(
                norm_g1=row(norm_g[l, 1]),
                w_in=w, conv_w=gdn_conv_w[e], alog=pad8(gdn_a_log[e]), dtb=pad8(gdn_dt_bias[e]),
                gdn_g=row(gdn_norm_g[e]),
                s5=_s5_prepare_ops(s5_lambda_re[e], s5_lambda_im[e], s5_log_step[e], s5_b_re[e], s5_b_im[e],
                                   s5_c_re[e], s5_c_im[e]),
                s5_d=row(s5_d[e]), glu_w=s5_glu_w[e].astype(BF16), glu_b=row(s5_glu_b[e]),
                w_out=ab_w_out[e].astype(BF16))
            mix, mix_ctx = _mixer_ab(h, hc, mod, l, not last, prm)
        else:
            o = l // 2
            cw = CONV_WIDTH
            prm = dict(
                norm_g1=row(norm_g[l, 1]),
                w_in=cd_w_in[o].astype(BF16),
                pieces=[(k * cw, cw) for k in range(2)] + [(2 * cw + k * kw, kw) for k in range(5)],
                lb=lb_all[o:o + 1], hgrn_g=row(hgrn_norm_g[o]),
                dw_w=conv_dw_w[o], dw_b=row(conv_dw_b[o]), ln_g=row(conv_ln_g[o]), ln_b=row(conv_ln_b[o]),
                w_out=cd_w_out[o].astype(BF16))
            mix, mix_ctx = _mixer_cd(h, hc, mod, l, not last, prm, grid_rows)
        h = _ffn_call(h, mod, l, 2, False, row(norm_g[l, 2]), w_up, w_down, (l, 1),
                      final_g=row(final_norm_g) if last else None, mixer=mix)
        if not last:
            hc = _ffn_call(hc, mod, l, 2, True, row(norm_g[l, 2]), w_up, w_down, (l, 1), mixer=mix_ctx)
    return h
```

```python
import functools
import math

import jax
import jax.numpy as jnp
from jax import lax
from jax.experimental import pallas as pl
from jax.experimental.pallas import tpu as pltpu

F32 = jnp.float32
BF16 = jnp.bfloat16
HIGHEST = lax.Precision.HIGHEST
EPS = 1e-6

D_MODEL = 1024
GRID_W = 64
FFN_DIM = 2816

GDN_HEADS = 4
GDN_DIM = 128
GDN_WIDTH = GDN_HEADS * GDN_DIM
GDN_CONV = 5
GDN_CHUNK = 256

S5_WIDTH = 512
S5_GROUP = 16
S5_GROUPS = 32
S5_STATE = 64
S5_CHUNK = 16
S5_ROW = S5_CHUNK * S5_GROUP
S5_LANES = S5_GROUPS * S5_STATE

CONV_WIDTH = 512
CONV_K = 31
CONV_PAD = 16

HGRN_HEADS = 4
HGRN_DIM = 128
HGRN_WIDTH = HGRN_HEADS * HGRN_DIM
HGRN_CHUNK = 16

V7X_VMEM_LIMIT = 48 * 1024 * 1024

TOKEN_TILE = 512
FFN_SLICE = 256
GDN_BLOCK = 512
HGRN_BLOCK = 256
CONV_ROWS = 16
S5_REGROUP_ROWS = 64
S5_SCAN_LANES = 512


def _params(*sem):
    return pltpu.CompilerParams(dimension_semantics=sem, vmem_limit_bytes=V7X_VMEM_LIMIT)


def _silu(x):
    return x * jax.nn.sigmoid(x)


def _mm(a, b):
    return jnp.dot(a.astype(BF16), b.astype(BF16), preferred_element_type=F32)


def _mm_nt(a, b):
    return lax.dot_general(a.astype(BF16), b.astype(BF16), (((1,), (1,)), ((), ())),
                           preferred_element_type=F32)


def _mm_tn(a, b):
    return lax.dot_general(a.astype(BF16), b.astype(BF16), (((0,), (0,)), ((), ())),
                           preferred_element_type=F32)


def _split3(x):
    hi = x.astype(BF16)
    r1 = x - hi.astype(F32)
    mid = r1.astype(BF16)
    lo = (r1 - mid.astype(F32)).astype(BF16)
    return hi, mid, lo


def _mm_exact_lhs(a_bf16, x):
    return jnp.dot(jnp.concatenate([a_bf16] * 3, axis=1), jnp.concatenate(_split3(x), axis=0),
                   preferred_element_type=F32)


def _mm_exact_rhs(x, b_bf16):
    return jnp.dot(jnp.concatenate(_split3(x), axis=1), jnp.concatenate([b_bf16] * 3, axis=0),
                   preferred_element_type=F32)


def _mm_x3(a, b):
    ah = a.astype(BF16)
    al = (a - ah.astype(F32)).astype(BF16)
    bh = b.astype(BF16)
    bl = (b - bh.astype(F32)).astype(BF16)
    return jnp.dot(jnp.concatenate([ah, ah, al], axis=1), jnp.concatenate([bh, bl, bh], axis=0),
                   preferred_element_type=F32)


def _same_block(i, j, size):
    return (i ^ j) < size


def _ada_norm(x, g, scale, shift):
    y = x * lax.rsqrt(jnp.mean(x * x, axis=-1, keepdims=True) + EPS) * g
    return y * (1.0 + scale) + shift


def _ada_kernel(c_ref, w_ref, b_ref, o_ref):
    o_ref[...] = _mm(_silu(c_ref[...]), w_ref[...]) + b_ref[...]


def _ada_call(cvec, ada_w, ada_b):
    depth = ada_w.shape[0]
    ncol = ada_w.shape[2] // D_MODEL
    out = pl.pallas_call(
        _ada_kernel,
        grid=(depth, ncol),
        in_specs=[pl.BlockSpec((8, D_MODEL), lambda l, j: (0, 0)),
                  pl.BlockSpec((None, D_MODEL, D_MODEL), lambda l, j: (l, 0, j)),
                  pl.BlockSpec((None, 1, D_MODEL), lambda l, j: (l, 0, j))],
        out_specs=pl.BlockSpec((None, None, 8, D_MODEL), lambda l, j: (l, j, 0, 0)),
        out_shape=jax.ShapeDtypeStruct((depth, ncol, 8, D_MODEL), F32),
        compiler_params=_params("arbitrary", "arbitrary"),
        name="ada_mod",
    )(cvec, ada_w, ada_b.reshape(depth, 1, -1))
    return out.reshape(depth * ncol * 8, 1, D_MODEL)


def _mod_spec(layer, sub, kind, ctx):
    base = (layer * 9 + sub * 3 + kind) * 8
    if ctx:
        return pl.BlockSpec((None, 1, D_MODEL), lambda b, *_: (base + 2, 0, 0))
    return pl.BlockSpec((None, 1, D_MODEL), lambda b, *_: (base + b, 0, 0))


def _head_norm(o, gate, g):
    outs = []
    for hd in range(o.shape[1] // 128):
        x = o[:, hd * 128:(hd + 1) * 128]
        x = x * lax.rsqrt(jnp.mean(x * x, axis=-1, keepdims=True) + EPS) * g
        outs.append(x * _silu(gate[:, hd * 128:(hd + 1) * 128]))
    return jnp.concatenate(outs, axis=1)


def _mix_ab(of_ref, ob_ref, z_ref, y5a_ref, y5b_ref, y5c_ref, y5d_ref, u_ref, ng_ref, dsk_ref, glw_ref, glb_ref,
            wout_ref):
    a = _head_norm(of_ref[...] + ob_ref[...], z_ref[...], ng_ref[...])
    y5 = jnp.concatenate([y5a_ref[...], y5b_ref[...], y5c_ref[...], y5d_ref[...]], axis=1)
    y = y5 + dsk_ref[...] * u_ref[...]
    y = 0.5 * y * (1.0 + jnp.tanh(math.sqrt(2.0 / math.pi) * (y + 0.044715 * (y * y * y))))
    bmix = y * jax.nn.sigmoid(_mm(y, glw_ref[...]) + glb_ref[...])
    return _mm(jnp.concatenate([a, bmix], axis=1), wout_ref[...])


def _mix_cd(c_ref, of_ref, ob_ref, g_ref, ng_ref, wout_ref):
    dmix = _head_norm(of_ref[...] + ob_ref[...], g_ref[...], ng_ref[...])
    return _mm(jnp.concatenate([c_ref[...], dmix], axis=1), wout_ref[...])


_MIXERS = {"ab": (_mix_ab, 13), "cd": (_mix_cd, 6)}


def _ffn_kernel(tf, final, mixer, h_ref, *refs):
    x = h_ref[...]
    if mixer is not None:
        mix_fn, n_mix = _MIXERS[mixer]
        x = x + refs[0][...] * mix_fn(*refs[1:1 + n_mix])
        refs = refs[1 + n_mix:]
    sh_ref, sc_ref, gt_ref, g_ref, wup_ref, wd_ref = refs[:6]
    if final:
        fg_ref, o_ref = refs[6:]
    else:
        (o_ref,) = refs[6:]
    xn = _ada_norm(x, g_ref[...], sc_ref[...], sh_ref[...]).astype(BF16)
    acc = None
    for j in range(FFN_DIM // tf):
        gate = jnp.dot(xn, wup_ref[:, j * tf:(j + 1) * tf], preferred_element_type=F32)
        up = jnp.dot(xn, wup_ref[:, FFN_DIM + j * tf:FFN_DIM + (j + 1) * tf], preferred_element_type=F32)
        act = (_silu(gate) * up).astype(BF16)
        part = jnp.dot(act, wd_ref[j * tf:(j + 1) * tf, :], preferred_element_type=F32)
        acc = part if acc is None else acc + part
    y = x + 0.5 * gt_ref[...] * acc
    if final:
        y = y * lax.rsqrt(jnp.mean(y * y, axis=-1, keepdims=True) + EPS) * fg_ref[...]
    o_ref[...] = y


def _ffn_call(h, mod, layer, sub, ctx, norm_g_row, w_up, w_down, which, final_g=None, mixer=None):
    b, seq, _ = h.shape
    tm = min(TOKEN_TILE, seq)
    tf = FFN_SLICE
    final = final_g is not None
    resident = dict(pipeline_mode=pl.Buffered(1))
    in_specs = [pl.BlockSpec((None, tm, D_MODEL), lambda b, i: (b, i, 0))]
    args = [h]
    if mixer is not None:
        kind, toks, consts = mixer
        in_specs.append(_mod_spec(layer, 1, 2, ctx))
        in_specs += [pl.BlockSpec((None, tm, t.shape[2]), lambda b, i: (b, i, 0)) for t in toks]
        in_specs += [pl.BlockSpec(cst.shape, lambda b, i: (0, 0), **resident) for cst in consts]
        args += [mod, *toks, *consts]
        assert 1 + len(toks) + len(consts) == 1 + _MIXERS[kind][1]
    in_specs += [
        _mod_spec(layer, sub, 0, ctx), _mod_spec(layer, sub, 1, ctx), _mod_spec(layer, sub, 2, ctx),
        pl.BlockSpec((1, D_MODEL), lambda b, i: (0, 0)),
        pl.BlockSpec((None, None, D_MODEL, 2 * FFN_DIM), lambda b, i: which + (0, 0), **resident),
        pl.BlockSpec((None, None, FFN_DIM, D_MODEL), lambda b, i: which + (0, 0), **resident),
    ]
    args += [mod, mod, mod, norm_g_row, w_up, w_down]
    if final:
        in_specs.append(pl.BlockSpec((1, D_MODEL), lambda b, i: (0, 0)))
        args.append(final_g)
    return pl.pallas_call(
        functools.partial(_ffn_kernel, tf, final, None if mixer is None else mixer[0]),
        grid=(b, seq // tm),
        in_specs=in_specs,
        out_specs=pl.BlockSpec((None, tm, D_MODEL), lambda b, i: (b, i, 0)),
        out_shape=jax.ShapeDtypeStruct(h.shape, F32),
        compiler_params=_params("parallel", "parallel"),
        name="ffn",
    )(*args)


def _inproj_kernel(nw, glu, h_ref, sh_ref, sc_ref, g_ref, *refs):
    w_refs, o_refs = refs[:nw], refs[nw:]
    xn = _ada_norm(h_ref[...], g_ref[...], sc_ref[...], sh_ref[...]).astype(BF16)
    outs = [jnp.dot(xn, w[...], preferred_element_type=F32) for w in w_refs]
    if glu:
        outs = [outs[0] * jax.nn.sigmoid(outs[1])] + outs[2:]
    for o_ref, val in zip(o_refs, outs):
        o_ref[...] = val


def _inproj_call(h, mod, layer, ctx, norm_g_row, w, pieces, glu):
    b, seq, _ = h.shape
    tm = min(TOKEN_TILE, seq)
    widths = [wd for _, wd in pieces]
    out_widths = widths[1:] if glu else widths
    in_specs = [pl.BlockSpec((None, tm, D_MODEL), lambda b, i: (b, i, 0)),
                _mod_spec(layer, 1, 0, ctx), _mod_spec(layer, 1, 1, ctx),
                pl.BlockSpec((1, D_MODEL), lambda b, i: (0, 0))]
    for off, wd in pieces:
        assert off % wd == 0
        in_specs.append(pl.BlockSpec((D_MODEL, wd), functools.partial(lambda b, i, blk: (0, blk), blk=off // wd),
                                     pipeline_mode=pl.Buffered(1)))
    return pl.pallas_call(
        functools.partial(_inproj_kernel, len(pieces), glu),
        grid=(b, seq // tm),
        in_specs=in_specs,
        out_specs=[pl.BlockSpec((None, tm, wd), lambda b, i: (b, i, 0)) for wd in out_widths],
        out_shape=[jax.ShapeDtypeStruct((b, seq, wd), F32) for wd in out_widths],
        compiler_params=_params("parallel", "parallel"),
        name="inproj",
    )(h, mod, mod, norm_g_row, *([w] * len(pieces)))


def _inproj_ab_kernel(tm, nt, h_ref, hp_ref, hn_ref, sh_ref, sc_ref, g_ref, wqkv_ref, wz_ref, wu_ref, wg_ref, cw_ref,
                      q_ref, k_ref, v_ref, z_ref, u_ref, gates_ref, pad_scr):
    i = pl.program_id(1)
    norm = lambda x: _ada_norm(x, g_ref[...], sc_ref[...], sh_ref[...]).astype(BF16)
    dot = functools.partial(jnp.dot, preferred_element_type=F32)
    xn = norm(h_ref[...])
    xh = norm(jnp.concatenate([hp_ref[...], hn_ref[...]], axis=0))
    w = cw_ref[...]
    gw = GDN_WIDTH

    def project(part):
        cols = pl.ds(part * gw, gw)
        wp = wqkv_ref[:, part * gw:(part + 1) * gw]
        halo = dot(xh, wp)
        pad_scr[pl.ds(0, 8), cols] = jnp.where(i == 0, 0.0, halo[:8])
        pad_scr[pl.ds(8, tm), cols] = dot(xn, wp)
        pad_scr[pl.ds(8 + tm, 8), cols] = jnp.where(i == nt - 1, 0.0, halo[8:])

    def conv(part):
        acc = None
        for tap in range(GDN_CONV):
            term = (pad_scr[pl.ds(8 - GDN_CONV // 2 + tap, tm), pl.ds(part * gw, gw)]
                    * w[tap:tap + 1, part * gw:(part + 1) * gw])
            acc = term if acc is None else acc + term
        return _silu(acc)

    def l2n_heads(x, scale):
        outs = []
        for hd in range(GDN_HEADS):
            xs = x[:, hd * GDN_DIM:(hd + 1) * GDN_DIM]
            outs.append(xs * (lax.rsqrt(jnp.sum(xs * xs, axis=-1, keepdims=True) + EPS) * scale))
        return jnp.concatenate(outs, axis=1)

    project(0)
    project(1)
    q_ref[...] = l2n_heads(conv(0), GDN_DIM ** -0.5)
    project(2)
    k_ref[...] = l2n_heads(conv(1), 1.0)
    z_ref[...] = dot(xn, wz_ref[...])
    v_ref[...] = conv(2)
    u_ref[...] = dot(xn, wu_ref[...])
    gates_ref[...] = dot(xn, wg_ref[...])


def _inproj_ab_call(h, mod, layer, ctx, norm_g_row, w, conv_w):
    b, seq, _ = h.shape
    tm = min(TOKEN_TILE, seq)
    nt = seq // tm
    r8 = tm // 8
    gw = GDN_WIDTH
    resident = dict(pipeline_mode=pl.Buffered(1))
    wspec = lambda wd, blk: pl.BlockSpec((D_MODEL, wd), lambda b, i: (0, blk), **resident)
    tok = lambda wd: pl.BlockSpec((None, tm, wd), lambda b, i: (b, i, 0))
    in_specs = [tok(D_MODEL),
                pl.BlockSpec((None, 8, D_MODEL), lambda b, i: (b, jnp.maximum(i * r8 - 1, 0), 0)),
                pl.BlockSpec((None, 8, D_MODEL), lambda b, i: (b, jnp.minimum((i + 1) * r8, seq // 8 - 1), 0)),
                _mod_spec(layer, 1, 0, ctx), _mod_spec(layer, 1, 1, ctx),
                pl.BlockSpec((1, D_MODEL), lambda b, i: (0, 0)),
                wspec(3 * gw, 0), wspec(gw, 3), wspec(S5_WIDTH, 4), wspec(128, (4 * gw + S5_WIDTH) // 128),
                pl.BlockSpec((GDN_CONV, 3 * gw), lambda b, i: (0, 0))]
    widths = [gw, gw, gw, gw, S5_WIDTH, 128]
    return pl.pallas_call(
        functools.partial(_inproj_ab_kernel, tm, nt),
        grid=(b, nt),
        in_specs=in_specs,
        out_specs=[tok(wd) for wd in widths],
        out_shape=[jax.ShapeDtypeStruct((b, seq, wd), F32) for wd in widths],
        scratch_shapes=[pltpu.VMEM((tm + 16, 3 * gw), F32)],
        compiler_params=_params("parallel", "parallel"),
        name="inproj_ab",
    )(h, h, h, mod, mod, norm_g_row, w, w, w, w, conv_w)


def _gdn_kernel(rev, t_blk, nb, q_scr, k_scr, v_scr, gates_ref, alog_ref, dtb_ref, s0_ref,
                o_ref, sfin_ref,
                s_scr, sel_scr, ut_scr, w_scr, qd_scr, kd_scr, qk_scr, cd_scr, gt_scr):
    i = pl.program_id(1)
    c = GDN_CHUNK
    nch = t_blk // c
    hh = GDN_HEADS

    @pl.when(i == 0)
    def _():
        s_scr[...] = s0_ref[...]

    gates = gates_ref[...]
    log_a =-jnp.exp(alog_ref[...]) * jax.nn.softplus(gates + dtb_ref[...])
    row = lax.broadcasted_iota(jnp.int32, (c, c), 0)
    col = lax.broadcasted_iota(jnp.int32, (c, c), 1)
    incl = (row <= col) if rev else (row >= col)
    strict = (row < col) if rev else (row > col)
    last = 0 if rev else c - 1
    g_all = jnp.concatenate([_mm_exact_lhs(incl.astype(BF16), log_a[m * c:(m + 1) * c, :]) for m in range(nch)],
                            axis=0)
    for m in range(nch):
        gt_scr[m] = g_all[m * c:(m + 1) * c, :].T
    beta_all = jax.nn.sigmoid(gates)
    d_off = hh if rev else 0
    for hd in range(hh):
        sel_scr[:, pl.ds(hd * 256, 128)] = jnp.broadcast_to(g_all[:, d_off + hd:d_off + hd + 1], (t_blk, 128))
        sel_scr[:, pl.ds(hd * 256 + 128, 128)] = jnp.broadcast_to(
            beta_all[:, 2 * hh + d_off + hd:2 * hh + d_off + hd + 1], (t_blk, 128))

    diag16 = _same_block(row, col, 16)
    levels = []
    d = 16
    while d < c:
        levels.append(_same_block(row, col, 2 * d) & jnp.logical_not(_same_block(row, col, d)))
        d *= 2

    def lanes_c(x):
        return x[:, :c] if c <= 128 else jnp.concatenate([x] * (c // 128), axis=1)

    heads = range(hh)

    def inverse_minus_identity(a_mats):
        idx = range(len(a_mats))
        p = [jnp.where(diag16, -a, 0.0) for a in a_mats]
        n = list(p)
        p = [_mm(x, x) for x in p]
        for it in range(3):
            if it < 2:
                both = [_mm(jnp.concatenate([n[i], p[i]], axis=0), p[i]) for i in idx]
                n = [n[i] + p[i] + both[i][:c] for i in idx]
                p = [both[i][c:] for i in idx]
            else:
                n = [n[i] + p[i] + _mm(n[i], p[i]) for i in idx]
        for mask in levels:
            off = [jnp.where(mask, a, 0.0) for a in a_mats]
            x = [off[i] + _mm(n[i], off[i]) for i in idx]
            n = [n[i] - (x[i] + _mm(x[i], n[i])) for i in idx]
        return n

    def prep(m, carry):
        rows = pl.ds(pl.multiple_of(m * c, c), c)
        lanes = [pl.ds(hd * GDN_DIM, GDN_DIM) for hd in heads]
        q = [q_scr[rows, lanes[hd]] for hd in heads]
        k = [k_scr[rows, lanes[hd]] for hd in heads]
        g = [sel_scr[rows, pl.ds(hd * 256, 128)] for hd in heads]
        beta = [sel_scr[rows, pl.ds(hd * 256 + 128, 128)] for hd in heads]
        decay = []
        for hd in heads:
            diff = lanes_c(g[hd]) - gt_scr[m, pl.ds(d_off + hd, 1), :]
            decay.append(jnp.where(incl, jnp.exp(jnp.where(incl, diff, 0.0)), 0.0))
        kq = [_mm_nt(jnp.concatenate([q[hd], k[hd]], axis=0), k[hd]) for hd in heads]
        a_mats = [jnp.where(strict, kq[hd][c:] * decay[hd], 0.0) * lanes_c(beta[hd]) for hd in heads]
        n_mats = inverse_minus_identity(a_mats)
        for hd in heads:
            eg = jnp.exp(g[hd])
            rhs = jnp.concatenate([v_scr[rows, lanes[hd]] * beta[hd], k[hd] * (beta[hd] * eg)], axis=1)
            sol = rhs + _mm_x3(n_mats[hd], rhs)
            g_last = g[hd][last:last + 1, :]
            ut_scr[rows, lanes[hd]] = sol[:, :GDN_DIM]
            w_scr[rows, lanes[hd]] = sol[:, GDN_DIM:]
            qd_scr[rows, lanes[hd]] = q[hd] * eg
            kd_scr[rows, lanes[hd]] = k[hd] * jnp.exp(g_last - g[hd])
            qk_scr[rows, pl.ds(hd * c, c)] = jnp.where(incl, kq[hd][:c] * decay[hd], 0.0)
            cd_scr[m, hd] = jnp.broadcast_to(jnp.exp(g_last), (8, GDN_DIM))
        return carry

    lax.fori_loop(0, nch, prep, 0, unroll=True)


    def step(ci, carry):
        cidx = (nch - 1 - ci) if rev else ci
        rows = pl.ds(pl.multiple_of(cidx * c, c), c)
        lanes = [pl.ds(hd * GDN_DIM, GDN_DIM) for hd in heads]
        s = [s_scr[hd] for hd in heads]
        ws = [_mm(jnp.concatenate([w_scr[rows, lanes[hd]], qd_scr[rows, lanes[hd]]], axis=0), s[hd]) for hd in heads]
        u = [ut_scr[rows, lanes[hd]] - ws[hd][:c] for hd in heads]
        intra = [_mm(qk_scr[rows, pl.ds(hd * c, c)], u[hd]) for hd in heads]
        outer = [_mm_tn(kd_scr[rows, lanes[hd]], u[hd]) for hd in heads]
        for hd in heads:
            o_ref[rows, lanes[hd]] = ws[hd][c:] + intra[hd]
            s_scr[hd] = s[hd] * cd_scr[cidx, hd][0:1, :] + outer[hd]
        return carry

    lax.fori_loop(0, nch, step, 0)

    @pl.when(i == nb - 1)
    def _():
        sfin_ref[...] = s_scr[...]


def _gdn_call(q, k, v, gates, alog_row, dtb_row, s0, rev):
    b, seq, _ = q.shape
    t_blk = min(GDN_BLOCK, seq)
    nb = seq // t_blk
    hh = GDN_HEADS
    width = GDN_WIDTH

    def blk(i):
        return (nb - 1 - i) if rev else i

    tile = pl.BlockSpec((None, t_blk, width), lambda b, i: (b, blk(i), 0))
    state_spec = pl.BlockSpec((None, hh, GDN_DIM, GDN_DIM), lambda b, i: (b, 0, 0, 0))
    in_specs = [tile, tile, tile,
                pl.BlockSpec((None, t_blk, 128), lambda b, i: (b, blk(i), 0)),
                pl.BlockSpec((1, 128), lambda b, i: (0, 0)),
                pl.BlockSpec((1, 128), lambda b, i: (0, 0)),
                state_spec]
    args = [q, k, v, gates, alog_row, dtb_row, s0]
    tok = lambda w: pltpu.VMEM((t_blk, w), F32)
    return pl.pallas_call(
        functools.partial(_gdn_kernel, rev, t_blk, nb),
        grid=(b, nb),
        in_specs=in_specs,
        out_specs=[tile, state_spec],
        out_shape=[jax.ShapeDtypeStruct((b, seq, width), F32),
                   jax.ShapeDtypeStruct((b, hh, GDN_DIM, GDN_DIM), F32)],
        scratch_shapes=[pltpu.VMEM((hh, GDN_DIM, GDN_DIM), F32),
                        tok(2 * width),
                        tok(width), tok(width), tok(width), tok(width), tok(hh * GDN_CHUNK),
                        pltpu.VMEM((t_blk // GDN_CHUNK, hh, 8, GDN_DIM), F32),
                        pltpu.VMEM((t_blk // GDN_CHUNK, 128, GDN_CHUNK), F32)],
        compiler_params=_params("parallel", "arbitrary"),
        name="gdn_bwd" if rev else "gdn_fwd",
    )(*args)


def _s5_ops_kernel(lr_ref, li_ref, ls_ref, btr_ref, bti_ref, cr_ref, ci_ref,
                   m_ref, winr_ref, wini_ref, woutr_ref, wouti_ref, a16r_ref, a16i_ref, tab_scr):
    d = pl.program_id(0)
    lr = lr_ref[...]
    li = li_ref[...]
    dt = jnp.exp(ls_ref[...])

    def apow(kk):
        mag = jnp.exp(lr * dt * kk)
        ang = li * dt * kk
        return mag * jnp.cos(ang), mag * jnp.sin(ang)

    def cmul(xr, xi, yr, yi):
        return xr * yr - xi * yi, xr * yi + xi * yr

    ar, ai = apow(1.0)
    den = lr * lr + li * li
    nr, ni = ar - 1.0, ai
    zr = (nr * lr + ni * li) / den
    zi = (ni * lr - nr * li) / den
    bbr, bbi = cmul(zr, zi, btr_ref[...], bti_ref[...])
    cr, ci = cr_ref[...], ci_ref[...]

    t16 = lax.broadcasted_iota(jnp.int32, (S5_CHUNK, 1), 0)
    tv16 = jnp.where(d == 0, t16, S5_CHUNK - 1 - t16).astype(F32)
    tile = lambda x: jnp.concatenate([x] * S5_CHUNK, axis=0)
    spread = (lax.broadcasted_iota(jnp.int32, (S5_ROW, S5_CHUNK), 0) // S5_GROUP
              == lax.broadcasted_iota(jnp.int32, (S5_ROW, S5_CHUNK), 1)).astype(BF16)
    rep = lambda x: _mm_exact_lhs(spread, x)
    pr, pi = apow(S5_CHUNK - 1.0 - tv16)
    xr, xi = cmul(tile(bbr), tile(bbi), rep(pr), rep(pi))
    winr_ref[...] = xr
    wini_ref[...] = xi
    pr, pi = apow(tv16 + 1.0)
    yr, yi = cmul(tile(cr), tile(ci), rep(pr), rep(pi))
    woutr_ref[...] = yr
    wouti_ref[...] = -yi

    nt = lambda p, q: lax.dot_general(p, q, (((1,), (1,)), ((), ())), precision=HIGHEST,
                                      preferred_element_type=F32)
    table = nt(xr, cr) - nt(xi, ci)
    expand = (lax.broadcasted_iota(jnp.int32, (S5_GROUP, S5_ROW), 0)
              == lax.broadcasted_iota(jnp.int32, (S5_GROUP, S5_ROW), 1) % S5_GROUP).astype(BF16)
    tab_scr[...] = jnp.zeros_like(tab_scr)
    tab_scr[pl.ds(S5_ROW, S5_ROW), :] = _mm_exact_rhs(table, expand)
    col_t = lax.broadcasted_iota(jnp.int32, (S5_ROW, S5_ROW), 1) // S5_GROUP
    m = jnp.zeros((S5_ROW, S5_ROW), F32)
    for t in range(S5_CHUNK):
        start = S5_ROW + jnp.where(d == 0, S5_CHUNK - 1 - t, -t) * S5_GROUP
        win = tab_scr[pl.ds(pl.multiple_of(start, S5_GROUP), S5_ROW), :]
        m = jnp.where(col_t == t, win, m)
    m_ref[...] = m
    a16r, a16i = apow(float(S5_CHUNK))
    a16r_ref[...] = a16r
    a16i_ref[...] = a16i


def _s5_ops_call(lam_re, lam_im, log_step, bt_re, bt_im, c_re, c_im):
    g, p, cg = S5_GROUPS, S5_STATE, S5_GROUP
    ls = jnp.broadcast_to(log_step[:, :, None, None], (2, g, 1, p))
    lam_spec = pl.BlockSpec((None, None, 1, p), lambda d, gi: (d, gi, 0, 0))
    par_spec = pl.BlockSpec((None, cg, p), lambda d, gi: (gi, 0, 0))
    out = lambda *shape: pl.BlockSpec((None, None) + shape, lambda d, gi: (d, gi, 0, 0))
    return pl.pallas_call(
        _s5_ops_kernel,
        grid=(2, g),
        in_specs=[lam_spec, lam_spec, lam_spec, par_spec, par_spec, par_spec, par_spec],
        out_specs=[out(S5_ROW, S5_ROW), out(S5_ROW, p), out(S5_ROW, p), out(S5_ROW, p), out(S5_ROW, p),
                   out(1, p), out(1, p)],
        out_shape=[jax.ShapeDtypeStruct((2, g, S5_ROW, S5_ROW), F32)]
        + [jax.ShapeDtypeStruct((2, g, S5_ROW, p), F32)] * 4
        + [jax.ShapeDtypeStruct((2, g, 1, p), F32)] * 2,
        scratch_shapes=[pltpu.VMEM((3 * S5_ROW, S5_ROW), F32)],
        compiler_params=_params("parallel", "parallel"),
        name="s5_ops",
    )(lam_re.reshape(2, g, 1, p), lam_im.reshape(2, g, 1, p), ls, bt_re, bt_im, c_re, c_im)


def _regroup_kernel(to_groups, rows, *refs):
    units = 128 // S5_GROUP
    cols_per_t = S5_WIDTH // 128
    cols_per_g = S5_ROW // 128
    tok_refs = refs[:cols_per_t] if to_groups else refs[1:]
    row_ref = refs[cols_per_t] if to_groups else refs[0]
    unit = lax.broadcasted_iota(jnp.int32, (rows, 128), 1) // S5_GROUP

    def token_slab(t, col):
        return tok_refs[col].at[pl.ds(t, rows, stride=S5_CHUNK), :]

    def gather(pieces, src_unit):
        acc = None
        for p in range(units):
            shift = ((p - src_unit[p]) * S5_GROUP) % 128
            piece = pltpu.roll(pieces[p], shift, axis=1) if shift else pieces[p]
            acc = piece if acc is None else jnp.where(unit == p, piece, acc)
        return acc

    if to_groups:
        for gh in range(cols_per_t):
            for t_hi in range(cols_per_g):
                slabs = [token_slab(units * t_hi + p, gh)[...] for p in range(units)]
                for q in range(units):
                    j = (units * gh + q) * cols_per_g + t_hi
                    row_ref[:, j * 128:(j + 1) * 128] = gather(slabs, [q] * units).astype(row_ref.dtype)
    else:
        for j in range(S5_CHUNK * cols_per_t):
            t, gh = j // cols_per_t, j % cols_per_t
            cols = [(units * gh + p) * cols_per_g + t // units for p in range(units)]
            pieces = [row_ref[:, col * 128:(col + 1) * 128] for col in cols]
            token_slab(t, gh)[...] = gather(pieces, [t % units] * units)


def _regroup_call(x, to_groups, dtype):
    b = x.shape[0]
    nc = x.shape[1] // S5_CHUNK if to_groups else x.shape[1]
    rows = min(S5_REGROUP_ROWS, nc)
    ncol = S5_WIDTH // 128
    row_spec = pl.BlockSpec((None, rows, S5_CHUNK * S5_WIDTH), lambda b, i: (b, i, 0))
    row_shape = jax.ShapeDtypeStruct((b, nc, S5_CHUNK * S5_WIDTH), dtype)
    if to_groups:
        in_specs = [pl.BlockSpec((None, rows * S5_CHUNK, 128), functools.partial(lambda b, i, col: (b, i, col), col=col))
                    for col in range(ncol)]
        args, out_specs, out_shape = [x] * ncol, row_spec, row_shape
    else:
        in_specs, args = [row_spec], [x]
        out_specs = [pl.BlockSpec((None, rows * S5_CHUNK, 128), lambda b, i: (b, i, 0))] * ncol
        out_shape = [jax.ShapeDtypeStruct((b, nc * S5_CHUNK, 128), dtype)] * ncol
    return pl.pallas_call(
        functools.partial(_regroup_kernel, to_groups, rows),
        grid=(b, nc // rows),
        in_specs=in_specs,
        out_specs=out_specs,
        out_shape=out_shape,
        compiler_params=_params("parallel", "parallel"),
        name="s5_to_groups" if to_groups else "s5_to_tokens",
    )(*args)


def _s5_in_kernel(u_ref, wr_ref, wi_ref, sr_ref, si_ref):
    dot = functools.partial(jnp.dot, preferred_element_type=F32)
    u0, u1 = u_ref[:, :S5_ROW], u_ref[:, S5_ROW:]
    sr_ref[...] = dot(u0, wr_ref[0]) + dot(u1, wr_ref[1])
    si_ref[...] = dot(u0, wi_ref[0]) + dot(u1, wi_ref[1])


def _s5_in_call(ug, win_r, win_i):
    b, nc, _ = ug.shape
    g = S5_GROUPS
    w_spec = pl.BlockSpec((None, 2, S5_ROW, 128), lambda d, b, gp: (d, gp, 0, 0))
    o_spec = pl.BlockSpec((None, None, nc, 128), lambda d, b, gp: (d, b, 0, gp))
    return pl.pallas_call(
        _s5_in_kernel,
        grid=(2, b, g // 2),
        in_specs=[pl.BlockSpec((None, nc, 2 * S5_ROW), lambda d, b, gp: (b, 0, gp)), w_spec, w_spec],
        out_specs=[o_spec, o_spec],
        out_shape=[jax.ShapeDtypeStruct((2, b, nc, S5_LANES), F32)] * 2,
        compiler_params=_params("parallel", "parallel", "parallel"),
        name="s5_in",
    )(ug, win_r, win_i)


def _s5_scan_kernel(nc, sr_ref, si_ref, ar_ref, ai_ref, h0r_ref, h0i_ref, hr_ref, hi_ref, fr_ref, fi_ref):
    coef = [(ar_ref[d], ai_ref[d]) for d in range(2)]

    def body(i, carry):
        out = []
        for d in range(2):
            hr, hi = carry[d]
            ar, ai = coef[d]
            n = i if d == 0 else nc - 1 - i
            hr_ref[d, pl.ds(n, 1), :] = hr
            hi_ref[d, pl.ds(n, 1), :] = hi
            sr = sr_ref[d, pl.ds(n, 1), :]
            si = si_ref[d, pl.ds(n, 1), :]
            out.append((ar * hr - ai * hi + sr, ar * hi + ai * hr + si))
        return tuple(out)

    fin = lax.fori_loop(0, nc, body, tuple((h0r_ref[d], h0i_ref[d]) for d in range(2)), unroll=4)
    for d in range(2):
        fr_ref[d] = fin[d][0]
        fi_ref[d] = fin[d][1]


def _s5_scan_call(s_re, s_im, a16_re, a16_im, h0_re, h0_im):
    _, b, nc, _ = s_re.shape
    tl = S5_SCAN_LANES
    big = pl.BlockSpec((2, None, nc, tl), lambda b, j: (0, b, 0, j))
    a_spec = pl.BlockSpec((2, 1, tl), lambda b, j: (0, 0, j))
    st_spec = pl.BlockSpec((2, None, 1, tl), lambda b, j: (0, b, 0, j))
    return pl.pallas_call(
        functools.partial(_s5_scan_kernel, nc),
        grid=(b, S5_LANES // tl),
        in_specs=[big, big, a_spec, a_spec, st_spec, st_spec],
        out_specs=[big, big, st_spec, st_spec],
        out_shape=[jax.ShapeDtypeStruct(s_re.shape, F32)] * 2
        + [jax.ShapeDtypeStruct((2, b, 1, S5_LANES), F32)] * 2,
        compiler_params=_params("parallel", "parallel"),
        name="s5_scan",
    )(s_re, s_im, a16_re, a16_im, h0_re, h0_im)


def _s5_out_kernel(u_ref, m_ref, hr_ref, hi_ref, wr_ref, wi_ref, y_ref):
    u = u_ref[...]
    acc = None
    for d in range(2):
        y = (jnp.dot(u, m_ref[d], preferred_element_type=F32)
             + _mm(hr_ref[d], wr_ref[d]) + _mm(hi_ref[d], wi_ref[d]))
        acc = y if acc is None else acc + y
    y_ref[...] = acc


def _s5_out_call(ug, m, h_re, h_im, wout_r, wout_i):
    b, nc, _ = ug.shape
    g = S5_GROUPS
    h_spec = pl.BlockSpec((2, None, nc, 128), lambda b, gi: (0, b, 0, gi // 2))
    w_spec = pl.BlockSpec((2, None, 128, S5_ROW), lambda b, gi: (0, gi, 0, 0))
    u_spec = pl.BlockSpec((None, nc, S5_ROW), lambda b, gi: (b, 0, gi))
    return pl.pallas_call(
        _s5_out_kernel,
        grid=(b, g),
        in_specs=[u_spec,
                  pl.BlockSpec((2, None, S5_ROW, S5_ROW), lambda b, gi: (0, gi, 0, 0)),
                  h_spec, h_spec, w_spec, w_spec],
        out_specs=u_spec,
        out_shape=jax.ShapeDtypeStruct(ug.shape, F32),
        compiler_params=_params("parallel", "parallel"),
        name="s5_out",
    )(ug, m, h_re, h_im, wout_r, wout_i)


def _s5_prepare_ops(lam_re, lam_im, log_step, b_re, b_im, c_re, c_im):
    m, win_r, win_i, wout_r, wout_i, a16_r, a16_i = _s5_ops_call(
        lam_re, lam_im, log_step, jnp.swapaxes(b_re, 1, 2), jnp.swapaxes(b_im, 1, 2), c_re, c_im)
    odd = (jnp.arange(S5_GROUPS) % 2 == 1)[None, :, None, None]

    def pad_in(w):
        z = jnp.zeros_like(w)
        return jnp.where(odd, jnp.concatenate([z, w], -1), jnp.concatenate([w, z], -1)).astype(BF16)

    def pad_out(w):
        wt = jnp.swapaxes(w, 2, 3)
        z = jnp.zeros_like(wt)
        return jnp.where(odd, jnp.concatenate([z, wt], 2), jnp.concatenate([wt, z], 2)).astype(BF16)

    return dict(m=m.astype(BF16), win_r=pad_in(win_r), win_i=pad_in(win_i),
                wout_r=pad_out(wout_r), wout_i=pad_out(wout_i),
                a16_r=a16_r.reshape(2, 1, S5_LANES), a16_i=a16_i.reshape(2, 1, S5_LANES))


def _s5_mix(u, ops, h0_re, h0_im):
    ug = _regroup_call(u, True, BF16)
    s_re, s_im = _s5_in_call(ug, ops["win_r"], ops["win_i"])
    h_re, h_im, f_re, f_im = _s5_scan_call(s_re, s_im, ops["a16_r"], ops["a16_i"], h0_re, h0_im)
    y = _s5_out_call(ug, ops["m"], h_re, h_im, ops["wout_r"], ops["wout_i"])
    y = _regroup_call(y, False, F32)
    return y, f_re, f_im


def _conv_kernel(rows, width, n_hor, has_ver, nt, *refs):
    if has_ver:
        (xh_ref, vp_ref, vc_ref, vn_ref, wh_ref, wv_ref, b_ref, g_ref, be_ref,
         o_ref, hs_scr, vs_scr) = refs
    else:
        xh_ref, wh_ref, b_ref, g_ref, be_ref, o_ref, hs_scr = refs
    i = pl.program_id(1)
    tokens = rows * width
    half = CONV_K // 2

    hs_scr[...] = jnp.zeros_like(hs_scr)
    for r in range(rows):
        hs_scr[r, pl.ds(CONV_PAD, width), :] = xh_ref[pl.ds(r * width, width), :]
    wh = wh_ref[...]
    acc_h = jnp.zeros((tokens, n_hor), F32)
    for tap in range(CONV_K):
        win = hs_scr[:, pl.ds(CONV_PAD - half + tap, width), :].reshape(tokens, n_hor)
        acc_h = acc_h + win * wh[tap:tap + 1, :]

    if has_ver:
        vs_scr[pl.ds(0, tokens), :] = jnp.where(i == 0, 0.0, vp_ref[...])
        vs_scr[pl.ds(tokens, tokens), :] = vc_ref[...]
        vs_scr[pl.ds(2 * tokens, tokens), :] = jnp.where(i == nt - 1, 0.0, vn_ref[...])
        wv = wv_ref[...]
        acc_v = jnp.zeros((tokens, CONV_WIDTH - n_hor), F32)
        for tap in range(CONV_K):
            acc_v = acc_v + vs_scr[pl.ds(tokens + (tap - half) * width, tokens), :] * wv[tap:tap + 1, :]
        x = jnp.concatenate([acc_h, acc_v], axis=1)
    else:
        x = acc_h
    x = x + b_ref[...]
    xc = x - jnp.mean(x, axis=-1, keepdims=True)
    y = xc * lax.rsqrt(jnp.mean(xc * xc, axis=-1, keepdims=True) + EPS) * g_ref[...] + be_ref[...]
    o_ref[...] = _silu(y)


def _conv_call(xc, dw_w, dw_b, ln_g, ln_b, grid_rows):
    b, seq, ch = xc.shape
    vec = pl.BlockSpec((1, ch), lambda b, i: (0, 0))
    if grid_rows > 0:
        width, rows, n_hor = GRID_W, CONV_ROWS, ch // 2
        tokens = rows * width
        nt = seq // tokens
        half_spec = lambda f: pl.BlockSpec((None, tokens, n_hor), f)
        in_specs = [half_spec(lambda b, i: (b, i, 0)),
                    half_spec(lambda b, i: (b, jnp.maximum(i - 1, 0), 1)),
                    half_spec(lambda b, i: (b, i, 1)),
                    half_spec(lambda b, i: (b, jnp.minimum(i + 1, nt - 1), 1)),
                    pl.BlockSpec((CONV_K, n_hor), lambda b, i: (0, 0)),
                    pl.BlockSpec((CONV_K, n_hor), lambda b, i: (0, 1)),
                    vec, vec, vec]
        args = [xc, xc, xc, xc, dw_w, dw_w, dw_b, ln_g, ln_b]
        scratch = [pltpu.VMEM((rows, width + 2 * CONV_PAD, n_hor), F32),
                   pltpu.VMEM((3 * tokens, ch - n_hor), F32)]
        has_ver = True
    else:
        width, rows, n_hor = seq, 1, ch
        tokens = seq
        nt = 1
        in_specs = [pl.BlockSpec((None, tokens, ch), lambda b, i: (b, 0, 0)),
                    pl.BlockSpec((CONV_K, ch), lambda b, i: (0, 0)),
                    vec, vec, vec]
        args = [xc, dw_w, dw_b, ln_g, ln_b]
        scratch = [pltpu.VMEM((rows, width + 2 * CONV_PAD, n_hor), F32)]
        has_ver = False
    return pl.pallas_call(
        functools.partial(_conv_kernel, rows, width, n_hor, has_ver, nt),
        grid=(b, nt),
        in_specs=in_specs,
        out_specs=pl.BlockSpec((None, tokens, ch), lambda b, i: (b, i, 0)),
        out_shape=jax.ShapeDtypeStruct((b, seq, ch), F32),
        scratch_shapes=scratch,
        compiler_params=_params("parallel", "parallel"),
        name="conformer_conv",
    )(*args)


def _lb_kernel(x_ref, o_ref):
    x = x_ref[...]
    n = x.shape[0]
    rows = [x[r:r + 1, :] for r in range(n)]
    mx = functools.reduce(jnp.maximum, rows)
    ex = [jnp.exp(r - mx) for r in rows]
    tot = functools.reduce(lambda p, q: p + q, ex)
    run = None
    for r in range(n):
        run = ex[r] / tot if run is None else run + ex[r] / tot
        o_ref[pl.ds(r, 1), :] = run


def _lb_call(logits):
    return pl.pallas_call(_lb_kernel, out_shape=jax.ShapeDtypeStruct(logits.shape, F32), name="hgrn_lb")(logits)


def _hgrn_kernel(rev, t_blk, nb, q_ref, f_ref, v_ref, lb_ref, s0_ref, o_ref, sfin_ref,
                 st_scr, kv_scr, sall_scr):
    i = pl.program_id(1)
    c = HGRN_CHUNK
    nch = t_blk // c
    hh = HGRN_HEADS

    @pl.when(i == 0)
    def _():
        st_scr[...] = s0_ref[...]

    lb = lb_ref[...]
    f = lb + (1.0 - lb) * jax.nn.sigmoid(f_ref[...])
    k = 1.0 - f
    log_f = jnp.log(f)
    row = lax.broadcasted_iota(jnp.int32, (t_blk, t_blk), 0)
    col = lax.broadcasted_iota(jnp.int32, (t_blk, t_blk), 1)
    same = _same_block(row, col, c)
    incl = same & ((row <= col) if rev else (row >= col))
    bc = _mm_exact_lhs(incl.astype(BF16), log_f)
    last = 0 if rev else c - 1
    bc3 = bc.reshape(nch, c, bc.shape[1])
    tot = jnp.broadcast_to(bc3[:, last:last + 1, :], bc3.shape).reshape(bc.shape)
    q_in = q_ref[...] * jnp.exp(bc)
    k_in = k * jnp.exp(-bc)
    k_out = k * jnp.exp(tot - bc)
    cd = jnp.exp(tot)
    v = v_ref[...]
    order = range(nch - 1, -1, -1) if rev else range(nch)
    heads = range(hh)
    ls = [slice(hd * HGRN_DIM, (hd + 1) * HGRN_DIM) for hd in heads]
    attn = [jnp.where(incl, _mm_nt(q_in[:, ls[hd]], k_in[:, ls[hd]]), 0.0) for hd in heads]
    o_intra = [_mm(attn[hd], v[:, ls[hd]]) for hd in heads]
    for n in range(nch):
        rows = slice(n * c, (n + 1) * c)
        for hd in heads:
            kv_scr[hd, n] = _mm_tn(v[rows, ls[hd]], k_out[rows, ls[hd]])
    st = [st_scr[hd] for hd in heads]
    for n in order:
        for hd in heads:
            sall_scr[hd, n] = st[hd]
            st[hd] = st[hd] * cd[n * c:n * c + 1, ls[hd]] + kv_scr[hd, n]
    for hd in heads:
        st_scr[hd] = st[hd]
    grp = 4
    blk = lax.broadcasted_iota(jnp.int32, (grp * c, HGRN_DIM), 0) // c
    for n in range(0, nch, grp):
        rows = slice(n * c, (n + grp) * c)
        for hd in heads:
            qg = q_in[rows, ls[hd]]
            q_bd = jnp.concatenate([jnp.where(blk == j, qg, 0.0) for j in range(grp)], axis=1)
            s_cat = jnp.concatenate([sall_scr[hd, n + j] for j in range(grp)], axis=1)
            o_ref[pl.ds(n * c, grp * c), pl.ds(hd * HGRN_DIM, HGRN_DIM)] = _mm_nt(q_bd, s_cat) + o_intra[hd][rows]

    @pl.when(i == nb - 1)
    def _():
        sfin_ref[...] = st_scr[...]


def _hgrn_call(q, fgate, v, lb_row, s0, rev):
    b, seq, _ = q.shape
    t_blk = min(HGRN_BLOCK, seq)
    nb = seq // t_blk
    hh = HGRN_HEADS

    def blk(i):
        return (nb - 1 - i) if rev else i

    tok = pl.BlockSpec((None, t_blk, HGRN_WIDTH), lambda b, i: (b, blk(i), 0))
    state_spec = pl.BlockSpec((None, hh, HGRN_DIM, HGRN_DIM), lambda b, i: (b, 0, 0, 0))
    chunk_states = pltpu.VMEM((hh, t_blk // HGRN_CHUNK, HGRN_DIM, HGRN_DIM), F32)
    return pl.pallas_call(
        functools.partial(_hgrn_kernel, rev, t_blk, nb),
        grid=(b, nb),
        in_specs=[tok, tok, tok, pl.BlockSpec((1, HGRN_WIDTH), lambda b, i: (0, 0)), state_spec],
        out_specs=[tok, state_spec],
        out_shape=[jax.ShapeDtypeStruct((b, seq, HGRN_WIDTH), F32),
                   jax.ShapeDtypeStruct((b, hh, HGRN_DIM, HGRN_DIM), F32)],
        scratch_shapes=[pltpu.VMEM((hh, HGRN_DIM, HGRN_DIM), F32), chunk_states, chunk_states],
        compiler_params=_params("parallel", "arbitrary"),
        name="hgrn_bwd" if rev else "hgrn_fwd",
    )(q, fgate, v, lb_row, s0)


def _mixer_ab(h, hc, mod, layer, need_ctx, prm):
    ng = prm["norm_g1"]
    outs = {}
    gdn_state = [jnp.zeros((h.shape[0], GDN_HEADS, GDN_DIM, GDN_DIM), F32)] * 2
    s5_state = [jnp.zeros((2, h.shape[0], 1, S5_LANES), F32)] * 2
    for ctx, x in ((True, hc), (False, h)):
        q, k, v, z, u, gates = _inproj_ab_call(x, mod, layer, ctx, ng, prm["w_in"], prm["conv_w"])
        o_f, sf = _gdn_call(q, k, v, gates, prm["alog"], prm["dtb"], gdn_state[0], rev=False)
        o_b, sb = _gdn_call(q, k, v, gates, prm["alog"], prm["dtb"], gdn_state[1], rev=True)
        y5, f_re, f_im = _s5_mix(u, prm["s5"], s5_state[0], s5_state[1])
        gdn_state = [sf, sb]
        s5_state = [f_re, f_im]
        if ctx and not need_ctx:
            continue
        outs[ctx] = ("ab", [o_f, o_b, z, *y5, u],
                     [prm["gdn_g"], prm["s5_d"], prm["glu_w"], prm["glu_b"], prm["w_out"]])
    return outs[False], outs.get(True)


def _mixer_cd(h, hc, mod, layer, need_ctx, prm, grid_rows):
    ng = prm["norm_g1"]
    outs = {}
    state = [jnp.zeros((h.shape[0], HGRN_HEADS, HGRN_DIM, HGRN_DIM), F32)] * 2
    for ctx, x in ((True, hc), (False, h)):
        xc, q, f_f, f_b, iv, g = _inproj_call(x, mod, layer, ctx, ng, prm["w_in"], prm["pieces"], glu=True)
        o_f, sf = _hgrn_call(q, f_f, iv, prm["lb"], state[0], rev=False)
        o_b, sb = _hgrn_call(q, f_b, iv, prm["lb"], state[1], rev=True)
        state = [sf, sb]
        if ctx and not need_ctx:
            continue
        cmix = _conv_call(xc, prm["dw_w"], prm["dw_b"], prm["ln_g"], prm["ln_b"], 0 if ctx else grid_rows)
        outs[ctx] = ("cd", [cmix, o_f, o_b, g], [prm["hgrn_g"], prm["w_out"]])
    return outs[False], outs.get(True)


def kernel(x, c, ctx, c_ctx, ada_w, ada_b, norm_g, ffn_w_up, ffn_w_down, ab_w_in, ab_w_out, gdn_conv_w, gdn_a_log, gdn_dt_bias, gdn_norm_g, s5_lambda_re, s5_lambda_im, s5_log_step, s5_b_re, s5_b_im, s5_c_re, s5_c_im, s5_d, s5_glu_w, s5_glu_b, cd_w_in, cd_w_out, conv_dw_w, conv_dw_b, conv_ln_g, conv_ln_b, hgrn_lb_logits, hgrn_norm_g, final_norm_g):
    depth = ada_w.shape[0]
    batch = x.shape[0]
    grid_rows = x.shape[1] // GRID_W
    assert batch <= 2 and x.shape[1] % 1024 == 0 and ctx.shape[1] % 256 == 0

    cvec = jnp.zeros((8, D_MODEL), F32).at[:batch].set(c).at[2].set(c_ctx)
    mod = _ada_call(cvec, ada_w, ada_b)
    lb_all = _lb_call(hgrn_lb_logits)
    row = lambda v: v.reshape(1, -1)
    gw, kw = GDN_WIDTH, HGRN_WIDTH

    w_up = ffn_w_up.astype(BF16)
    w_down = ffn_w_down.astype(BF16)
    h, hc = x, ctx
    for l in range(depth):
        last = l == depth - 1
        h = _ffn_call(h, mod, l, 0, False, row(norm_g[l, 0]), w_up, w_down, (l, 0))
        hc = _ffn_call(hc, mod, l, 0, True, row(norm_g[l, 0]), w_up, w_down, (l, 0))
        if l % 2 == 0:
            e = l // 2
            w = ab_w_in[e]
            ng4 = 4 * GDN_HEADS
            w = jnp.concatenate([w[:, :4 * gw], w[:, 4 * gw + ng4:], w[:, 4 * gw:4 * gw + ng4],
                                 jnp.zeros((D_MODEL, 128 - ng4), F32)], axis=1).astype(BF16)
            pad8 = lambda v: jnp.zeros((1, 128), F32).at[0, :2 * GDN_HEADS].set(v.reshape(-1))
            prm = dict(
                norm_g1=row(norm_g[l, 1]),
                w_in=w, conv_w=gdn_conv_w[e], alog=pad8(gdn_a_log[e]), dtb=pad8(gdn_dt_bias[e]),
                gdn_g=row(gdn_norm_g[e]),
                s5=_s5_prepare_ops(s5_lambda_re[e], s5_lambda_im[e], s5_log_step[e], s5_b_re[e], s5_b_im[e],
                                   s5_c_re[e], s5_c_im[e]),
                s5_d=row(s5_d[e]), glu_w=s5_glu_w[e].astype(BF16), glu_b=row(s5_glu_b[e]),
                w_out=ab_w_out[e].astype(BF16))
            mix, mix_ctx = _mixer_ab(h, hc, mod, l, not last, prm)
        else:
            o = l // 2
            cw = CONV_WIDTH
            prm = dict(
                norm_g1=row(norm_g[l, 1]),
                w_in=cd_w_in[o].astype(BF16),
                pieces=[(k * cw, cw) for k in range(2)] + [(2 * cw + k * kw, kw) for k in range(5)],
                lb=lb_all[o:o + 1], hgrn_g=row(hgrn_norm_g[o]),
                dw_w=conv_dw_w[o], dw_b=row(conv_dw_b[o]), ln_g=row(conv_ln_g[o]), ln_b=row(conv_ln_b[o]),
                w_out=cd_w_out[o].astype(BF16))
            mix, mix_ctx = _mixer_cd(h, hc, mod, l, not last, prm, grid_rows)
        h = _ffn_call(h, mod, l, 2, False, row(norm_g[l, 2]), w_up, w_down, (l, 1),
                      final_g=row(final_norm_g) if last else None, mixer=mix)
        if not last:
            hc = _ffn_call(hc, mod, l, 2, True, row(norm_g[l, 2]), w_up, w_down, (l, 1), mixer=mix_ctx)
    return h
```

```python
import functools
import math

import jax
import jax.numpy as jnp
from jax import lax
from jax.experimental import pallas as pl
from jax.experimental.pallas import tpu as pltpu

F32 = jnp.float32
BF16 = jnp.bfloat16
HIGHEST = lax.Precision.HIGHEST
EPS = 1e-6

D_MODEL = 1024
GRID_W = 64
FFN_DIM = 2816

GDN_HEADS = 4
GDN_DIM = 128
GDN_WIDTH = GDN_HEADS * GDN_DIM
GDN_CONV = 5
GDN_CHUNK = 256

S5_WIDTH = 512
S5_GROUP = 16
S5_GROUPS = 32
S5_STATE = 64
S5_CHUNK = 16
S5_ROW = S5_CHUNK * S5_GROUP
S5_LANES = S5_GROUPS * S5_STATE

CONV_WIDTH = 512
CONV_K = 31
CONV_PAD = 16

HGRN_HEADS = 4
HGRN_DIM = 128
HGRN_WIDTH = HGRN_HEADS * HGRN_DIM
HGRN_CHUNK = 16

V7X_VMEM_LIMIT = 48 * 1024 * 1024

TOKEN_TILE = 512
FFN_SLICE = 256
GDN_BLOCK = 512
HGRN_BLOCK = 256
CONV_ROWS = 16
S5_REGROUP_ROWS = 64
S5_SCAN_LANES = 512


def _params(*sem):
    return pltpu.CompilerParams(dimension_semantics=sem, vmem_limit_bytes=V7X_VMEM_LIMIT)


def _silu(x):
    return x * jax.nn.sigmoid(x)


def _mm(a, b):
    return jnp.dot(a.astype(BF16), b.astype(BF16), preferred_element_type=F32)


def _mm_nt(a, b):
    return lax.dot_general(a.astype(BF16), b.astype(BF16), (((1,), (1,)), ((), ())),
                           preferred_element_type=F32)


def _mm_tn(a, b):
    return lax.dot_general(a.astype(BF16), b.astype(BF16), (((0,), (0,)), ((), ())),
                           preferred_element_type=F32)


def _split3(x):
    hi = x.astype(BF16)
    r1 = x - hi.astype(F32)
    mid = r1.astype(BF16)
    lo = (r1 - mid.astype(F32)).astype(BF16)
    return hi, mid, lo


def _mm_exact_lhs(a_bf16, x):
    return jnp.dot(jnp.concatenate([a_bf16] * 3, axis=1), jnp.concatenate(_split3(x), axis=0),
                   preferred_element_type=F32)


def _mm_exact_rhs(x, b_bf16):
    return jnp.dot(jnp.concatenate(_split3(x), axis=1), jnp.concatenate([b_bf16] * 3, axis=0),
                   preferred_element_type=F32)


def _mm_x3(a, b):
    ah = a.astype(BF16)
    al = (a - ah.astype(F32)).astype(BF16)
    bh = b.astype(BF16)
    bl = (b - bh.astype(F32)).astype(BF16)
    return jnp.dot(jnp.concatenate([ah, ah, al], axis=1), jnp.concatenate([bh, bl, bh], axis=0),
                   preferred_element_type=F32)


def _same_block(i, j, size):
    return (i ^ j) < size


def _ada_norm(x, g, scale, shift):
    y = x * lax.rsqrt(jnp.mean(x * x, axis=-1, keepdims=True) + EPS) * g
    return y * (1.0 + scale) + shift


def _ada_kernel(c_ref, w_ref, b_ref, o_ref):
    o_ref[...] = _mm(_silu(c_ref[...]), w_ref[...]) + b_ref[...]


def _ada_call(cvec, ada_w, ada_b):
    depth = ada_w.shape[0]
    ncol = ada_w.shape[2] // D_MODEL
    out = pl.pallas_call(
        _ada_kernel,
        grid=(depth, ncol),
        in_specs=[pl.BlockSpec((8, D_MODEL), lambda l, j: (0, 0)),
                  pl.BlockSpec((None, D_MODEL, D_MODEL), lambda l, j: (l, 0, j)),
                  pl.BlockSpec((None, 1, D_MODEL), lambda l, j: (l, 0, j))],
        out_specs=pl.BlockSpec((None, None, 8, D_MODEL), lambda l, j: (l, j, 0, 0)),
        out_shape=jax.ShapeDtypeStruct((depth, ncol, 8, D_MODEL), F32),
        compiler_params=_params("arbitrary", "arbitrary"),
        name="ada_mod",
    )(cvec, ada_w, ada_b.reshape(depth, 1, -1))
    return out.reshape(depth * ncol * 8, 1, D_MODEL)


def _mod_spec(layer, sub, kind, ctx):
    base = (layer * 9 + sub * 3 + kind) * 8
    if ctx:
        return pl.BlockSpec((None, 1, D_MODEL), lambda b, *_: (base + 2, 0, 0))
    return pl.BlockSpec((None, 1, D_MODEL), lambda b, *_: (base + b, 0, 0))


def _head_norm(o, gate, g):
    outs = []
    for hd in range(o.shape[1] // 128):
        x = o[:, hd * 128:(hd + 1) * 128]
        x = x * lax.rsqrt(jnp.mean(x * x, axis=-1, keepdims=True) + EPS) * g
        outs.append(x * _silu(gate[:, hd * 128:(hd + 1) * 128]))
    return jnp.concatenate(outs, axis=1)


def _mix_ab(of_ref, ob_ref, z_ref, y5a_ref, y5b_ref, y5c_ref, y5d_ref, u_ref, ng_ref, dsk_ref, glw_ref, glb_ref,
            wout_ref):
    a = _head_norm(of_ref[...] + ob_ref[...], z_ref[...], ng_ref[...])
    y5 = jnp.concatenate([y5a_ref[...], y5b_ref[...], y5c_ref[...], y5d_ref[...]], axis=1)
    y = y5 + dsk_ref[...] * u_ref[...]
    y = 0.5 * y * (1.0 + jnp.tanh(math.sqrt(2.0 / math.pi) * (y + 0.044715 * (y * y * y))))
    bmix = y * jax.nn.sigmoid(_mm(y, glw_ref[...]) + glb_ref[...])
    return _mm(jnp.concatenate([a, bmix], axis=1), wout_ref[...])


def _mix_cd(c_ref, of_ref, ob_ref, g_ref, ng_ref, wout_ref):
    dmix = _head_norm(of_ref[...] + ob_ref[...], g_ref[...], ng_ref[...])
    return _mm(jnp.concatenate([c_ref[...], dmix], axis=1), wout_ref[...])


_MIXERS = {"ab": (_mix_ab, 13), "cd": (_mix_cd, 6)}


def _ffn_kernel(tf, final, mixer, h_ref, *refs):
    x = h_ref[...]
    if mixer is not None:
        mix_fn, n_mix = _MIXERS[mixer]
        x = x + refs[0][...] * mix_fn(*refs[1:1 + n_mix])
        refs = refs[1 + n_mix:]
    sh_ref, sc_ref, gt_ref, g_ref, wup_ref, wd_ref = refs[:6]
    if final:
        fg_ref, o_ref = refs[6:]
    else:
        (o_ref,) = refs[6:]
    xn = _ada_norm(x, g_ref[...], sc_ref[...], sh_ref[...]).astype(BF16)
    acc = None
    for j in range(FFN_DIM // tf):
        gate = jnp.dot(xn, wup_ref[:, j * tf:(j + 1) * tf], preferred_element_type=F32)
        up = jnp.dot(xn, wup_ref[:, FFN_DIM + j * tf:FFN_DIM + (j + 1) * tf], preferred_element_type=F32)
        act = (_silu(gate) * up).astype(BF16)
        part = jnp.dot(act, wd_ref[j * tf:(j + 1) * tf, :], preferred_element_type=F32)
        acc = part if acc is None else acc + part
    y = x + 0.5 * gt_ref[...] * acc
    if final:
        y = y * lax.rsqrt(jnp.mean(y * y, axis=-1, keepdims=True) + EPS) * fg_ref[...]
    o_ref[...] = y


def _ffn_call(h, mod, layer, sub, ctx, norm_g_row, w_up, w_down, which, final_g=None, mixer=None):
    b, seq, _ = h.shape
    tm = min(TOKEN_TILE, seq)
    tf = FFN_SLICE
    final = final_g is not None
    resident = dict(pipeline_mode=pl.Buffered(1))
    in_specs = [pl.BlockSpec((None, tm, D_MODEL), lambda b, i: (b, i, 0))]
    args = [h]
    if mixer is not None:
        kind, toks, consts = mixer
        in_specs.append(_mod_spec(layer, 1, 2, ctx))
        in_specs += [pl.BlockSpec((None, tm, t.shape[2]), lambda b, i: (b, i, 0)) for t in toks]
        in_specs += [pl.BlockSpec(cst.shape, lambda b, i: (0, 0), **resident) for cst in consts]
        args += [mod, *toks, *consts]
        assert 1 + len(toks) + len(consts) == 1 + _MIXERS[kind][1]
    in_specs += [
        _mod_spec(layer, sub, 0, ctx), _mod_spec(layer, sub, 1, ctx), _mod_spec(layer, sub, 2, ctx),
        pl.BlockSpec((1, D_MODEL), lambda b, i: (0, 0)),
        pl.BlockSpec((None, None, D_MODEL, 2 * FFN_DIM), lambda b, i: which + (0, 0), **resident),
        pl.BlockSpec((None, None, FFN_DIM, D_MODEL), lambda b, i: which + (0, 0), **resident),
    ]
    args += [mod, mod, mod, norm_g_row, w_up, w_down]
    if final:
        in_specs.append(pl.BlockSpec((1, D_MODEL), lambda b, i: (0, 0)))
        args.append(final_g)
    return pl.pallas_call(
        functools.partial(_ffn_kernel, tf, final, None if mixer is None else mixer[0]),
        grid=(b, seq // tm),
        in_specs=in_specs,
        out_specs=pl.BlockSpec((None, tm, D_MODEL), lambda b, i: (b, i, 0)),
        out_shape=jax.ShapeDtypeStruct(h.shape, F32),
        compiler_params=_params("parallel", "parallel"),
        name="ffn",
    )(*args)


def _inproj_kernel(nw, glu, h_ref, sh_ref, sc_ref, g_ref, *refs):
    w_refs, o_refs = refs[:nw], refs[nw:]
    xn = _ada_norm(h_ref[...], g_ref[...], sc_ref[...], sh_ref[...]).astype(BF16)
    outs = [jnp.dot(xn, w[...], preferred_element_type=F32) for w in w_refs]
    if glu:
        outs = [outs[0] * jax.nn.sigmoid(outs[1])] + outs[2:]
    for o_ref, val in zip(o_refs, outs):
        o_ref[...] = val


def _inproj_call(h, mod, layer, ctx, norm_g_row, w, pieces, glu):
    b, seq, _ = h.shape
    tm = min(TOKEN_TILE, seq)
    widths = [wd for _, wd in pieces]
    out_widths = widths[1:] if glu else widths
    in_specs = [pl.BlockSpec((None, tm, D_MODEL), lambda b, i: (b, i, 0)),
                _mod_spec(layer, 1, 0, ctx), _mod_spec(layer, 1, 1, ctx),
                pl.BlockSpec((1, D_MODEL), lambda b, i: (0, 0))]
    for off, wd in pieces:
        assert off % wd == 0
        in_specs.append(pl.BlockSpec((D_MODEL, wd), functools.partial(lambda b, i, blk: (0, blk), blk=off // wd),
                                     pipeline_mode=pl.Buffered(1)))
    return pl.pallas_call(
        functools.partial(_inproj_kernel, len(pieces), glu),
        grid=(b, seq // tm),
        in_specs=in_specs,
        out_specs=[pl.BlockSpec((None, tm, wd), lambda b, i: (b, i, 0)) for wd in out_widths],
        out_shape=[jax.ShapeDtypeStruct((b, seq, wd), F32) for wd in out_widths],
        compiler_params=_params("parallel", "parallel"),
        name="inproj",
    )(h, mod, mod, norm_g_row, *([w] * len(pieces)))


def _inproj_ab_kernel(tm, nt, h_ref, hp_ref, hn_ref, sh_ref, sc_ref, g_ref, wqkv_ref, wz_ref, wu_ref, wg_ref, cw_ref,
                      q_ref, k_ref, v_ref, z_ref, u_ref, gates_ref, pad_scr):
    i = pl.program_id(1)
    norm = lambda x: _ada_norm(x, g_ref[...], sc_ref[...], sh_ref[...]).astype(BF16)
    dot = functools.partial(jnp.dot, preferred_element_type=F32)
    xn = norm(h_ref[...])
    xh = norm(jnp.concatenate([hp_ref[...], hn_ref[...]], axis=0))
    w = cw_ref[...]
    gw = GDN_WIDTH

    def project(part):
        cols = pl.ds(part * gw, gw)
        wp = wqkv_ref[:, part * gw:(part + 1) * gw]
        halo = dot(xh, wp)
        pad_scr[pl.ds(0, 8), cols] = jnp.where(i == 0, 0.0, halo[:8])
        pad_scr[pl.ds(8, tm), cols] = dot(xn, wp)
        pad_scr[pl.ds(8 + tm, 8), cols] = jnp.where(i == nt - 1, 0.0, halo[8:])

    def conv(part):
        acc = None
        for tap in range(GDN_CONV):
            term = (pad_scr[pl.ds(8 - GDN_CONV // 2 + tap, tm), pl.ds(part * gw, gw)]
                    * w[tap:tap + 1, part * gw:(part + 1) * gw])
            acc = term if acc is None else acc + term
        return _silu(acc)

    def l2n_heads(x, scale):
        outs = []
        for hd in range(GDN_HEADS):
            xs = x[:, hd * GDN_DIM:(hd + 1) * GDN_DIM]
            outs.append(xs * (lax.rsqrt(jnp.sum(xs * xs, axis=-1, keepdims=True) + EPS) * scale))
        return jnp.concatenate(outs, axis=1)

    project(0)
    project(1)
    q_ref[...] = l2n_heads(conv(0), GDN_DIM ** -0.5)
    project(2)
    k_ref[...] = l2n_heads(conv(1), 1.0)
    z_ref[...] = dot(xn, wz_ref[...])
    v_ref[...] = conv(2)
    u_ref[...] = dot(xn, wu_ref[...])
    gates_ref[...] = dot(xn, wg_ref[...])


def _inproj_ab_call(h, mod, layer, ctx, norm_g_row, w, conv_w):
    b, seq, _ = h.shape
    tm = min(TOKEN_TILE, seq)
    nt = seq // tm
    r8 = tm // 8
    gw = GDN_WIDTH
    resident = dict(pipeline_mode=pl.Buffered(1))
    wspec = lambda wd, blk: pl.BlockSpec((D_MODEL, wd), lambda b, i: (0, blk), **resident)
    tok = lambda wd: pl.BlockSpec((None, tm, wd), lambda b, i: (b, i, 0))
    in_specs = [tok(D_MODEL),
                pl.BlockSpec((None, 8, D_MODEL), lambda b, i: (b, jnp.maximum(i * r8 - 1, 0), 0)),
                pl.BlockSpec((None, 8, D_MODEL), lambda b, i: (b, jnp.minimum((i + 1) * r8, seq // 8 - 1), 0)),
                _mod_spec(layer, 1, 0, ctx), _mod_spec(layer, 1, 1, ctx),
                pl.BlockSpec((1, D_MODEL), lambda b, i: (0, 0)),
                wspec(3 * gw, 0), wspec(gw, 3), wspec(S5_WIDTH, 4), wspec(128, (4 * gw + S5_WIDTH) // 128),
                pl.BlockSpec((GDN_CONV, 3 * gw), lambda b, i: (0, 0))]
    widths = [gw, gw, gw, gw, S5_WIDTH, 128]
    return pl.pallas_call(
        functools.partial(_inproj_ab_kernel, tm, nt),
        grid=(b, nt),
        in_specs=in_specs,
        out_specs=[tok(wd) for wd in widths],
        out_shape=[jax.ShapeDtypeStruct((b, seq, wd), F32) for wd in widths],
        scratch_shapes=[pltpu.VMEM((tm + 16, 3 * gw), F32)],
        compiler_params=_params("parallel", "parallel"),
        name="inproj_ab",
    )(h, h, h, mod, mod, norm_g_row, w, w, w, w, conv_w)


def _gdn_kernel(rev, t_blk, nb, q_scr, k_scr, v_scr, gates_ref, alog_ref, dtb_ref, s0_ref,
                o_ref, sfin_ref,
                s_scr, sel_scr, ut_scr, w_scr, qd_scr, kd_scr, qk_scr, cd_scr, gt_scr):
    i = pl.program_id(1)
    c = GDN_CHUNK
    nch = t_blk // c
    hh = GDN_HEADS

    @pl.when(i == 0)
    def _():
        s_scr[...] = s0_ref[...]

    gates = gates_ref[...]
    log_a =-jnp.exp(alog_ref[...]) * jax.nn.softplus(gates + dtb_ref[...])
    row = lax.broadcasted_iota(jnp.int32, (c, c), 0)
    col = lax.broadcasted_iota(jnp.int32, (c, c), 1)
    incl = (row <= col) if rev else (row >= col)
    strict = (row < col) if rev else (row > col)
    last = 0 if rev else c - 1
    g_all = jnp.concatenate([_mm_exact_lhs(incl.astype(BF16), log_a[m * c:(m + 1) * c, :]) for m in range(nch)],
                            axis=0)
    for m in range(nch):
        gt_scr[m] = g_all[m * c:(m + 1) * c, :].T
    beta_all = jax.nn.sigmoid(gates)
    d_off = hh if rev else 0
    for hd in range(hh):
        sel_scr[:, pl.ds(hd * 256, 128)] = jnp.broadcast_to(g_all[:, d_off + hd:d_off + hd + 1], (t_blk, 128))
        sel_scr[:, pl.ds(hd * 256 + 128, 128)] = jnp.broadcast_to(
            beta_all[:, 2 * hh + d_off + hd:2 * hh + d_off + hd + 1], (t_blk, 128))

    diag16 = _same_block(row, col, 16)
    levels = []
    d = 16
    while d < c:
        levels.append(_same_block(row, col, 2 * d) & jnp.logical_not(_same_block(row, col, d)))
        d *= 2

    def lanes_c(x):
        return x[:, :c] if c <= 128 else jnp.concatenate([x] * (c // 128), axis=1)

    heads = range(hh)

    def inverse_minus_identity(a_mats):
        idx = range(len(a_mats))
        p = [jnp.where(diag16, -a, 0.0) for a in a_mats]
        n = list(p)
        p = [_mm(x, x) for x in p]
        for it in range(3):
            if it < 2:
                both = [_mm(jnp.concatenate([n[i], p[i]], axis=0), p[i]) for i in idx]
                n = [n[i] + p[i] + both[i][:c] for i in idx]
                p = [both[i][c:] for i in idx]
            else:
                n = [n[i] + p[i] + _mm(n[i], p[i]) for i in idx]
        for mask in levels:
            off = [jnp.where(mask, a, 0.0) for a in a_mats]
            x = [off[i] + _mm(n[i], off[i]) for i in idx]
            n = [n[i] - (x[i] + _mm(x[i], n[i])) for i in idx]
        return n

    def prep(m, carry):
        rows = pl.ds(pl.multiple_of(m * c, c), c)
        lanes = [pl.ds(hd * GDN_DIM, GDN_DIM) for hd in heads]
        q = [q_scr[rows, lanes[hd]] for hd in heads]
        k = [k_scr[rows, lanes[hd]] for hd in heads]
        g = [sel_scr[rows, pl.ds(hd * 256, 128)] for hd in heads]
        beta = [sel_scr[rows, pl.ds(hd * 256 + 128, 128)] for hd in heads]
        decay = []
        for hd in heads:
            diff = lanes_c(g[hd]) - gt_scr[m, pl.ds(d_off + hd, 1), :]
            decay.append(jnp.where(incl, jnp.exp(jnp.where(incl, diff, 0.0)), 0.0))
        kq = [_mm_nt(jnp.concatenate([q[hd], k[hd]], axis=0), k[hd]) for hd in heads]
        a_mats = [jnp.where(strict, kq[hd][c:] * decay[hd], 0.0) * lanes_c(beta[hd]) for hd in heads]
        n_mats = inverse_minus_identity(a_mats)
        for hd in heads:
            eg = jnp.exp(g[hd])
            rhs = jnp.concatenate([v_scr[rows, lanes[hd]] * beta[hd], k[hd] * (beta[hd] * eg)], axis=1)
            sol = rhs + _mm_x3(n_mats[hd], rhs)
            g_last = g[hd][last:last + 1, :]
            ut_scr[rows, lanes[hd]] = sol[:, :GDN_DIM]
            w_scr[rows, lanes[hd]] = sol[:, GDN_DIM:]
            qd_scr[rows, lanes[hd]] = q[hd] * eg
            kd_scr[rows, lanes[hd]] = k[hd] * jnp.exp(g_last - g[hd])
            qk_scr[rows, pl.ds(hd * c, c)] = jnp.where(incl, kq[hd][:c] * decay[hd], 0.0)
            cd_scr[m, hd] = jnp.broadcast_to(jnp.exp(g_last), (8, GDN_DIM))
        return carry

    lax.fori_loop(0, nch, prep, 0, unroll=True)


    def step(ci, carry):
        cidx = (nch - 1 - ci) if rev else ci
        rows = pl.ds(pl.multiple_of(cidx * c, c), c)
        lanes = [pl.ds(hd * GDN_DIM, GDN_DIM) for hd in heads]
        s = [s_scr[hd] for hd in heads]
        ws = [_mm(jnp.concatenate([w_scr[rows, lanes[hd]], qd_scr[rows, lanes[hd]]], axis=0), s[hd]) for hd in heads]
        u = [ut_scr[rows, lanes[hd]] - ws[hd][:c] for hd in heads]
        intra = [_mm(qk_scr[rows, pl.ds(hd * c, c)], u[hd]) for hd in heads]
        outer = [_mm_tn(kd_scr[rows, lanes[hd]], u[hd]) for hd in heads]
        for hd in heads:
            o_ref[rows, lanes[hd]] = ws[hd][c:] + intra[hd]
            s_scr[hd] = s[hd] * cd_scr[cidx, hd][0:1, :] + outer[hd]
        return carry

    lax.fori_loop(0, nch, step, 0)

    @pl.when(i == nb - 1)
    def _():
        sfin_ref[...] = s_scr[...]


def _gdn_call(q, k, v, gates, alog_row, dtb_row, s0, rev):
    b, seq, _ = q.shape
    t_blk = min(GDN_BLOCK, seq)
    nb = seq // t_blk
    hh = GDN_HEADS
    width = GDN_WIDTH

    def blk(i):
        return (nb - 1 - i) if rev else i

    tile = pl.BlockSpec((None, t_blk, width), lambda b, i: (b, blk(i), 0))
    state_spec = pl.BlockSpec((None, hh, GDN_DIM, GDN_DIM), lambda b, i: (b, 0, 0, 0))
    in_specs = [tile, tile, tile,
                pl.BlockSpec((None, t_blk, 128), lambda b, i: (b, blk(i), 0)),
                pl.BlockSpec((1, 128), lambda b, i: (0, 0)),
                pl.BlockSpec((1, 128), lambda b, i: (0, 0)),
                state_spec]
    args = [q, k, v, gates, alog_row, dtb_row, s0]
    tok = lambda w: pltpu.VMEM((t_blk, w), F32)
    return pl.pallas_call(
        functools.partial(_gdn_kernel, rev, t_blk, nb),
        grid=(b, nb),
        in_specs=in_specs,
        out_specs=[tile, state_spec],
        out_shape=[jax.ShapeDtypeStruct((b, seq, width), F32),
                   jax.ShapeDtypeStruct((b, hh, GDN_DIM, GDN_DIM), F32)],
        scratch_shapes=[pltpu.VMEM((hh, GDN_DIM, GDN_DIM), F32),
                        tok(2 * width),
                        tok(width), tok(width), tok(width), tok(width), tok(hh * GDN_CHUNK),
                        pltpu.VMEM((t_blk // GDN_CHUNK, hh, 8, GDN_DIM), F32),
                        pltpu.VMEM((t_blk // GDN_CHUNK, 128, GDN_CHUNK), F32)],
        compiler_params=_params("parallel", "arbitrary"),
        name="gdn_bwd" if rev else "gdn_fwd",
    )(*args)


def _s5_ops_kernel(lr_ref, li_ref, ls_ref, btr_ref, bti_ref, cr_ref, ci_ref,
                   m_ref, winr_ref, wini_ref, woutr_ref, wouti_ref, a16r_ref, a16i_ref, tab_scr):
    d = pl.program_id(0)
    lr = lr_ref[...]
    li = li_ref[...]
    dt = jnp.exp(ls_ref[...])

    def apow(kk):
        mag = jnp.exp(lr * dt * kk)
        ang = li * dt * kk
        return mag * jnp.cos(ang), mag * jnp.sin(ang)

    def cmul(xr, xi, yr, yi):
        return xr * yr - xi * yi, xr * yi + xi * yr

    ar, ai = apow(1.0)
    den = lr * lr + li * li
    nr, ni = ar - 1.0, ai
    zr = (nr * lr + ni * li) / den
    zi = (ni * lr - nr * li) / den
    bbr, bbi = cmul(zr, zi, btr_ref[...], bti_ref[...])
    cr, ci = cr_ref[...], ci_ref[...]

    t16 = lax.broadcasted_iota(jnp.int32, (S5_CHUNK, 1), 0)
    tv16 = jnp.where(d == 0, t16, S5_CHUNK - 1 - t16).astype(F32)
    tile = lambda x: jnp.concatenate([x] * S5_CHUNK, axis=0)
    spread = (lax.broadcasted_iota(jnp.int32, (S5_ROW, S5_CHUNK), 0) // S5_GROUP
              == lax.broadcasted_iota(jnp.int32, (S5_ROW, S5_CHUNK), 1)).astype(BF16)
    rep = lambda x: _mm_exact_lhs(spread, x)
    pr, pi = apow(S5_CHUNK - 1.0 - tv16)
    xr, xi = cmul(tile(bbr), tile(bbi), rep(pr), rep(pi))
    winr_ref[...] = xr
    wini_ref[...] = xi
    pr, pi = apow(tv16 + 1.0)
    yr, yi = cmul(tile(cr), tile(ci), rep(pr), rep(pi))
    woutr_ref[...] = yr
    wouti_ref[...] = -yi

    nt = lambda p, q: lax.dot_general(p, q, (((1,), (1,)), ((), ())), precision=HIGHEST,
                                      preferred_element_type=F32)
    table = nt(xr, cr) - nt(xi, ci)
    expand = (lax.broadcasted_iota(jnp.int32, (S5_GROUP, S5_ROW), 0)
              == lax.broadcasted_iota(jnp.int32, (S5_GROUP, S5_ROW), 1) % S5_GROUP).astype(BF16)
    tab_scr[...] = jnp.zeros_like(tab_scr)
    tab_scr[pl.ds(S5_ROW, S5_ROW), :] = _mm_exact_rhs(table, expand)
    col_t = lax.broadcasted_iota(jnp.int32, (S5_ROW, S5_ROW), 1) // S5_GROUP
    m = jnp.zeros((S5_ROW, S5_ROW), F32)
    for t in range(S5_CHUNK):
        start = S5_ROW + jnp.where(d == 0, S5_CHUNK - 1 - t, -t) * S5_GROUP
        win = tab_scr[pl.ds(pl.multiple_of(start, S5_GROUP), S5_ROW), :]
        m = jnp.where(col_t == t, win, m)
    m_ref[...] = m
    a16r, a16i = apow(float(S5_CHUNK))
    a16r_ref[...] = a16r
    a16i_ref[...] = a16i


def _s5_ops_call(lam_re, lam_im, log_step, bt_re, bt_im, c_re, c_im):
    g, p, cg = S5_GROUPS, S5_STATE, S5_GROUP
    ls = jnp.broadcast_to(log_step[:, :, None, None], (2, g, 1, p))
    lam_spec = pl.BlockSpec((None, None, 1, p), lambda d, gi: (d, gi, 0, 0))
    par_spec = pl.BlockSpec((None, cg, p), lambda d, gi: (gi, 0, 0))
    out = lambda *shape: pl.BlockSpec((None, None) + shape, lambda d, gi: (d, gi, 0, 0))
    return pl.pallas_call(
        _s5_ops_kernel,
        grid=(2, g),
        in_specs=[lam_spec, lam_spec, lam_spec, par_spec, par_spec, par_spec, par_spec],
        out_specs=[out(S5_ROW, S5_ROW), out(S5_ROW, p), out(S5_ROW, p), out(S5_ROW, p), out(S5_ROW, p),
                   out(1, p), out(1, p)],
        out_shape=[jax.ShapeDtypeStruct((2, g, S5_ROW, S5_ROW), F32)]
        + [jax.ShapeDtypeStruct((2, g, S5_ROW, p), F32)] * 4
        + [jax.ShapeDtypeStruct((2, g, 1, p), F32)] * 2,
        scratch_shapes=[pltpu.VMEM((3 * S5_ROW, S5_ROW), F32)],
        compiler_params=_params("parallel", "parallel"),
        name="s5_ops",
    )(lam_re.reshape(2, g, 1, p), lam_im.reshape(2, g, 1, p), ls, bt_re, bt_im, c_re, c_im)


def _regroup_kernel(to_groups, rows, *refs):
    units = 128 // S5_GROUP
    cols_per_t = S5_WIDTH // 128
    cols_per_g = S5_ROW // 128
    tok_refs = refs[:cols_per_t] if to_groups else refs[1:]
    row_ref = refs[cols_per_t] if to_groups else refs[0]
    unit = lax.broadcasted_iota(jnp.int32, (rows, 128), 1) // S5_GROUP

    def token_slab(t, col):
        return tok_refs[col].at[pl.ds(t, rows, stride=S5_CHUNK), :]

    def gather(pieces, src_unit):
        acc = None
        for p in range(units):
            shift = ((p - src_unit[p]) * S5_GROUP) % 128
            piece = pltpu.roll(pieces[p], shift, axis=1) if shift else pieces[p]
            acc = piece if acc is None else jnp.where(unit == p, piece, acc)
        return acc

    if to_groups:
        for gh in range(cols_per_t):
            for t_hi in range(cols_per_g):
                slabs = [token_slab(units * t_hi + p, gh)[...] for p in range(units)]
                for q in range(units):
                    j = (units * gh + q) * cols_per_g + t_hi
                    row_ref[:, j * 128:(j + 1) * 128] = gather(slabs, [q] * units).astype(row_ref.dtype)
    else:
        for j in range(S5_CHUNK * cols_per_t):
            t, gh = j // cols_per_t, j % cols_per_t
            cols = [(units * gh + p) * cols_per_g + t // units for p in range(units)]
            pieces = [row_ref[:, col * 128:(col + 1) * 128] for col in cols]
            token_slab(t, gh)[...] = gather(pieces, [t % units] * units)


def _regroup_call(x, to_groups, dtype):
    b = x.shape[0]
    nc = x.shape[1] // S5_CHUNK if to_groups else x.shape[1]
    rows = min(S5_REGROUP_ROWS, nc)
    ncol = S5_WIDTH // 128
    row_spec = pl.BlockSpec((None, rows, S5_CHUNK * S5_WIDTH), lambda b, i: (b, i, 0))
    row_shape = jax.ShapeDtypeStruct((b, nc, S5_CHUNK * S5_WIDTH), dtype)
    if to_groups:
        in_specs = [pl.BlockSpec((None, rows * S5_CHUNK, 128), functools.partial(lambda b, i, col: (b, i, col), col=col))
                    for col in range(ncol)]
        args, out_specs, out_shape = [x] * ncol, row_spec, row_shape
    else:
        in_specs, args = [row_spec], [x]
        out_specs = [pl.BlockSpec((None, rows * S5_CHUNK, 128), lambda b, i: (b, i, 0))] * ncol
        out_shape = [jax.ShapeDtypeStruct((b, nc * S5_CHUNK, 128), dtype)] * ncol
    return pl.pallas_call(
        functools.partial(_regroup_kernel, to_groups, rows),
        grid=(b, nc // rows),
        in_specs=in_specs,
        out_specs=out_specs,
        out_shape=out_shape,
        compiler_params=_params("parallel", "parallel"),
        name="s5_to_groups" if to_groups else "s5_to_tokens",
    )(*args)


def _s5_in_kernel(u_ref, wr_ref, wi_ref, sr_ref, si_ref):
    dot = functools.partial(jnp.dot, preferred_element_type=F32)
    u0, u1 = u_ref[:, :S5_ROW], u_ref[:, S5_ROW:]
    sr_ref[...] = dot(u0, wr_ref[0]) + dot(u1, wr_ref[1])
    si_ref[...] = dot(u0, wi_ref[0]) + dot(u1, wi_ref[1])


def _s5_in_call(ug, win_r, win_i):
    b, nc, _ = ug.shape
    g = S5_GROUPS
    w_spec = pl.BlockSpec((None, 2, S5_ROW, 128), lambda d, b, gp: (d, gp, 0, 0))
    o_spec = pl.BlockSpec((None, None, nc, 128), lambda d, b, gp: (d, b, 0, gp))
    return pl.pallas_call(
        _s5_in_kernel,
        grid=(2, b, g // 2),
        in_specs=[pl.BlockSpec((None, nc, 2 * S5_ROW), lambda d, b, gp: (b, 0, gp)), w_spec, w_spec],
        out_specs=[o_spec, o_spec],
        out_shape=[jax.ShapeDtypeStruct((2, b, nc, S5_LANES), F32)] * 2,
        compiler_params=_params("parallel", "parallel", "parallel"),
        name="s5_in",
    )(ug, win_r, win_i)


def _s5_scan_kernel(nc, sr_ref, si_ref, ar_ref, ai_ref, h0r_ref, h0i_ref, hr_ref, hi_ref, fr_ref, fi_ref):
    coef = [(ar_ref[d], ai_ref[d]) for d in range(2)]

    def body(i, carry):
        out = []
        for d in range(2):
            hr, hi = carry[d]
            ar, ai = coef[d]
            n = i if d == 0 else nc - 1 - i
            hr_ref[d, pl.ds(n, 1), :] = hr
            hi_ref[d, pl.ds(n, 1), :] = hi
            sr = sr_ref[d, pl.ds(n, 1), :]
            si = si_ref[d, pl.ds(n, 1), :]
            out.append((ar * hr - ai * hi + sr, ar * hi + ai * hr + si))
        return tuple(out)

    fin = lax.fori_loop(0, nc, body, tuple((h0r_ref[d], h0i_ref[d]) for d in range(2)), unroll=4)
    for d in range(2):
        fr_ref[d] = fin[d][0]
        fi_ref[d] = fin[d][1]


def _s5_scan_call(s_re, s_im, a16_re, a16_im, h0_re, h0_im):
    _, b, nc, _ = s_re.shape
    tl = S5_SCAN_LANES
    big = pl.BlockSpec((2, None, nc, tl), lambda b, j: (0, b, 0, j))
    a_spec = pl.BlockSpec((2, 1, tl), lambda b, j: (0, 0, j))
    st_spec = pl.BlockSpec((2, None, 1, tl), lambda b, j: (0, b, 0, j))
    return pl.pallas_call(
        functools.partial(_s5_scan_kernel, nc),
        grid=(b, S5_LANES // tl),
        in_specs=[big, big, a_spec, a_spec, st_spec, st_spec],
        out_specs=[big, big, st_spec, st_spec],
        out_shape=[jax.ShapeDtypeStruct(s_re.shape, F32)] * 2
        + [jax.ShapeDtypeStruct((2, b, 1, S5_LANES), F32)] * 2,
        compiler_params=_params("parallel", "parallel"),
        name="s5_scan",
    )(s_re, s_im, a16_re, a16_im, h0_re, h0_im)


def _s5_out_kernel(u_ref, m_ref, hr_ref, hi_ref, wr_ref, wi_ref, y_ref):
    u = u_ref[...]
    acc = None
    for d in range(2):
        y = (jnp.dot(u, m_ref[d], preferred_element_type=F32)
             + _mm(hr_ref[d], wr_ref[d]) + _mm(hi_ref[d], wi_ref[d]))
        acc = y if acc is None else acc + y
    y_ref[...] = acc


def _s5_out_call(ug, m, h_re, h_im, wout_r, wout_i):
    b, nc, _ = ug.shape
    g = S5_GROUPS
    h_spec = pl.BlockSpec((2, None, nc, 128), lambda b, gi: (0, b, 0, gi // 2))
    w_spec = pl.BlockSpec((2, None, 128, S5_ROW), lambda b, gi: (0, gi, 0, 0))
    u_spec = pl.BlockSpec((None, nc, S5_ROW), lambda b, gi: (b, 0, gi))
    return pl.pallas_call(
        _s5_out_kernel,
        grid=(b, g),
        in_specs=[u_spec,
                  pl.BlockSpec((2, None, S5_ROW, S5_ROW), lambda b, gi: (0, gi, 0, 0)),
                  h_spec, h_spec, w_spec, w_spec],
        out_specs=u_spec,
        out_shape=jax.ShapeDtypeStruct(ug.shape, F32),
        compiler_params=_params("parallel", "parallel"),
        name="s5_out",
    )(ug, m, h_re, h_im, wout_r, wout_i)


def _s5_prepare_ops(lam_re, lam_im, log_step, b_re, b_im, c_re, c_im):
    m, win_r, win_i, wout_r, wout_i, a16_r, a16_i = _s5_ops_call(
        lam_re, lam_im, log_step, jnp.swapaxes(b_re, 1, 2), jnp.swapaxes(b_im, 1, 2), c_re, c_im)
    odd = (jnp.arange(S5_GROUPS) % 2 == 1)[None, :, None, None]

    def pad_in(w):
        z = jnp.zeros_like(w)
        return jnp.where(odd, jnp.concatenate([z, w], -1), jnp.concatenate([w, z], -1)).astype(BF16)

    def pad_out(w):
        wt = jnp.swapaxes(w, 2, 3)
        z = jnp.zeros_like(wt)
        return jnp.where(odd, jnp.concatenate([z, wt], 2), jnp.concatenate([wt, z], 2)).astype(BF16)

    return dict(m=m.astype(BF16), win_r=pad_in(win_r), win_i=pad_in(win_i),
                wout_r=pad_out(wout_r), wout_i=pad_out(wout_i),
                a16_r=a16_r.reshape(2, 1, S5_LANES), a16_i=a16_i.reshape(2, 1, S5_LANES))


def _s5_mix(u, ops, h0_re, h0_im):
    ug = _regroup_call(u, True, BF16)
    s_re, s_im = _s5_in_call(ug, ops["win_r"], ops["win_i"])
    h_re, h_im, f_re, f_im = _s5_scan_call(s_re, s_im, ops["a16_r"], ops["a16_i"], h0_re, h0_im)
    y = _s5_out_call(ug, ops["m"], h_re, h_im, ops["wout_r"], ops["wout_i"])
    y = _regroup_call(y, False, F32)
    return y, f_re, f_im


def _conv_kernel(rows, width, n_hor, has_ver, nt, *refs):
    if has_ver:
        (xh_ref, vp_ref, vc_ref, vn_ref, wh_ref, wv_ref, b_ref, g_ref, be_ref,
         o_ref, hs_scr, vs_scr) = refs
    else:
        xh_ref, wh_ref, b_ref, g_ref, be_ref, o_ref, hs_scr = refs
    i = pl.program_id(1)
    tokens = rows * width
    half = CONV_K // 2

    hs_scr[...] = jnp.zeros_like(hs_scr)
    for r in range(rows):
        hs_scr[r, pl.ds(CONV_PAD, width), :] = xh_ref[pl.ds(r * width, width), :]
    wh = wh_ref[...]
    acc_h = jnp.zeros((tokens, n_hor), F32)
    for tap in range(CONV_K):
        win = hs_scr[:, pl.ds(CONV_PAD - half + tap, width), :].reshape(tokens, n_hor)
        acc_h = acc_h + win * wh[tap:tap + 1, :]

    if has_ver:
        vs_scr[pl.ds(0, tokens), :] = jnp.where(i == 0, 0.0, vp_ref[...])
        vs_scr[pl.ds(tokens, tokens), :] = vc_ref[...]
        vs_scr[pl.ds(2 * tokens, tokens), :] = jnp.where(i == nt - 1, 0.0, vn_ref[...])
        wv = wv_ref[...]
        acc_v = jnp.zeros((tokens, CONV_WIDTH - n_hor), F32)
        for tap in range(CONV_K):
            acc_v = acc_v + vs_scr[pl.ds(tokens + (tap - half) * width, tokens), :] * wv[tap:tap + 1, :]
        x = jnp.concatenate([acc_h, acc_v], axis=1)
    else:
        x = acc_h
    x = x + b_ref[...]
    xc = x - jnp.mean(x, axis=-1, keepdims=True)
    y = xc * lax.rsqrt(jnp.mean(xc * xc, axis=-1, keepdims=True) + EPS) * g_ref[...] + be_ref[...]
    o_ref[...] = _silu(y)


def _conv_call(xc, dw_w, dw_b, ln_g, ln_b, grid_rows):
    b, seq, ch = xc.shape
    vec = pl.BlockSpec((1, ch), lambda b, i: (0, 0))
    if grid_rows > 0:
        width, rows, n_hor = GRID_W, CONV_ROWS, ch // 2
        tokens = rows * width
        nt = seq // tokens
        half_spec = lambda f: pl.BlockSpec((None, tokens, n_hor), f)
        in_specs = [half_spec(lambda b, i: (b, i, 0)),
                    half_spec(lambda b, i: (b, jnp.maximum(i - 1, 0), 1)),
                    half_spec(lambda b, i: (b, i, 1)),
                    half_spec(lambda b, i: (b, jnp.minimum(i + 1, nt - 1), 1)),
                    pl.BlockSpec((CONV_K, n_hor), lambda b, i: (0, 0)),
                    pl.BlockSpec((CONV_K, n_hor), lambda b, i: (0, 1)),
                    vec, vec, vec]
        args = [xc, xc, xc, xc, dw_w, dw_w, dw_b, ln_g, ln_b]
        scratch = [pltpu.VMEM((rows, width + 2 * CONV_PAD, n_hor), F32),
                   pltpu.VMEM((3 * tokens, ch - n_hor), F32)]
        has_ver = True
    else:
        width, rows, n_hor = seq, 1, ch
        tokens = seq
        nt = 1
        in_specs = [pl.BlockSpec((None, tokens, ch), lambda b, i: (b, 0, 0)),
                    pl.BlockSpec((CONV_K, ch), lambda b, i: (0, 0)),
                    vec, vec, vec]
        args = [xc, dw_w, dw_b, ln_g, ln_b]
        scratch = [pltpu.VMEM((rows, width + 2 * CONV_PAD, n_hor), F32)]
        has_ver = False
    return pl.pallas_call(
        functools.partial(_conv_kernel, rows, width, n_hor, has_ver, nt),
        grid=(b, nt),
        in_specs=in_specs,
        out_specs=pl.BlockSpec((None, tokens, ch), lambda b, i: (b, i, 0)),
        out_shape=jax.ShapeDtypeStruct((b, seq, ch), F32),
        scratch_shapes=scratch,
        compiler_params=_params("parallel", "parallel"),
        name="conformer_conv",
    )(*args)


def _lb_kernel(x_ref, o_ref):
    x = x_ref[...]
    n = x.shape[0]
    rows = [x[r:r + 1, :] for r in range(n)]
    mx = functools.reduce(jnp.maximum, rows)
    ex = [jnp.exp(r - mx) for r in rows]
    tot = functools.reduce(lambda p, q: p + q, ex)
    run = None
    for r in range(n):
        run = ex[r] / tot if run is None else run + ex[r] / tot
        o_ref[pl.ds(r, 1), :] = run


def _lb_call(logits):
    return pl.pallas_call(_lb_kernel, out_shape=jax.ShapeDtypeStruct(logits.shape, F32), name="hgrn_lb")(logits)


def _hgrn_kernel(rev, t_blk, nb, q_ref, f_ref, v_ref, lb_ref, s0_ref, o_ref, sfin_ref,
                 st_scr, kv_scr, sall_scr):
    i = pl.program_id(1)
    c = HGRN_CHUNK
    nch = t_blk // c
    hh = HGRN_HEADS

    @pl.when(i == 0)
    def _():
        st_scr[...] = s0_ref[...]

    lb = lb_ref[...]
    f = lb + (1.0 - lb) * jax.nn.sigmoid(f_ref[...])
    k = 1.0 - f
    log_f = jnp.log(f)
    row = lax.broadcasted_iota(jnp.int32, (t_blk, t_blk), 0)
    col = lax.broadcasted_iota(jnp.int32, (t_blk, t_blk), 1)
    same = _same_block(row, col, c)
    incl = same & ((row <= col) if rev else (row >= col))
    bc = _mm_exact_lhs(incl.astype(BF16), log_f)
    last = 0 if rev else c - 1
    bc3 = bc.reshape(nch, c, bc.shape[1])
    tot = jnp.broadcast_to(bc3[:, last:last + 1, :], bc3.shape).reshape(bc.shape)
    q_in = q_ref[...] * jnp.exp(bc)
    k_in = k * jnp.exp(-bc)
    k_out = k * jnp.exp(tot - bc)
    cd = jnp.exp(tot)
    v = v_ref[...]
    order = range(nch - 1, -1, -1) if rev else range(nch)
    heads = range(hh)
    ls = [slice(hd * HGRN_DIM, (hd + 1) * HGRN_DIM) for hd in heads]
    attn = [jnp.where(incl, _mm_nt(q_in[:, ls[hd]], k_in[:, ls[hd]]), 0.0) for hd in heads]
    o_intra = [_mm(attn[hd], v[:, ls[hd]]) for hd in heads]
    for n in order:
        rows = slice(n * c, (n + 1) * c)
        for hd in heads:
            kv_scr[hd, n] = _mm_tn(v[rows, ls[hd]], k_out[rows, ls[hd]])
    st = [st_scr[hd] for hd in heads]
    for n in order:
        for hd in heads:
            sall_scr[hd, n] = st[hd]
            st[hd] = st[hd] * cd[n * c:n * c + 1, ls[hd]] + kv_scr[hd, n]
    for hd in heads:
        st_scr[hd] = st[hd]
    grp = 4
    blk = lax.broadcasted_iota(jnp.int32, (grp * c, HGRN_DIM), 0) // c
    for n in (range(nch - grp, -1, -grp) if rev else range(0, nch, grp)):
        rows = slice(n * c, (n + grp) * c)
        for hd in heads:
            qg = q_in[rows, ls[hd]]
            q_bd = jnp.concatenate([jnp.where(blk == j, qg, 0.0) for j in range(grp)], axis=1)
            s_cat = jnp.concatenate([sall_scr[hd, n + j] for j in range(grp)], axis=1)
            o_ref[pl.ds(n * c, grp * c), pl.ds(hd * HGRN_DIM, HGRN_DIM)] = _mm_nt(q_bd, s_cat) + o_intra[hd][rows]

    @pl.when(i == nb - 1)
    def _():
        sfin_ref[...] = st_scr[...]


def _hgrn_call(q, fgate, v, lb_row, s0, rev):
    b, seq, _ = q.shape
    t_blk = min(HGRN_BLOCK, seq)
    nb = seq // t_blk
    hh = HGRN_HEADS

    def blk(i):
        return (nb - 1 - i) if rev else i

    tok = pl.BlockSpec((None, t_blk, HGRN_WIDTH), lambda b, i: (b, blk(i), 0))
    state_spec = pl.BlockSpec((None, hh, HGRN_DIM, HGRN_DIM), lambda b, i: (b, 0, 0, 0))
    chunk_states = pltpu.VMEM((hh, t_blk // HGRN_CHUNK, HGRN_DIM, HGRN_DIM), F32)
    return pl.pallas_call(
        functools.partial(_hgrn_kernel, rev, t_blk, nb),
        grid=(b, nb),
        in_specs=[tok, tok, tok, pl.BlockSpec((1, HGRN_WIDTH), lambda b, i: (0, 0)), state_spec],
        out_specs=[tok, state_spec],
        out_shape=[jax.ShapeDtypeStruct((b, seq, HGRN_WIDTH), F32),
                   jax.ShapeDtypeStruct((b, hh, HGRN_DIM, HGRN_DIM), F32)],
        scratch_shapes=[pltpu.VMEM((hh, HGRN_DIM, HGRN_DIM), F32), chunk_states, chunk_states],
        compiler_params=_params("parallel", "arbitrary"),
        name="hgrn_bwd" if rev else "hgrn_fwd",
    )(q, fgate, v, lb_row, s0)


def _mixer_ab(h, hc, mod, layer, need_ctx, prm):
    ng = prm["norm_g1"]
    outs = {}
    gdn_state = [jnp.zeros((h.shape[0], GDN_HEADS, GDN_DIM, GDN_DIM), F32)] * 2
    s5_state = [jnp.zeros((2, h.shape[0], 1, S5_LANES), F32)] * 2
    for ctx, x in ((True, hc), (False, h)):
        q, k, v, z, u, gates = _inproj_ab_call(x, mod, layer, ctx, ng, prm["w_in"], prm["conv_w"])
        o_f, sf = _gdn_call(q, k, v, gates, prm["alog"], prm["dtb"], gdn_state[0], rev=False)
        o_b, sb = _gdn_call(q, k, v, gates, prm["alog"], prm["dtb"], gdn_state[1], rev=True)
        y5, f_re, f_im = _s5_mix(u, prm["s5"], s5_state[0], s5_state[1])
        gdn_state = [sf, sb]
        s5_state = [f_re, f_im]
        if ctx and not need_ctx:
            continue
        outs[ctx] = ("ab", [o_f, o_b, z, *y5, u],
                     [prm["gdn_g"], prm["s5_d"], prm["glu_w"], prm["glu_b"], prm["w_out"]])
    return outs[False], outs.get(True)


def _mixer_cd(h, hc, mod, layer, need_ctx, prm, grid_rows):
    ng = prm["norm_g1"]
    outs = {}
    state = [jnp.zeros((h.shape[0], HGRN_HEADS, HGRN_DIM, HGRN_DIM), F32)] * 2
    for ctx, x in ((True, hc), (False, h)):
        xc, q, f_f, f_b, iv, g = _inproj_call(x, mod, layer, ctx, ng, prm["w_in"], prm["pieces"], glu=True)
        o_f, sf = _hgrn_call(q, f_f, iv, prm["lb"], state[0], rev=False)
        o_b, sb = _hgrn_call(q, f_b, iv, prm["lb"], state[1], rev=True)
        state = [sf, sb]
        if ctx and not need_ctx:
            continue
        cmix = _conv_call(xc, prm["dw_w"], prm["dw_b"], prm["ln_g"], prm["ln_b"], 0 if ctx else grid_rows)
        outs[ctx] = ("cd", [cmix, o_f, o_b, g], [prm["hgrn_g"], prm["w_out"]])
    return outs[False], outs.get(True)


def kernel(x, c, ctx, c_ctx, ada_w, ada_b, norm_g, ffn_w_up, ffn_w_down, ab_w_in, ab_w_out, gdn_conv_w, gdn_a_log, gdn_dt_bias, gdn_norm_g, s5_lambda_re, s5_lambda_im, s5_log_step, s5_b_re, s5_b_im, s5_c_re, s5_c_im, s5_d, s5_glu_w, s5_glu_b, cd_w_in, cd_w_out, conv_dw_w, conv_dw_b, conv_ln_g, conv_ln_b, hgrn_lb_logits, hgrn_norm_g, final_norm_g):
    depth = ada_w.shape[0]
    batch = x.shape[0]
    grid_rows = x.shape[1] // GRID_W
    assert batch <= 2 and x.shape[1] % 1024 == 0 and ctx.shape[1] % 256 == 0

    cvec = jnp.zeros((8, D_MODEL), F32).at[:batch].set(c).at[2].set(c_ctx)
    mod = _ada_call(cvec, ada_w, ada_b)
    lb_all = _lb_call(hgrn_lb_logits)
    row = lambda v: v.reshape(1, -1)
    gw, kw = GDN_WIDTH, HGRN_WIDTH

    w_up = ffn_w_up.astype(BF16)
    w_down = ffn_w_down.astype(BF16)
    h, hc = x, ctx
    for l in range(depth):
        last = l == depth - 1
        h = _ffn_call(h, mod, l, 0, False, row(norm_g[l, 0]), w_up, w_down, (l, 0))
        hc = _ffn_call(hc, mod, l, 0, True, row(norm_g[l, 0]), w_up, w_down, (l, 0))
        if l % 2 == 0:
            e = l // 2
            w = ab_w_in[e]
            ng4 = 4 * GDN_HEADS
            w = jnp.concatenate([w[:, :4 * gw], w[:, 4 * gw + ng4:], w[:, 4 * gw:4 * gw + ng4],
                                 jnp.zeros((D_MODEL, 128 - ng4), F32)], axis=1).astype(BF16)
            pad8 = lambda v: jnp.zeros((1, 128), F32).at[0, :2 * GDN_HEADS].set(v.reshape(-1))
            prm = dict(
                norm_g1=row(norm_g[l, 1]),
                w_in=w, conv_w=gdn_conv_w[e], alog=pad8(gdn_a_log[e]), dtb=pad8(gdn_dt_bias[e]),
                gdn_g=row(gdn_norm_g[e]),
                s5=_s5_prepare_ops(s5_lambda_re[e], s5_lambda_im[e], s5_log_step[e], s5_b_re[e], s5_b_im[e],
                                   s5_c_re[e], s5_c_im[e]),
                s5_d=row(s5_d[e]), glu_w=s5_glu_w[e].astype(BF16), glu_b=row(s5_glu_b[e]),
                w_out=ab_w_out[e].astype(BF16))
            mix, mix_ctx = _mixer_ab(h, hc, mod, l, not last, prm)
        else:
            o = l // 2
            cw = CONV_WIDTH
            prm = dict(
                norm_g1=row(norm_g[l, 1]),
                w_in=cd_w_in[o].astype(BF16),
                pieces=[(k * cw, cw) for k in range(2)] + [(2 * cw + k * kw, kw) for k in range(5)],
                lb=lb_all[o:o + 1], hgrn_g=row(hgrn_norm_g[o]),
                dw_w=conv_dw_w[o], dw_b=row(conv_dw_b[o]), ln_g=row(conv_ln_g[o]), ln_b=row(conv_ln_b[o]),
                w_out=cd_w_out[o].astype(BF16))
            mix, mix_ctx = _mixer_cd(h, hc, mod, l, not last, prm, grid_rows)
        h = _ffn_call(h, mod, l, 2, False, row(norm_g[l, 2]), w_up, w_down, (l, 1),
                      final_g=row(final_norm_g) if last else None, mixer=mix)
        if not last:
            hc = _ffn_call(hc, mod, l, 2, True, row(norm_g[l, 2]), w_up, w_down, (l, 1), mixer=mix_ctx)
    return h
```
